```python
import jax, jax.numpy as jnp
from jax import lax
import numpy as np

D_MODEL = 1024
BATCH = 4
SEQ = 4096
DEPTH = 2

HEAD_DIM = 64
N_FOX_HEADS = 8
N_MOBA_HEADS = 8
FOX_WIDTH = N_FOX_HEADS * HEAD_DIM
MOBA_WIDTH = N_MOBA_HEADS * HEAD_DIM
ATTN_WIDTH = FOX_WIDTH + MOBA_WIDTH
IN_COLS = 3 * ATTN_WIDTH + N_FOX_HEADS
Q_BLOCK = 128
MOBA_BLOCK = 256
MOBA_TOPK = 3
MOBA_Q_CHUNK = 32
N_RWKV_HEADS = D_MODEL // HEAD_DIM
D_DECAY_LORA = 64
D_AAA_LORA = 64
D_GATE_LORA = 160
D_FF = -(-8 * D_MODEL // (3 * 256)) * 256
NORM_EPS = 1e-6
GN_EPS = HEAD_DIM * 1e-5
N_EVEN = (DEPTH + 1) // 2
N_ODD = DEPTH // 2

kernel_name = "fox_moba_rwkv7_adaln_hybrid"

F32 = jnp.float32


def rmsnorm(x, g):
    xf = x.astype(F32)
    y = xf * lax.rsqrt(jnp.mean(xf * xf, axis=-1, keepdims=True) + NORM_EPS)
    return y.astype(x.dtype) * g


def modulate(h, shift, scale):
    return h * (1.0 + scale[:, None, :]) + shift[:, None, :]


def split_heads(t, n_heads):
    b, s, _ = t.shape
    return t.reshape(b, s, n_heads, HEAD_DIM).transpose(0, 2, 1, 3)


def fox_attention(q, k, v, f_logit):
    b, h, s, dh = q.shape
    cum = jnp.cumsum(jax.nn.log_sigmoid(f_logit.astype(F32)), axis=-1)
    q = q * (dh ** -0.5)
    outs = []
    for i in range(s // Q_BLOCK):
        lo, hi = i * Q_BLOCK, (i + 1) * Q_BLOCK
        logits = jnp.einsum('bhqd,bhkd->bhqk', q[:, :, lo:hi], k[:, :, :hi]).astype(F32)
        logits = logits + cum[:, :, lo:hi, None] - cum[:, :, None, :hi]
        causal = jnp.arange(lo, hi)[:, None] >= jnp.arange(hi)[None, :]
        p = jax.nn.softmax(jnp.where(causal, logits, -jnp.inf), axis=-1)
        outs.append(jnp.einsum('bhqk,bhkd->bhqd', p.astype(v.dtype), v[:, :, :hi]))
    return jnp.concatenate(outs, axis=2)


def moba_attention(q, k, v):
    b, h, s, dh = q.shape
    nb = -(-s // MOBA_BLOCK)
    pad = nb * MOBA_BLOCK - s
    padw = ((0, 0), (0, 0), (0, pad), (0, 0))
    k_blocks = jnp.pad(k, padw).reshape(b, h, nb, MOBA_BLOCK, dh)
    v_blocks = jnp.pad(v, padw).reshape(b, h, nb, MOBA_BLOCK, dh)
    q = q * (dh ** -0.5)
    k_mean = jnp.mean(k_blocks.astype(F32), axis=3)
    q_blk = jnp.arange(s) // MOBA_BLOCK
    gate = jnp.einsum('bhsd,bhnd->bhsn', q.astype(F32), k_mean)
    fully_past = jnp.arange(nb)[None, :] < q_blk[:, None]
    gate = jnp.where(fully_past, gate, -jnp.inf)
    n_sel = min(MOBA_TOPK, nb)
    _, sel_idx = lax.top_k(gate, n_sel)
    sel_valid = jnp.arange(n_sel)[None, :] < q_blk[:, None]

    nc = s // MOBA_Q_CHUNK
    q_c = jnp.moveaxis(q.reshape(b, h, nc, MOBA_Q_CHUNK, dh), 2, 0)
    idx_c = jnp.moveaxis(sel_idx.reshape(b, h, nc, MOBA_Q_CHUNK, n_sel), 2, 0)
    valid_c = sel_valid.reshape(nc, MOBA_Q_CHUNK, n_sel)
    gather = jax.vmap(jax.vmap(lambda blocks, idx: blocks[idx]))

    def chunk(args):
        qq, ii, vv, ci = args
        qpos = ci * MOBA_Q_CHUNK + jnp.arange(MOBA_Q_CHUNK)
        own = (ci * MOBA_Q_CHUNK) // MOBA_BLOCK
        kg = gather(k_blocks, ii)
        vg = gather(v_blocks, ii)
        s_sel = jnp.einsum('bhqd,bhqnkd->bhqnk', qq, kg).astype(F32)
        s_sel = jnp.where(vv[None, None, :, :, None], s_sel, -jnp.inf)
        k_own = lax.dynamic_index_in_dim(k_blocks, own, axis=2, keepdims=False)
        v_own = lax.dynamic_index_in_dim(v_blocks, own, axis=2, keepdims=False)
        s_own = jnp.einsum('bhqd,bhkd->bhqk', qq, k_own).astype(F32)
        own_pos = own * MOBA_BLOCK + jnp.arange(MOBA_BLOCK)
        s_own = jnp.where(own_pos[None, :] <= qpos[:, None], s_own, -jnp.inf)
        logits = jnp.concatenate([s_sel.reshape(b, h, MOBA_Q_CHUNK, n_sel * MOBA_BLOCK), s_own], axis=-1)
        p = jax.nn.softmax(logits, axis=-1).astype(v.dtype)
        p_sel = p[..., :n_sel * MOBA_BLOCK].reshape(b, h, MOBA_Q_CHUNK, n_sel, MOBA_BLOCK)
        p_own = p[..., n_sel * MOBA_BLOCK:]
        return (jnp.einsum('bhqnk,bhqnkd->bhqd', p_sel, vg)
                + jnp.einsum('bhqk,bhkd->bhqd', p_own, v_own))

    out = lax.map(chunk, (q_c, idx_c, valid_c, jnp.arange(nc)))
    return jnp.moveaxis(out, 0, 2).reshape(b, h, s, dh)


def parallel_attention(h, w_in, b_f, w_out):
    b, s, _ = h.shape
    proj = h @ w_in
    qa, ka, va, qb, kb, vb = [proj[..., i * FOX_WIDTH:(i + 1) * FOX_WIDTH] for i in range(6)]
    f_logit = (proj[..., 3 * ATTN_WIDTH:] + b_f).transpose(0, 2, 1)
    o_a = fox_attention(split_heads(qa, N_FOX_HEADS), split_heads(ka, N_FOX_HEADS),
                        split_heads(va, N_FOX_HEADS), f_logit)
    o_b = moba_attention(split_heads(qb, N_MOBA_HEADS), split_heads(kb, N_MOBA_HEADS),
                         split_heads(vb, N_MOBA_HEADS))
    o = jnp.concatenate([o_a, o_b], axis=1)
    return o.transpose(0, 2, 1, 3).reshape(b, s, ATTN_WIDTH) @ w_out


def rwkv7_step(state, inp):
    r_t, w_t, k_t, v_t, a_t, b_t = inp
    sa = jnp.einsum('bhij,bhj->bhi', state, a_t)
    state = (state * w_t[:, :, None, :] + sa[..., None] * b_t[:, :, None, :]
             + v_t[..., None] * k_t[:, :, None, :])
    return state, jnp.einsum('bhij,bhj->bhi', state, r_t)


def rwkv7_time_mix(h, mu, w_r, w_k, w_v, w_o, w0, w1, w2, a0, a1, a2, g1, g2,
                   k_k, k_a, r_k, ln_w, ln_b):
    b, s, d = h.shape
    nh, n = N_RWKV_HEADS, HEAD_DIM
    xx = jnp.pad(h, ((0, 0), (1, 0), (0, 0)))[:, :-1] - h
    xr, xw, xk, xv, xa, xg = [h + xx * mu[i] for i in range(6)]
    r = xr @ w_r
    k = xk @ w_k
    v = xv @ w_v
    w_log = -jax.nn.softplus(-(w0 + jnp.tanh(xw @ w1) @ w2).astype(F32)) - 0.5
    decay = jnp.exp(-jnp.exp(w_log))
    a = jax.nn.sigmoid(a0 + (xa @ a1) @ a2)
    g = jax.nn.sigmoid(xg @ g1) @ g2
    kk = (k * k_k).reshape(b, s, nh, n).astype(F32)
    kk = kk / jnp.maximum(jnp.sqrt(jnp.sum(kk * kk, axis=-1, keepdims=True)), 1e-12)
    k = k * (1.0 + (a - 1.0) * k_a)
    heads = lambda t: t.reshape(b, s, nh, n).astype(F32)
    r_h, k_h, v_h, a_h, dec_h = heads(r), heads(k), heads(v), heads(a), heads(decay)
    xs = tuple(jnp.moveaxis(t, 1, 0) for t in (r_h, dec_h, k_h, v_h, -kk, kk * a_h))
    state0 = jnp.zeros((b, nh, n, n), F32)
    _, y = lax.scan(rwkv7_step, state0, xs)
    y = jnp.moveaxis(y, 0, 1)
    mean = jnp.mean(y, axis=-1, keepdims=True)
    var = jnp.mean(jnp.square(y - mean), axis=-1, keepdims=True)
    y = (y - mean) * lax.rsqrt(var + GN_EPS) * ln_w.reshape(nh, n).astype(F32) + ln_b.reshape(nh, n).astype(F32)
    bonus = jnp.sum(r_h * k_h * r_k.astype(F32), axis=-1, keepdims=True) * v_h
    out = (y + bonus).reshape(b, s, d).astype(h.dtype) * g
    return out @ w_o


def swiglu(h, w_in, w_out):
    gate, up = jnp.split(h @ w_in, 2, axis=-1)
    return (jax.nn.silu(gate) * up) @ w_out


def setup_inputs(seed: int = 0) -> dict:
    key = jax.random.key(seed)
    ks = iter(jax.random.split(key, 40))
    D = D_MODEL
    nrm = lambda shape, scale: jax.random.normal(next(ks), shape, F32) * scale
    uni = lambda shape, lo, hi: jax.random.uniform(next(ks), shape, F32, lo, hi)
    return {
        "x": nrm((BATCH, SEQ, D), 1.0),
        "c": nrm((BATCH, D), 1.0),
        "attn_w_in": nrm((N_EVEN, D, IN_COLS), D ** -0.5),
        "attn_b_f": uni((N_EVEN, N_FOX_HEADS), 1.0, 6.0),
        "attn_w_out": nrm((N_EVEN, ATTN_WIDTH, D), ATTN_WIDTH ** -0.5),
        "rwkv_mu": uni((N_ODD, 6, D), 0.0, 1.0),
        "rwkv_w_r": nrm((N_ODD, D, D), D ** -0.5),
        "rwkv_w_k": nrm((N_ODD, D, D), D ** -0.5),
        "rwkv_w_v": nrm((N_ODD, D, D), D ** -0.5),
        "rwkv_w_o": nrm((N_ODD, D, D), D ** -0.5),
        "rwkv_w0": uni((N_ODD, D), -6.0, -1.0),
        "rwkv_w1": nrm((N_ODD, D, D_DECAY_LORA), D ** -0.5),
        "rwkv_w2": nrm((N_ODD, D_DECAY_LORA, D), 0.5 * D_DECAY_LORA ** -0.5),
        "rwkv_a0": nrm((N_ODD, D), 0.1),
        "rwkv_a1": nrm((N_ODD, D, D_AAA_LORA), D ** -0.5),
        "rwkv_a2": nrm((N_ODD, D_AAA_LORA, D), 0.5 * D_AAA_LORA ** -0.5),
        "rwkv_g1": nrm((N_ODD, D, D_GATE_LORA), D ** -0.5),
        "rwkv_g2": nrm((N_ODD, D_GATE_LORA, D), D_GATE_LORA ** -0.5),
        "rwkv_k_k": 0.85 + nrm((N_ODD, D), 0.05),
        "rwkv_k_a": 1.0 + nrm((N_ODD, D), 0.05),
        "rwkv_r_k": nrm((N_ODD, N_RWKV_HEADS, HEAD_DIM), 0.1),
        "rwkv_ln_w": 1.0 + nrm((N_ODD, D), 0.05),
        "rwkv_ln_b": nrm((N_ODD, D), 0.02),
        "mod_w": nrm((DEPTH, D, 6 * D), 0.5 * D ** -0.5),
        "mod_b": nrm((DEPTH, 6 * D), 0.02),
        "norm_mix": 1.0 + nrm((DEPTH, D), 0.05),
        "norm_ffn": 1.0 + nrm((DEPTH, D), 0.05),
        "ffn_w_in": nrm((DEPTH, D, 2 * D_FF), D ** -0.5),
        "ffn_w_out": nrm((DEPTH, D_FF, D), D_FF ** -0.5),
        "norm_final": 1.0 + nrm((D,), 0.05),
    }


def reference(x, c, attn_w_in, attn_b_f, attn_w_out, rwkv_mu, rwkv_w_r, rwkv_w_k, rwkv_w_v,
              rwkv_w_o, rwkv_w0, rwkv_w1, rwkv_w2, rwkv_a0, rwkv_a1, rwkv_a2, rwkv_g1, rwkv_g2,
              rwkv_k_k, rwkv_k_a, rwkv_r_k, rwkv_ln_w, rwkv_ln_b, mod_w, mod_b, norm_mix,
              norm_ffn, ffn_w_in, ffn_w_out, norm_final):
    cond = jax.nn.silu(c)
    for i in range(DEPTH):
        mod = cond @ mod_w[i] + mod_b[i]
        sh_m, sc_m, gt_m, sh_f, sc_f, gt_f = jnp.split(mod, 6, axis=-1)
        h = modulate(rmsnorm(x, norm_mix[i]), sh_m, sc_m)
        j = i // 2
        if i % 2 == 0:
            y = parallel_attention(h, attn_w_in[j], attn_b_f[j], attn_w_out[j])
        else:
            y = rwkv7_time_mix(h, rwkv_mu[j], rwkv_w_r[j], rwkv_w_k[j], rwkv_w_v[j], rwkv_w_o[j],
                               rwkv_w0[j], rwkv_w1[j], rwkv_w2[j], rwkv_a0[j], rwkv_a1[j],
                               rwkv_a2[j], rwkv_g1[j], rwkv_g2[j], rwkv_k_k[j], rwkv_k_a[j],
                               rwkv_r_k[j], rwkv_ln_w[j], rwkv_ln_b[j])
        x = x + gt_m[:, None, :] * y
        h = modulate(rmsnorm(x, norm_ffn[i]), sh_f, sc_f)
        x = x + gt_f[:, None, :] * swiglu(h, ffn_w_in[i], ffn_w_out[i])
    return rmsnorm(x, norm_final)
```

```python
import functools

import jax
import jax.numpy as jnp
from jax import lax
from jax.experimental import pallas as pl
from jax.experimental.pallas import tpu as pltpu

F32 = jnp.float32
BF16 = jnp.bfloat16

D = 1024
HEAD = 64
PAIR = 2 * HEAD
N_PAIR_ATTN = 8
FOX_W = 512
D_FF = 2816
MOBA_BLOCK = 256
N_MOBA_BLOCKS = 16
MOBA_TOPK = 3
AUG = 128
AUG_PER_HEAD = 16
NORM_EPS = 1e-6
GN_EPS = HEAD * 1e-5
NEG = -1e30
PEN = -30000.0
CHUNK = 64
VMEM_LIMIT = 56 * 1024 * 1024


def _mm(a, b):
    return jnp.dot(a.astype(BF16), b.astype(BF16), preferred_element_type=F32)


def _mm_nt(a, b):
    return lax.dot_general(a.astype(BF16), b.astype(BF16), (((1,), (1,)), ((), ())),
                           preferred_element_type=F32)


def _mm_tn(a, b):
    return lax.dot_general(a.astype(BF16), b.astype(BF16), (((0,), (0,)), ((), ())),
                           preferred_element_type=F32)


def _mm_split3(a, b_exact):
    a1 = a.astype(BF16)
    r1 = a - a1.astype(F32)
    a2 = r1.astype(BF16)
    a3 = (r1 - a2.astype(F32)).astype(BF16)
    bb = b_exact.astype(BF16)
    return (jnp.dot(a1, bb, preferred_element_type=F32)
            + jnp.dot(a2, bb, preferred_element_type=F32)
            + jnp.dot(a3, bb, preferred_element_type=F32))


def _sigmoid(x):
    return 1.0 / (1.0 + jnp.exp(-x))


def _softplus(x):
    return jnp.maximum(x, 0.0) + jnp.log(1.0 + jnp.exp(-jnp.abs(x)))


def _norm_mod(x, g, sh, sc):
    y = x * lax.rsqrt(jnp.mean(x * x, axis=-1, keepdims=True) + NORM_EPS)
    return (y * g) * (1.0 + sc) + sh


def _mod_kernel(c_ref, w_ref, b_ref, o_ref):
    c = c_ref[...]
    cond = c * _sigmoid(c)
    o_ref[0] = jnp.dot(cond, w_ref[0], precision=lax.Precision.HIGHEST,
                       preferred_element_type=F32) + b_ref[0]


def _modulation(c, mod_w, mod_b):
    depth, _, n = mod_w.shape
    b = c.shape[0]
    rows = 8
    c_pad = jnp.zeros((rows, D), F32).at[:b].set(c)
    tn = 1536
    out = pl.pallas_call(
        _mod_kernel,
        grid=(depth, n // tn),
        in_specs=[pl.BlockSpec((rows, D), lambda l, j: (0, 0)),
                  pl.BlockSpec((1, D, tn), lambda l, j: (l, 0, j)),
                  pl.BlockSpec((1, 1, tn), lambda l, j: (l, 0, j))],
        out_specs=pl.BlockSpec((1, rows, tn), lambda l, j: (l, 0, j)),
        out_shape=jax.ShapeDtypeStruct((depth, rows, n), F32),
        compiler_params=pltpu.CompilerParams(
            dimension_semantics=("arbitrary", "arbitrary"), vmem_limit_bytes=VMEM_LIMIT),
        name="mod",
    )(c_pad, mod_w, mod_b.reshape(depth, 1, n))
    return out[:, :b]


def _inproj_kernel(x_ref, g_ref, sh_ref, sc_ref, w_ref, wf_ref, bf_ref,
                   proj_ref, c1_ref, c2_ref, c3_ref, carry_ref):
    tm = x_ref.shape[1]

    @pl.when(pl.program_id(1) == 0)
    def _():
        carry_ref[...] = jnp.zeros_like(carry_ref)

    h = _norm_mod(x_ref[0], g_ref[...], sh_ref[0], sc_ref[0]).astype(BF16)
    proj_ref[0] = jnp.dot(h, w_ref[...], preferred_element_type=F32).astype(BF16)
    f = jnp.dot(h, wf_ref[...], preferred_element_type=F32) + bf_ref[...]
    lf = -_softplus(-f)
    row = lax.broadcasted_iota(jnp.int32, (tm, tm), 0)
    col = lax.broadcasted_iota(jnp.int32, (tm, tm), 1)
    tri = jnp.where(row >= col, 1.0, 0.0).astype(F32)
    cum = _mm_split3_lhs_exact(tri, lf) + carry_ref[...]
    carry_ref[...] = cum[tm - 1:tm, :]
    c1 = cum.astype(BF16)
    r1 = cum - c1.astype(F32)
    c2 = r1.astype(BF16)
    c1_ref[0] = c1
    c2_ref[0] = c2
    c3_ref[0] = (r1 - c2.astype(F32)).astype(BF16)


def _mm_split3_lhs_exact(a_exact, b):
    b1 = b.astype(BF16)
    r1 = b - b1.astype(F32)
    b2 = r1.astype(BF16)
    b3 = (r1 - b2.astype(F32)).astype(BF16)
    aa = a_exact.astype(BF16)
    return (jnp.dot(aa, b1, preferred_element_type=F32)
            + jnp.dot(aa, b2, preferred_element_type=F32)
            + jnp.dot(aa, b3, preferred_element_type=F32))


def _inproj(x, g, sh, sc, w_main, w_f, b_f):
    b, s, _ = x.shape
    tm = 256
    n = w_main.shape[1]
    row3 = lambda bi, si: (bi, 0, 0)
    const2 = lambda bi, si: (0, 0)
    tile3 = lambda bi, si: (bi, si, 0)
    return pl.pallas_call(
        _inproj_kernel,
        grid=(b, s // tm),
        in_specs=[pl.BlockSpec((1, tm, D), tile3),
                  pl.BlockSpec((1, D), const2),
                  pl.BlockSpec((1, 1, D), row3),
                  pl.BlockSpec((1, 1, D), row3),
                  pl.BlockSpec((D, n), const2),
                  pl.BlockSpec((D, AUG), const2),
                  pl.BlockSpec((1, AUG), const2)],
        out_specs=[pl.BlockSpec((1, tm, n), tile3),
                   pl.BlockSpec((1, tm, AUG), tile3),
                   pl.BlockSpec((1, tm, AUG), tile3),
                   pl.BlockSpec((1, tm, AUG), tile3)],
        out_shape=[jax.ShapeDtypeStruct((b, s, n), BF16),
                   jax.ShapeDtypeStruct((b, s, AUG), BF16),
                   jax.ShapeDtypeStruct((b, s, AUG), BF16),
                   jax.ShapeDtypeStruct((b, s, AUG), BF16)],
        scratch_shapes=[pltpu.VMEM((1, AUG), F32)],
        compiler_params=pltpu.CompilerParams(
            dimension_semantics=("arbitrary", "arbitrary"), vmem_limit_bytes=VMEM_LIMIT),
        name="inproj",
    )(x, g, sh, sc, w_main, w_f, b_f)


def _kmean_kernel(k_ref, o_ref):
    k = k_ref[0].astype(F32)
    o_ref[0, 0] = jnp.mean(k, axis=0, keepdims=True)


def _moba_kmean(proj):
    b, s, _ = proj.shape
    return pl.pallas_call(
        _kmean_kernel,
        grid=(b, N_MOBA_BLOCKS),
        in_specs=[pl.BlockSpec((1, MOBA_BLOCK, FOX_W), lambda bi, n: (bi, n, 4))],
        out_specs=pl.BlockSpec((1, 1, 1, FOX_W), lambda bi, n: (bi, n, 0, 0)),
        out_shape=jax.ShapeDtypeStruct((b, N_MOBA_BLOCKS, 1, FOX_W), F32),
        compiler_params=pltpu.CompilerParams(dimension_semantics=("arbitrary", "arbitrary")),
        name="moba_kmean",
    )(proj).reshape(b, N_MOBA_BLOCKS, FOX_W)


def _moba_pen_kernel(q_ref, km_ref, o_ref):
    own = pl.program_id(1)
    gate = jnp.dot(q_ref[0].astype(F32), km_ref[0], precision=lax.Precision.HIGHEST,
                   preferred_element_type=F32)
    lane = lax.broadcasted_iota(jnp.int32, gate.shape, 1)
    n = lane & (N_MOBA_BLOCKS - 1)
    past = n < own
    g = jnp.where(past, gate, -jnp.inf)
    rank = jnp.zeros(gate.shape, jnp.int32)
    for k in range(1, N_MOBA_BLOCKS):
        gk = jnp.where(n >= k, pltpu.roll(g, k, 1), pltpu.roll(g, AUG - N_MOBA_BLOCKS + k, 1))
        beats = (gk > g) | ((gk == g) & (n >= k))
        rank = rank + beats.astype(jnp.int32)
    keep = (past & (rank < MOBA_TOPK)) | (n == own)
    o_ref[0] = jnp.where(keep, 0.0, PEN).astype(BF16)


def _moba_penalty(proj, km_mat):
    b, s, _ = proj.shape
    return pl.pallas_call(
        _moba_pen_kernel,
        grid=(b, N_MOBA_BLOCKS),
        in_specs=[pl.BlockSpec((1, MOBA_BLOCK, FOX_W), lambda bi, n: (bi, n, 3)),
                  pl.BlockSpec((1, FOX_W, AUG), lambda bi, n: (bi, 0, 0))],
        out_specs=pl.BlockSpec((1, MOBA_BLOCK, AUG), lambda bi, n: (bi, n, 0)),
        out_shape=jax.ShapeDtypeStruct((b, s, AUG), BF16),
        compiler_params=pltpu.CompilerParams(dimension_semantics=("arbitrary", "arbitrary")),
        name="moba_pen",
    )(proj, km_mat)


def _attn_kernel(qa_ref, ka_ref, v_ref, o_ref):
    tq = qa_ref.shape[1]
    tk = tq
    qi = pl.program_id(2)
    qa = qa_ref[0]
    lane = lax.broadcasted_iota(jnp.int32, (1, PAIR + AUG), 1)
    head0 = (lane < HEAD) | ((lane >= PAIR) & (lane < PAIR + AUG_PER_HEAD))
    head1 = ((lane >= HEAD) & (lane < PAIR)) | (
        (lane >= PAIR + AUG_PER_HEAD) & (lane < PAIR + 2 * AUG_PER_HEAD))
    zero = jnp.zeros_like(qa)
    qs = (jnp.where(head0, qa, zero), jnp.where(head1, qa, zero))
    row = lax.broadcasted_iota(jnp.int32, (tq, tk), 0)
    col = lax.broadcasted_iota(jnp.int32, (tq, tk), 1)
    causal = row >= col

    def tile(ki, carry, diagonal):
        start = pl.multiple_of(ki * tk, tk)
        k = ka_ref[0, pl.ds(start, tk), :]
        v = v_ref[0, pl.ds(start, tk), :]
        new = []
        for q, (m, l, acc) in zip(qs, carry):
            s = lax.dot_general(q, k, (((1,), (1,)), ((), ())), preferred_element_type=F32)
            if diagonal:
                s = jnp.where(causal, s, NEG)
            m_new = jnp.maximum(m, jnp.max(s, axis=-1, keepdims=True))
            alpha = jnp.exp(m - m_new)
            p = jnp.exp(s - m_new)
            l = alpha * l + jnp.sum(p, axis=-1, keepdims=True)
            acc = alpha * acc + jnp.dot(p.astype(BF16), v, preferred_element_type=F32)
            new.append((m_new, l, acc))
        return tuple(new)

    init = tuple((jnp.full((tq, 1), NEG, F32), jnp.zeros((tq, 1), F32),
                  jnp.zeros((tq, PAIR), F32)) for _ in range(2))
    carry = lax.fori_loop(0, qi, lambda ki, c: tile(ki, c, False), init)
    (_, l0, a0), (_, l1, a1) = tile(qi, carry, True)
    lane_o = lax.broadcasted_iota(jnp.int32, (1, PAIR), 1)
    o_ref[0] = jnp.where(lane_o < HEAD, a0 / l0, a1 / l1).astype(o_ref.dtype)


def _attention(qa, ka, proj):
    b, s, _ = qa.shape
    tq = 256
    v_map = lambda bi, p, qi: (bi, 0, jnp.where(p < 4, 8 + p, 16 + p))
    return pl.pallas_call(
        _attn_kernel,
        grid=(b, N_PAIR_ATTN, s // tq),
        in_specs=[pl.BlockSpec((1, tq, PAIR + AUG), lambda bi, p, qi: (bi, qi, p)),
                  pl.BlockSpec((1, s, PAIR + AUG), lambda bi, p, qi: (bi, 0, p)),
                  pl.BlockSpec((1, s, PAIR), v_map)],
        out_specs=pl.BlockSpec((1, tq, PAIR), lambda bi, p, qi: (bi, qi, p)),
        out_shape=jax.ShapeDtypeStruct((b, s, N_PAIR_ATTN * PAIR), BF16),
        compiler_params=pltpu.CompilerParams(
            dimension_semantics=("arbitrary", "arbitrary", "arbitrary"),
            vmem_limit_bytes=VMEM_LIMIT),
        name="attn",
    )(qa, ka, proj)


def _attn_operands(proj, c1, c2, c3, pen):
    b, s, _ = proj.shape
    npair = 4
    cs = jnp.stack([c1, c2, c3], axis=-1)[:, :, :8, :]
    ones = jnp.ones_like(cs)
    pad_h = jnp.zeros((b, s, 8, AUG_PER_HEAD - 6), BF16)
    q_aug = jnp.concatenate([cs, ones, pad_h], axis=-1).reshape(b, s, npair, 2 * AUG_PER_HEAD)
    k_aug = jnp.concatenate([ones, -cs, pad_h], axis=-1).reshape(b, s, npair, 2 * AUG_PER_HEAD)
    pad_p = jnp.zeros((b, s, npair, AUG - 2 * AUG_PER_HEAD), BF16)
    q_fox = proj[:, :, 0:FOX_W].reshape(b, s, npair, PAIR)
    k_fox = proj[:, :, FOX_W:2 * FOX_W].reshape(b, s, npair, PAIR)
    qa_fox = jnp.concatenate([q_fox, q_aug, pad_p], axis=-1)
    ka_fox = jnp.concatenate([k_fox, k_aug, pad_p], axis=-1)
    q_mb = proj[:, :, 3 * FOX_W:4 * FOX_W].reshape(b, s, npair, PAIR)
    k_mb = proj[:, :, 4 * FOX_W:5 * FOX_W].reshape(b, s, npair, PAIR)
    blk = jnp.arange(s, dtype=jnp.int32) // MOBA_BLOCK
    onehot = (blk[:, None] == jnp.arange(N_MOBA_BLOCKS, dtype=jnp.int32)[None, :]).astype(BF16)
    onehot2 = jnp.concatenate([onehot, onehot], axis=-1)
    onehot2 = jnp.broadcast_to(onehot2[None, :, None, :], (b, s, npair, 2 * AUG_PER_HEAD))
    qa_mb = jnp.concatenate([q_mb, pen.reshape(b, s, npair, 2 * AUG_PER_HEAD), pad_p], axis=-1)
    ka_mb = jnp.concatenate([k_mb, onehot2, pad_p], axis=-1)
    width = N_PAIR_ATTN * (PAIR + AUG)
    qa = jnp.concatenate([qa_fox, qa_mb], axis=2).reshape(b, s, width)
    ka = jnp.concatenate([ka_fox, ka_mb], axis=2).reshape(b, s, width)
    return qa, ka


def _post_kernel(x_ref, a_ref, wo_ref, gtm_ref, g_ref, sh_ref, sc_ref, gtf_ref,
                 wg_ref, wu_ref, wd_ref, gfin_ref, o_ref, *, final_norm, n_ff_chunks):
    x1 = x_ref[0] + gtm_ref[0] * jnp.dot(a_ref[0], wo_ref[...], preferred_element_type=F32)
    h = _norm_mod(x1, g_ref[...], sh_ref[0], sc_ref[0]).astype(BF16)
    cw = D_FF // n_ff_chunks
    ff = jnp.zeros_like(x1)
    for c in range(n_ff_chunks):
        gate = jnp.dot(h, wg_ref[:, c * cw:(c + 1) * cw], preferred_element_type=F32)
        up = jnp.dot(h, wu_ref[:, c * cw:(c + 1) * cw], preferred_element_type=F32)
        act = (gate * _sigmoid(gate) * up).astype(BF16)
        ff = ff + jnp.dot(act, wd_ref[c * cw:(c + 1) * cw, :], preferred_element_type=F32)
    x2 = x1 + gtf_ref[0] * ff
    if final_norm:
        x2 = x2 * lax.rsqrt(jnp.mean(x2 * x2, axis=-1, keepdims=True) + NORM_EPS) * gfin_ref[...]
    o_ref[0] = x2


def _post(x, a, w_o, gt_m, g_ffn, sh_f, sc_f, gt_f, w_gate, w_up, w_down, g_final, final_norm):
    b, s, _ = x.shape
    tm = 256
    row3 = lambda bi, si: (bi, 0, 0)
    const2 = lambda bi, si: (0, 0)
    tile3 = lambda bi, si: (bi, si, 0)
    kern = functools.partial(_post_kernel, final_norm=final_norm, n_ff_chunks=2)
    return pl.pallas_call(
        kern,
        grid=(b, s // tm),
        in_specs=[pl.BlockSpec((1, tm, D), tile3),
                  pl.BlockSpec((1, tm, D), tile3),
                  pl.BlockSpec((D, D), const2),
                  pl.BlockSpec((1, 1, D), row3),
                  pl.BlockSpec((1, D), const2),
                  pl.BlockSpec((1, 1, D), row3),
                  pl.BlockSpec((1, 1, D), row3),
                  pl.BlockSpec((1, 1, D), row3),
                  pl.BlockSpec((D, D_FF), const2),
                  pl.BlockSpec((D, D_FF), const2),
                  pl.BlockSpec((D_FF, D), const2),
                  pl.BlockSpec((1, D), const2)],
        out_specs=pl.BlockSpec((1, tm, D), tile3),
        out_shape=jax.ShapeDtypeStruct((b, s, D), F32),
        compiler_params=pltpu.CompilerParams(
            dimension_semantics=("arbitrary", "arbitrary"), vmem_limit_bytes=VMEM_LIMIT),
        name="post_final" if final_norm else "post",
    )(x, a, w_o, gt_m, g_ffn, sh_f, sc_f, gt_f, w_gate, w_up, w_down, g_final)


def _rwkvproj_kernel(x_ref, g_ref, sh_ref, sc_ref, mu_ref, wr_ref, wk_ref, wv_ref,
                     w1_ref, w2_ref, a1_ref, a2_ref, g1_ref, g2_ref,
                     w0_ref, a0_ref, kk_ref, ka_ref,
                     r_ref, k_ref, v_ref, kkv_ref, a_ref, lw_ref, gate_ref, carry_ref):
    tm = x_ref.shape[1]

    @pl.when(pl.program_id(1) == 0)
    def _():
        carry_ref[...] = jnp.zeros_like(carry_ref)

    h = _norm_mod(x_ref[0], g_ref[...], sh_ref[0], sc_ref[0])
    rowi = lax.broadcasted_iota(jnp.int32, (tm, 1), 0)
    h_prev = jnp.where(rowi == 0, carry_ref[...], pltpu.roll(h, 1, 0))
    carry_ref[...] = h[tm - 1:tm, :]
    xx = h_prev - h
    mix = lambda i: (h + xx * mu_ref[i:i + 1, :]).astype(BF16)
    r = jnp.dot(mix(0), wr_ref[...], preferred_element_type=F32)
    wl = w0_ref[...] + _mm(jnp.tanh(_mm(mix(1), w1_ref[...])), w2_ref[...])
    k = jnp.dot(mix(2), wk_ref[...], preferred_element_type=F32)
    v = jnp.dot(mix(3), wv_ref[...], preferred_element_type=F32)
    a = _sigmoid(a0_ref[...] + _mm(_mm(mix(4), a1_ref[...]), a2_ref[...]))
    gate = _mm(_sigmoid(_mm(mix(5), g1_ref[...])), g2_ref[...])
    w_log = -_softplus(-wl) - 0.5
    r_ref[0] = r
    k_ref[0] = k * (1.0 + (a - 1.0) * ka_ref[...])
    v_ref[0] = v
    kkv_ref[0] = k * kk_ref[...]
    a_ref[0] = a
    lw_ref[0] = -jnp.exp(w_log)
    gate_ref[0] = gate


def _rwkv_proj(x, g, sh, sc, mu, w_r, w_k, w_v, w1, w2, a1, a2, g1, g2, w0, a0, k_k, k_a):
    b, s, _ = x.shape
    tm = 256
    row3 = lambda bi, si: (bi, 0, 0)
    const2 = lambda bi, si: (0, 0)
    tile3 = lambda bi, si: (bi, si, 0)
    full = lambda arr: pl.BlockSpec(arr.shape, const2)
    outs = pl.pallas_call(
        _rwkvproj_kernel,
        grid=(b, s // tm),
        in_specs=[pl.BlockSpec((1, tm, D), tile3), full(g),
                  pl.BlockSpec((1, 1, D), row3), pl.BlockSpec((1, 1, D), row3),
                  full(mu), full(w_r), full(w_k), full(w_v),
                  full(w1), full(w2), full(a1), full(a2), full(g1), full(g2),
                  full(w0), full(a0), full(k_k), full(k_a)],
        out_specs=[pl.BlockSpec((1, tm, D), tile3)] * 7,
        out_shape=[jax.ShapeDtypeStruct((b, s, D), F32)] * 7,
        scratch_shapes=[pltpu.VMEM((1, D), F32)],
        compiler_params=pltpu.CompilerParams(
            dimension_semantics=("arbitrary", "arbitrary"), vmem_limit_bytes=VMEM_LIMIT),
        name="rwkv_proj",
    )(x, g, sh, sc, mu, w_r, w_k, w_v, w1, w2, a1, a2, g1, g2, w0, a0, k_k, k_a)
    return outs


def _scan_kernel(r_ref, k_ref, v_ref, kkv_ref, a_ref, lw_ref, gate_ref,
                 rk_ref, lnw_ref, lnb_ref, o_ref, state_ref):
    tc = r_ref.shape[1]
    n_chunks = tc // CHUNK

    @pl.when(pl.program_id(2) == 0)
    def _():
        state_ref[...] = jnp.zeros_like(state_ref)

    r = r_ref[0]
    k = k_ref[0]
    v = v_ref[0]
    kkv = kkv_ref[0]
    a = a_ref[0]
    lw = lw_ref[0]

    li = lax.broadcasted_iota(jnp.int32, (PAIR, PAIR), 0)
    lj = lax.broadcasted_iota(jnp.int32, (PAIR, PAIR), 1)
    ones_bd = jnp.where((li >> 6) == (lj >> 6), 1.0, 0.0).astype(BF16)
    strict = (li & (CHUNK - 1)) > (lj & (CHUNK - 1))
    incl = (li & (CHUNK - 1)) >= (lj & (CHUNK - 1))
    eye = jnp.where(li == lj, 1.0, 0.0).astype(F32)

    ti = lax.broadcasted_iota(jnp.int32, (tc, tc), 0)
    tj = lax.broadcasted_iota(jnp.int32, (tc, tc), 1)
    tri = jnp.where((ti >= tj) & ((ti >> 6) == (tj >> 6)), 1.0, 0.0).astype(F32)
    cl = _mm_split3_lhs_exact(tri, lw)

    n2 = _mm(kkv * kkv, ones_bd)
    kk = kkv / jnp.maximum(jnp.sqrt(n2), 1e-12)
    bvec = kk * a
    eg = jnp.exp(cl)
    egi = jnp.exp(-cl)
    a_t = -kk * jnp.exp(cl - lw)
    b_t = bvec * egi
    k_t = k * egi
    r_t = r * eg

    lane = lax.broadcasted_iota(jnp.int32, (1, PAIR), 1)
    is_h0 = lane < HEAD

    def stack(z):
        return jnp.concatenate([jnp.where(is_h0, z, 0.0), jnp.where(is_h0, 0.0, z)], axis=0)

    state = state_ref[...]
    ys = []
    for c in range(n_chunks):
        sl = slice(c * CHUNK, (c + 1) * CHUNK)
        cl_end = cl[(c + 1) * CHUNK - 1:(c + 1) * CHUNK, :]
        tail = jnp.exp(cl_end - cl[sl])
        a_s, r_s = stack(a_t[sl]), stack(r_t[sl])
        b_s, k_s = stack(b_t[sl]), stack(k_t[sl])
        v_s = stack(v[sl])
        bh_s, kh_s = stack(bvec[sl] * tail), stack(k[sl] * tail)

        ar = jnp.concatenate([a_s, r_s], axis=0)
        bk = jnp.concatenate([b_s, k_s], axis=0)
        gram = _mm_nt(ar, bk)
        m_ab = jnp.where(strict, gram[:PAIR, :PAIR], 0.0)
        m_ak = jnp.where(strict, gram[:PAIR, PAIR:], 0.0)
        a_rb = jnp.where(incl, gram[PAIR:, :PAIR], 0.0)
        a_rk = jnp.where(incl, gram[PAIR:, PAIR:], 0.0)

        t_inv = eye + m_ab
        mp = m_ab
        for _ in range(5):
            mp = _mm(mp, mp)
            t_inv = t_inv + _mm(t_inv, mp)

        ars = _mm_nt(ar, state)
        u = _mm(t_inv, ars[:PAIR] + _mm(m_ak, v_s))
        uv = jnp.concatenate([u, v_s], axis=0)
        y_s = ars[PAIR:] + _mm(jnp.concatenate([a_rb, a_rk], axis=1), uv)
        ys.append(y_s[:CHUNK] + y_s[CHUNK:])
        state = state * jnp.exp(cl_end) + _mm_tn(uv, jnp.concatenate([bh_s, kh_s], axis=0))
    state_ref[...] = state

    y = jnp.concatenate(ys, axis=0)
    inv_n = 1.0 / HEAD
    mean = _mm(y, ones_bd) * inv_n
    yc = y - mean
    var = _mm(yc * yc, ones_bd) * inv_n
    yn = yc * lax.rsqrt(var + GN_EPS) * lnw_ref[...] + lnb_ref[...]
    bonus = _mm(r * k * rk_ref[...], ones_bd) * v
    o_ref[0] = ((yn + bonus) * gate_ref[0]).astype(o_ref.dtype)


def _rwkv_scan(r, k, v, kkv, a, lw, gate, r_k, ln_w, ln_b):
    b, s, _ = r.shape
    tc = 256
    tile = pl.BlockSpec((1, tc, PAIR), lambda bi, p, ti: (bi, ti, p))
    vec = pl.BlockSpec((1, PAIR), lambda bi, p, ti: (0, p))
    return pl.pallas_call(
        _scan_kernel,
        grid=(b, D // PAIR, s // tc),
        in_specs=[tile] * 7 + [vec] * 3,
        out_specs=tile,
        out_shape=jax.ShapeDtypeStruct((b, s, D), BF16),
        scratch_shapes=[pltpu.VMEM((PAIR, PAIR), F32)],
        compiler_params=pltpu.CompilerParams(
            dimension_semantics=("arbitrary", "arbitrary", "arbitrary"),
            vmem_limit_bytes=VMEM_LIMIT),
        name="rwkv_scan",
    )(r, k, v, kkv, a, lw, gate, r_k, ln_w, ln_b)


def kernel(x, c, attn_w_in, attn_b_f, attn_w_out, rwkv_mu, rwkv_w_r, rwkv_w_k, rwkv_w_v, rwkv_w_o, rwkv_w0, rwkv_w1, rwkv_w2, rwkv_a0, rwkv_a1, rwkv_a2, rwkv_g1, rwkv_g2, rwkv_k_k, rwkv_k_a, rwkv_r_k, rwkv_ln_w, rwkv_ln_b, mod_w, mod_b, norm_mix, norm_ffn, ffn_w_in, ffn_w_out, norm_final):
    b, s, _ = x.shape
    mod = _modulation(c, mod_w, mod_b)
    mods = [[mod[l, :, i * D:(i + 1) * D].reshape(b, 1, D) for i in range(6)] for l in range(2)]
    row = lambda t: t.reshape(1, D)
    g_final = row(norm_final)

    sh_m, sc_m, gt_m, sh_f, sc_f, gt_f = mods[0]
    w_in = attn_w_in[0]
    qscale = HEAD ** -0.5
    col_scale = jnp.ones((3 * D,), F32)
    col_scale = col_scale.at[0:FOX_W].set(qscale).at[3 * FOX_W:4 * FOX_W].set(qscale)
    w_main = (w_in[:, :3 * D] * col_scale[None, :]).astype(BF16)
    w_f = jnp.zeros((D, AUG), F32).at[:, :8].set(w_in[:, 3 * D:]).astype(BF16)
    b_f = jnp.zeros((1, AUG), F32).at[0, :8].set(attn_b_f[0])
    proj, c1, c2, c3 = _inproj(x, row(norm_mix[0]), sh_m, sc_m, w_main, w_f, b_f)

    km = _moba_kmean(proj)
    km_t = jnp.tile(jnp.transpose(km, (0, 2, 1)), (1, 1, 8))
    same_head = (jnp.arange(FOX_W)[:, None] // HEAD) == (jnp.arange(AUG)[None, :] // N_MOBA_BLOCKS)
    km_mat = jnp.where(same_head[None], km_t, 0.0)
    pen = _moba_penalty(proj, km_mat)

    qa, ka = _attn_operands(proj, c1, c2, c3, pen)
    o = _attention(qa, ka, proj)

    w_gate = ffn_w_in[:, :, :D_FF].astype(BF16)
    w_up = ffn_w_in[:, :, D_FF:].astype(BF16)
    w_down = ffn_w_out.astype(BF16)
    x = _post(x, o, attn_w_out[0].astype(BF16), gt_m, row(norm_ffn[0]), sh_f, sc_f, gt_f,
              w_gate[0], w_up[0], w_down[0], g_final, final_norm=False)

    sh_m, sc_m, gt_m, sh_f, sc_f, gt_f = mods[1]
    pad_in = lambda w, n: jnp.zeros((D, n), BF16).at[:, :w.shape[1]].set(w.astype(BF16))
    pad_out = lambda w, n: jnp.zeros((n, D), BF16).at[:w.shape[0], :].set(w.astype(BF16))
    r, k, v, kkv, a, lw, gate = _rwkv_proj(
        x, row(norm_mix[1]), sh_m, sc_m, rwkv_mu[0],
        rwkv_w_r[0].astype(BF16), rwkv_w_k[0].astype(BF16), rwkv_w_v[0].astype(BF16),
        pad_in(rwkv_w1[0], 128), pad_out(rwkv_w2[0], 128),
        pad_in(rwkv_a1[0], 128), pad_out(rwkv_a2[0], 128),
        pad_in(rwkv_g1[0], 256), pad_out(rwkv_g2[0], 256),
        row(rwkv_w0[0]), row(rwkv_a0[0]), row(rwkv_k_k[0]), row(rwkv_k_a[0]))
    y = _rwkv_scan(r, k, v, kkv, a, lw, gate, row(rwkv_r_k[0]), row(rwkv_ln_w[0]),
                   row(rwkv_ln_b[0]))
    x = _post(x, y, rwkv_w_o[0].astype(BF16), gt_m, row(norm_ffn[1]), sh_f, sc_f, gt_f,
              w_gate[1], w_up[1], w_down[1], g_final, final_norm=True)
    return x
```

```python
import functools

import jax
import jax.numpy as jnp
from jax import lax
from jax.experimental import pallas as pl
from jax.experimental.pallas import tpu as pltpu

F32 = jnp.float32
BF16 = jnp.bfloat16

D = 1024
HEAD = 64
PAIR = 2 * HEAD
N_PAIR_ATTN = 8
FOX_W = 512
D_FF = 2816
MOBA_BLOCK = 256
N_MOBA_BLOCKS = 16
MOBA_TOPK = 3
AUG = 128
AUG_PER_HEAD = 16
NORM_EPS = 1e-6
GN_EPS = HEAD * 1e-5
LOG2E = 1.4426950408889634
NEG = -1e30
PEN = -30000.0
CHUNK = 64
VMEM_LIMIT = 56 * 1024 * 1024


def _mm(a, b):
    return jnp.dot(a.astype(BF16), b.astype(BF16), preferred_element_type=F32)


def _mm_nt(a, b):
    return lax.dot_general(a.astype(BF16), b.astype(BF16), (((1,), (1,)), ((), ())),
                           preferred_element_type=F32)


def _mm_tn(a, b):
    return lax.dot_general(a.astype(BF16), b.astype(BF16), (((0,), (0,)), ((), ())),
                           preferred_element_type=F32)


def _mm_split3(a, b_exact):
    a1 = a.astype(BF16)
    r1 = a - a1.astype(F32)
    a2 = r1.astype(BF16)
    a3 = (r1 - a2.astype(F32)).astype(BF16)
    bb = b_exact.astype(BF16)
    return (jnp.dot(a1, bb, preferred_element_type=F32)
            + jnp.dot(a2, bb, preferred_element_type=F32)
            + jnp.dot(a3, bb, preferred_element_type=F32))


def _sigmoid(x):
    return 1.0 / (1.0 + jnp.exp(-x))


def _softplus(x):
    return jnp.maximum(x, 0.0) + jnp.log(1.0 + jnp.exp(-jnp.abs(x)))


def _norm_mod(x, g, sh, sc):
    y = x * lax.rsqrt(jnp.mean(x * x, axis=-1, keepdims=True) + NORM_EPS)
    return (y * g) * (1.0 + sc) + sh


def _mod_kernel(c_ref, w_ref, b_ref, o_ref):
    c = c_ref[...]
    cond = c * _sigmoid(c)
    o_ref[0] = jnp.dot(cond, w_ref[0], precision=lax.Precision.HIGHEST,
                       preferred_element_type=F32) + b_ref[0]


def _modulation(c, mod_w, mod_b):
    depth, _, n = mod_w.shape
    b = c.shape[0]
    rows = 8
    c_pad = jnp.zeros((rows, D), F32).at[:b].set(c)
    tn = 1536
    out = pl.pallas_call(
        _mod_kernel,
        grid=(depth, n // tn),
        in_specs=[pl.BlockSpec((rows, D), lambda l, j: (0, 0)),
                  pl.BlockSpec((1, D, tn), lambda l, j: (l, 0, j)),
                  pl.BlockSpec((1, 1, tn), lambda l, j: (l, 0, j))],
        out_specs=pl.BlockSpec((1, rows, tn), lambda l, j: (l, 0, j)),
        out_shape=jax.ShapeDtypeStruct((depth, rows, n), F32),
        compiler_params=pltpu.CompilerParams(
            dimension_semantics=("arbitrary", "arbitrary"), vmem_limit_bytes=VMEM_LIMIT),
        name="mod",
    )(c_pad, mod_w, mod_b.reshape(depth, 1, n))
    return out[:, :b]


def _inproj_kernel(x_ref, g_ref, sh_ref, sc_ref, w_ref, wf_ref, bf_ref,
                   proj_ref, vt_ref, c1_ref, c2_ref, c3_ref, carry_ref):
    tm = x_ref.shape[1]

    @pl.when(pl.program_id(1) == 0)
    def _():
        carry_ref[...] = jnp.zeros_like(carry_ref)

    h = _norm_mod(x_ref[0], g_ref[...], sh_ref[0], sc_ref[0]).astype(BF16)
    proj = jnp.dot(h, w_ref[...], preferred_element_type=F32)
    proj_ref[0] = proj.astype(BF16)
    v_all = jnp.concatenate([proj[:, 2 * FOX_W:3 * FOX_W], proj[:, 5 * FOX_W:6 * FOX_W]], axis=1)
    vt_ref[0, :, 0] = v_all.T.reshape(N_PAIR_ATTN, PAIR, tm).astype(BF16)
    f = jnp.dot(h, wf_ref[...], preferred_element_type=F32) + bf_ref[...]
    lf = -_softplus(-f)
    row = lax.broadcasted_iota(jnp.int32, (tm, tm), 0)
    col = lax.broadcasted_iota(jnp.int32, (tm, tm), 1)
    tri = jnp.where(row >= col, 1.0, 0.0).astype(F32)
    cum = _mm_split3_lhs_exact(tri, lf) + carry_ref[...]
    carry_ref[...] = cum[tm - 1:tm, :]
    cum = cum * LOG2E
    c1 = cum.astype(BF16)
    r1 = cum - c1.astype(F32)
    c2 = r1.astype(BF16)
    c1_ref[0] = c1
    c2_ref[0] = c2
    c3_ref[0] = (r1 - c2.astype(F32)).astype(BF16)


def _mm_split3_lhs_exact(a_exact, b):
    b1 = b.astype(BF16)
    r1 = b - b1.astype(F32)
    b2 = r1.astype(BF16)
    b3 = (r1 - b2.astype(F32)).astype(BF16)
    aa = a_exact.astype(BF16)
    return (jnp.dot(aa, b1, preferred_element_type=F32)
            + jnp.dot(aa, b2, preferred_element_type=F32)
            + jnp.dot(aa, b3, preferred_element_type=F32))


def _inproj(x, g, sh, sc, w_main, w_f, b_f):
    b, s, _ = x.shape
    tm = 256
    n = w_main.shape[1]
    row3 = lambda bi, si: (bi, 0, 0)
    const2 = lambda bi, si: (0, 0)
    tile3 = lambda bi, si: (bi, si, 0)
    return pl.pallas_call(
        _inproj_kernel,
        grid=(b, s // tm),
        in_specs=[pl.BlockSpec((1, tm, D), tile3),
                  pl.BlockSpec((1, D), const2),
                  pl.BlockSpec((1, 1, D), row3),
                  pl.BlockSpec((1, 1, D), row3),
                  pl.BlockSpec((D, n), const2),
                  pl.BlockSpec((D, AUG), const2),
                  pl.BlockSpec((1, AUG), const2)],
        out_specs=[pl.BlockSpec((1, tm, n), tile3),
                   pl.BlockSpec((1, N_PAIR_ATTN, 1, PAIR, tm), lambda bi, si: (bi, 0, si, 0, 0)),
                   pl.BlockSpec((1, tm, AUG), tile3),
                   pl.BlockSpec((1, tm, AUG), tile3),
                   pl.BlockSpec((1, tm, AUG), tile3)],
        out_shape=[jax.ShapeDtypeStruct((b, s, n), BF16),
                   jax.ShapeDtypeStruct((b, N_PAIR_ATTN, s // tm, PAIR, tm), BF16),
                   jax.ShapeDtypeStruct((b, s, AUG), BF16),
                   jax.ShapeDtypeStruct((b, s, AUG), BF16),
                   jax.ShapeDtypeStruct((b, s, AUG), BF16)],
        scratch_shapes=[pltpu.VMEM((1, AUG), F32)],
        compiler_params=pltpu.CompilerParams(
            dimension_semantics=("arbitrary", "arbitrary"), vmem_limit_bytes=VMEM_LIMIT),
        name="inproj",
    )(x, g, sh, sc, w_main, w_f, b_f)


def _kmean_kernel(k_ref, o_ref):
    k = k_ref[0].astype(F32)
    o_ref[0, 0] = jnp.mean(k, axis=0, keepdims=True)


def _moba_kmean(proj):
    b, s, _ = proj.shape
    return pl.pallas_call(
        _kmean_kernel,
        grid=(b, N_MOBA_BLOCKS),
        in_specs=[pl.BlockSpec((1, MOBA_BLOCK, FOX_W), lambda bi, n: (bi, n, 4))],
        out_specs=pl.BlockSpec((1, 1, 1, FOX_W), lambda bi, n: (bi, n, 0, 0)),
        out_shape=jax.ShapeDtypeStruct((b, N_MOBA_BLOCKS, 1, FOX_W), F32),
        compiler_params=pltpu.CompilerParams(dimension_semantics=("arbitrary", "arbitrary")),
        name="moba_kmean",
    )(proj).reshape(b, N_MOBA_BLOCKS, FOX_W)


def _moba_pen_kernel(q_ref, km_ref, o_ref):
    own = pl.program_id(1)
    gate = jnp.dot(q_ref[0].astype(F32), km_ref[0], precision=lax.Precision.HIGHEST,
                   preferred_element_type=F32)
    lane = lax.broadcasted_iota(jnp.int32, gate.shape, 1)
    n = lane & (N_MOBA_BLOCKS - 1)
    past = n < own
    g = jnp.where(past, gate, -jnp.inf)
    rank = jnp.zeros(gate.shape, jnp.int32)
    for k in range(1, N_MOBA_BLOCKS):
        gk = jnp.where(n >= k, pltpu.roll(g, k, 1), pltpu.roll(g, AUG - N_MOBA_BLOCKS + k, 1))
        beats = (gk > g) | ((gk == g) & (n >= k))
        rank = rank + beats.astype(jnp.int32)
    keep = (past & (rank < MOBA_TOPK)) | (n == own)
    o_ref[0] = jnp.where(keep, 0.0, PEN).astype(BF16)


def _moba_penalty(proj, km_mat):
    b, s, _ = proj.shape
    return pl.pallas_call(
        _moba_pen_kernel,
        grid=(b, N_MOBA_BLOCKS),
        in_specs=[pl.BlockSpec((1, MOBA_BLOCK, FOX_W), lambda bi, n: (bi, n, 3)),
                  pl.BlockSpec((1, FOX_W, AUG), lambda bi, n: (bi, 0, 0))],
        out_specs=pl.BlockSpec((1, MOBA_BLOCK, AUG), lambda bi, n: (bi, n, 0)),
        out_shape=jax.ShapeDtypeStruct((b, s, AUG), BF16),
        compiler_params=pltpu.CompilerParams(dimension_semantics=("arbitrary", "arbitrary")),
        name="moba_pen",
    )(proj, km_mat)


def _attn_kernel(qa_ref, ka_ref, vt_ref, o_ref, s_ref, macc_ref, lacc_ref, acc_ref):
    tq = qa_ref.shape[1]
    tk = vt_ref.shape[4]
    qi = pl.program_id(2)
    qa = qa_ref[0]
    lane = lax.broadcasted_iota(jnp.int32, (1, PAIR + AUG), 1)
    head0 = (lane < HEAD) | ((lane >= PAIR) & (lane < PAIR + AUG_PER_HEAD))
    head1 = ((lane >= HEAD) & (lane < PAIR)) | (
        (lane >= PAIR + AUG_PER_HEAD) & (lane < PAIR + 2 * AUG_PER_HEAD))
    macc_ref[...] = jnp.full(macc_ref.shape, NEG, F32)

    def score_tiles(first_tile, n, diagonal):
        rows = n * tk
        start = pl.multiple_of(first_tile * tk, tk)
        k = ka_ref[0, pl.ds(start, rows), :]
        zero = jnp.zeros_like(k)
        k2 = jnp.concatenate([jnp.where(head0, k, zero), jnp.where(head1, k, zero)], axis=0)
        st = lax.dot_general(k2, qa, (((1,), (1,)), ((), ())), preferred_element_type=F32)
        if diagonal:
            key_pos = lax.broadcasted_iota(jnp.int32, (2 * rows, tq), 0) & (tk - 1)
            qry_pos = lax.broadcasted_iota(jnp.int32, (2 * rows, tq), 1)
            st = jnp.where(key_pos <= qry_pos, st, NEG)
        s_ref[0, pl.ds(start, rows), :] = st[:rows]
        s_ref[1, pl.ds(start, rows), :] = st[rows:]
        macc_ref[...] = jnp.maximum(macc_ref[...],
                                    jnp.max(st.reshape(2, rows // 8, 8, tq), axis=1))

    for bit in (8, 4, 2, 1):
        @pl.when((qi & bit) != 0)
        def _(bit=bit):
            score_tiles(qi & (-2 * bit), bit, False)
    score_tiles(qi, 1, True)
    m = jnp.max(macc_ref[...], axis=1, keepdims=True)

    def pv_tile(t):
        start = pl.multiple_of(t * tk, tk)
        vt = vt_ref[0, 0, t]
        ls, accs = [], []
        for h in range(2):
            p = jnp.exp2(s_ref[h, pl.ds(start, tk), :] - m[h])
            ls.append(jnp.sum(p.reshape(tk // 8, 8, tq), axis=0))
            accs.append(jnp.dot(vt[h * HEAD:(h + 1) * HEAD, :], p.astype(BF16),
                                preferred_element_type=F32))
        return jnp.stack(ls), jnp.concatenate(accs, axis=0)

    def pv_tiles(first_tile, n):
        parts = [pv_tile(first_tile + t) for t in range(n)]
        while len(parts) > 1:
            parts = [(a[0] + b[0], a[1] + b[1]) for a, b in zip(parts[0::2], parts[1::2])]
        return parts[0]

    def quad_step(j, carry):
        lq, aq = pv_tiles(4 * j, 4)
        return carry[0] + lq, carry[1] + aq

    n_tiles = qi + 1
    lacc, acc = lax.fori_loop(0, n_tiles // 4, quad_step,
                              (jnp.zeros((2, 8, tq), F32), jnp.zeros((PAIR, tq), F32)))
    lacc_ref[...] = lacc
    acc_ref[...] = acc
    for bit in (2, 1):
        @pl.when((n_tiles & bit) != 0)
        def _(bit=bit):
            lb, ab = pv_tiles(n_tiles & (-2 * bit), bit)
            lacc_ref[...] += lb
            acc_ref[...] += ab

    l = jnp.sum(lacc_ref[...], axis=1, keepdims=True)
    acc = acc_ref[...]
    ot = jnp.concatenate([acc[:HEAD] / l[0], acc[HEAD:] / l[1]], axis=0)
    o_ref[0] = ot.T.astype(o_ref.dtype)


def _attention(qa, ka, vt):
    b, s, _ = qa.shape
    tq = 256
    nk, tk = vt.shape[2], vt.shape[4]
    return pl.pallas_call(
        _attn_kernel,
        grid=(b, N_PAIR_ATTN, s // tq),
        in_specs=[pl.BlockSpec((1, tq, PAIR + AUG), lambda bi, p, qi: (bi, qi, p)),
                  pl.BlockSpec((1, s, PAIR + AUG), lambda bi, p, qi: (bi, 0, p)),
                  pl.BlockSpec((1, 1, nk, PAIR, tk), lambda bi, p, qi: (bi, p, 0, 0, 0))],
        out_specs=pl.BlockSpec((1, tq, PAIR), lambda bi, p, qi: (bi, qi, p)),
        out_shape=jax.ShapeDtypeStruct((b, s, N_PAIR_ATTN * PAIR), BF16),
        scratch_shapes=[pltpu.VMEM((2, s, tq), F32),
                        pltpu.VMEM((2, 8, tq), F32),
                        pltpu.VMEM((2, 8, tq), F32),
                        pltpu.VMEM((PAIR, tq), F32)],
        compiler_params=pltpu.CompilerParams(
            dimension_semantics=("arbitrary", "arbitrary", "arbitrary"),
            vmem_limit_bytes=VMEM_LIMIT),
        name="attn",
    )(qa, ka, vt)


def _attn_operands(proj, c1, c2, c3, pen):
    b, s, _ = proj.shape
    npair = 4
    cs = jnp.stack([c1, c2, c3], axis=-1)[:, :, :8, :]
    ones = jnp.ones_like(cs)
    pad_h = jnp.zeros((b, s, 8, AUG_PER_HEAD - 6), BF16)
    q_aug = jnp.concatenate([cs, ones, pad_h], axis=-1).reshape(b, s, npair, 2 * AUG_PER_HEAD)
    k_aug = jnp.concatenate([ones, -cs, pad_h], axis=-1).reshape(b, s, npair, 2 * AUG_PER_HEAD)
    pad_p = jnp.zeros((b, s, npair, AUG - 2 * AUG_PER_HEAD), BF16)
    q_fox = proj[:, :, 0:FOX_W].reshape(b, s, npair, PAIR)
    k_fox = proj[:, :, FOX_W:2 * FOX_W].reshape(b, s, npair, PAIR)
    qa_fox = jnp.concatenate([q_fox, q_aug, pad_p], axis=-1)
    ka_fox = jnp.concatenate([k_fox, k_aug, pad_p], axis=-1)
    q_mb = proj[:, :, 3 * FOX_W:4 * FOX_W].reshape(b, s, npair, PAIR)
    k_mb = proj[:, :, 4 * FOX_W:5 * FOX_W].reshape(b, s, npair, PAIR)
    blk = jnp.arange(s, dtype=jnp.int32) // MOBA_BLOCK
    onehot = (blk[:, None] == jnp.arange(N_MOBA_BLOCKS, dtype=jnp.int32)[None, :]).astype(BF16)
    onehot2 = jnp.concatenate([onehot, onehot], axis=-1)
    onehot2 = jnp.broadcast_to(onehot2[None, :, None, :], (b, s, npair, 2 * AUG_PER_HEAD))
    qa_mb = jnp.concatenate([q_mb, pen.reshape(b, s, npair, 2 * AUG_PER_HEAD), pad_p], axis=-1)
    ka_mb = jnp.concatenate([k_mb, onehot2, pad_p], axis=-1)
    width = N_PAIR_ATTN * (PAIR + AUG)
    qa = jnp.concatenate([qa_fox, qa_mb], axis=2).reshape(b, s, width)
    ka = jnp.concatenate([ka_fox, ka_mb], axis=2).reshape(b, s, width)
    return qa, ka


def _post_kernel(x_ref, a_ref, wo_ref, gtm_ref, g_ref, sh_ref, sc_ref, gtf_ref,
                 wg_ref, wu_ref, wd_ref, gfin_ref, o_ref, *, final_norm, n_ff_chunks):
    x1 = x_ref[0] + gtm_ref[0] * jnp.dot(a_ref[0], wo_ref[...], preferred_element_type=F32)
    h = _norm_mod(x1, g_ref[...], sh_ref[0], sc_ref[0]).astype(BF16)
    cw = D_FF // n_ff_chunks
    ff = jnp.zeros_like(x1)
    for c in range(n_ff_chunks):
        gate = jnp.dot(h, wg_ref[:, c * cw:(c + 1) * cw], preferred_element_type=F32)
        up = jnp.dot(h, wu_ref[:, c * cw:(c + 1) * cw], preferred_element_type=F32)
        act = (gate * _sigmoid(gate) * up).astype(BF16)
        ff = ff + jnp.dot(act, wd_ref[c * cw:(c + 1) * cw, :], preferred_element_type=F32)
    x2 = x1 + gtf_ref[0] * ff
    if final_norm:
        x2 = x2 * lax.rsqrt(jnp.mean(x2 * x2, axis=-1, keepdims=True) + NORM_EPS) * gfin_ref[...]
    o_ref[0] = x2


def _post(x, a, w_o, gt_m, g_ffn, sh_f, sc_f, gt_f, w_gate, w_up, w_down, g_final, final_norm):
    b, s, _ = x.shape
    tm = 256
    row3 = lambda bi, si: (bi, 0, 0)
    const2 = lambda bi, si: (0, 0)
    tile3 = lambda bi, si: (bi, si, 0)
    kern = functools.partial(_post_kernel, final_norm=final_norm, n_ff_chunks=2)
    return pl.pallas_call(
        kern,
        grid=(b, s // tm),
        in_specs=[pl.BlockSpec((1, tm, D), tile3),
                  pl.BlockSpec((1, tm, D), tile3),
                  pl.BlockSpec((D, D), const2),
                  pl.BlockSpec((1, 1, D), row3),
                  pl.BlockSpec((1, D), const2),
                  pl.BlockSpec((1, 1, D), row3),
                  pl.BlockSpec((1, 1, D), row3),
                  pl.BlockSpec((1, 1, D), row3),
                  pl.BlockSpec((D, D_FF), const2),
                  pl.BlockSpec((D, D_FF), const2),
                  pl.BlockSpec((D_FF, D), const2),
                  pl.BlockSpec((1, D), const2)],
        out_specs=pl.BlockSpec((1, tm, D), tile3),
        out_shape=jax.ShapeDtypeStruct((b, s, D), F32),
        compiler_params=pltpu.CompilerParams(
            dimension_semantics=("arbitrary", "arbitrary"), vmem_limit_bytes=VMEM_LIMIT),
        name="post_final" if final_norm else "post",
    )(x, a, w_o, gt_m, g_ffn, sh_f, sc_f, gt_f, w_gate, w_up, w_down, g_final)


def _rwkvproj_kernel(x_ref, g_ref, sh_ref, sc_ref, mu_ref, wr_ref, wk_ref, wv_ref,
                     w1_ref, w2_ref, a1_ref, a2_ref, g1_ref, g2_ref,
                     w0_ref, a0_ref, kk_ref, ka_ref,
                     r_ref, k_ref, v_ref, kkv_ref, a_ref, lw_ref, gate_ref, carry_ref):
    tm = x_ref.shape[1]

    @pl.when(pl.program_id(1) == 0)
    def _():
        carry_ref[...] = jnp.zeros_like(carry_ref)

    h = _norm_mod(x_ref[0], g_ref[...], sh_ref[0], sc_ref[0])
    rowi = lax.broadcasted_iota(jnp.int32, (tm, 1), 0)
    h_prev = jnp.where(rowi == 0, carry_ref[...], pltpu.roll(h, 1, 0))
    carry_ref[...] = h[tm - 1:tm, :]
    xx = h_prev - h
    mix = lambda i: (h + xx * mu_ref[i:i + 1, :]).astype(BF16)
    r = jnp.dot(mix(0), wr_ref[...], preferred_element_type=F32)
    wl = w0_ref[...] + _mm(jnp.tanh(_mm(mix(1), w1_ref[...])), w2_ref[...])
    k = jnp.dot(mix(2), wk_ref[...], preferred_element_type=F32)
    v = jnp.dot(mix(3), wv_ref[...], preferred_element_type=F32)
    a = _sigmoid(a0_ref[...] + _mm(_mm(mix(4), a1_ref[...]), a2_ref[...]))
    gate = _mm(_sigmoid(_mm(mix(5), g1_ref[...])), g2_ref[...])
    w_log = -_softplus(-wl) - 0.5
    r_ref[0] = r
    k_ref[0] = k * (1.0 + (a - 1.0) * ka_ref[...])
    v_ref[0] = v
    kkv_ref[0] = k * kk_ref[...]
    a_ref[0] = a
    lw_ref[0] = -jnp.exp(w_log)
    gate_ref[0] = gate


def _rwkv_proj(x, g, sh, sc, mu, w_r, w_k, w_v, w1, w2, a1, a2, g1, g2, w0, a0, k_k, k_a):
    b, s, _ = x.shape
    tm = 256
    row3 = lambda bi, si: (bi, 0, 0)
    const2 = lambda bi, si: (0, 0)
    tile3 = lambda bi, si: (bi, si, 0)
    full = lambda arr: pl.BlockSpec(arr.shape, const2)
    outs = pl.pallas_call(
        _rwkvproj_kernel,
        grid=(b, s // tm),
        in_specs=[pl.BlockSpec((1, tm, D), tile3), full(g),
                  pl.BlockSpec((1, 1, D), row3), pl.BlockSpec((1, 1, D), row3),
                  full(mu), full(w_r), full(w_k), full(w_v),
                  full(w1), full(w2), full(a1), full(a2), full(g1), full(g2),
                  full(w0), full(a0), full(k_k), full(k_a)],
        out_specs=[pl.BlockSpec((1, tm, D), tile3)] * 7,
        out_shape=[jax.ShapeDtypeStruct((b, s, D), F32)] * 7,
        scratch_shapes=[pltpu.VMEM((1, D), F32)],
        compiler_params=pltpu.CompilerParams(
            dimension_semantics=("arbitrary", "arbitrary"), vmem_limit_bytes=VMEM_LIMIT),
        name="rwkv_proj",
    )(x, g, sh, sc, mu, w_r, w_k, w_v, w1, w2, a1, a2, g1, g2, w0, a0, k_k, k_a)
    return outs


def _scan_kernel(r_ref, k_ref, v_ref, kkv_ref, a_ref, lw_ref, gate_ref,
                 rk_ref, lnw_ref, lnb_ref, o_ref, state_ref):
    tc = r_ref.shape[1]
    n_chunks = tc // CHUNK

    @pl.when(pl.program_id(2) == 0)
    def _():
        state_ref[...] = jnp.zeros_like(state_ref)

    r = r_ref[0]
    k = k_ref[0]
    v = v_ref[0]
    kkv = kkv_ref[0]
    a = a_ref[0]
    lw = lw_ref[0]

    li = lax.broadcasted_iota(jnp.int32, (PAIR, PAIR), 0)
    lj = lax.broadcasted_iota(jnp.int32, (PAIR, PAIR), 1)
    ones_bd = jnp.where((li >> 6) == (lj >> 6), 1.0, 0.0).astype(BF16)
    strict = (li & (CHUNK - 1)) > (lj & (CHUNK - 1))
    incl = (li & (CHUNK - 1)) >= (lj & (CHUNK - 1))

    ti = lax.broadcasted_iota(jnp.int32, (tc, tc), 0)
    tj = lax.broadcasted_iota(jnp.int32, (tc, tc), 1)
    tri = jnp.where((ti >= tj) & ((ti >> 6) == (tj >> 6)), 1.0, 0.0).astype(F32)
    cl = _mm_split3_lhs_exact(tri, lw)

    n2 = _mm(kkv * kkv, ones_bd)
    kk = kkv / jnp.maximum(jnp.sqrt(n2), 1e-12)
    bvec = kk * a
    eg = jnp.exp(cl)
    egi = jnp.exp(-cl)
    a_t = -kk * jnp.exp(cl - lw)
    b_t = bvec * egi
    k_t = k * egi
    r_t = r * eg

    lane = lax.broadcasted_iota(jnp.int32, (1, PAIR), 1)
    is_h0 = lane < HEAD

    def stack(z):
        return jnp.concatenate([jnp.where(is_h0, z, 0.0), jnp.where(is_h0, 0.0, z)], axis=0)

    chunks = range(n_chunks)
    sls = [slice(c * CHUNK, (c + 1) * CHUNK) for c in chunks]
    cl_end = [cl[(c + 1) * CHUNK - 1:(c + 1) * CHUNK, :] for c in chunks]
    tails = [jnp.exp(cl_end[c] - cl[sls[c]]) for c in chunks]
    a_s = [stack(a_t[sl]) for sl in sls]
    r_s = [stack(r_t[sl]) for sl in sls]
    v_s = [stack(v[sl]) for sl in sls]
    bh_s = [stack(bvec[sls[c]] * tails[c]) for c in chunks]
    kh_s = [stack(k[sls[c]] * tails[c]) for c in chunks]
    grams = [_mm_nt(jnp.concatenate([a_s[c], r_s[c]], axis=0),
                    jnp.concatenate([stack(b_t[sls[c]]), stack(k_t[sls[c]])], axis=0))
             for c in chunks]
    m_ab = [jnp.where(strict, gm[:PAIR, :PAIR], 0.0) for gm in grams]
    m_ak = [jnp.where(strict, gm[:PAIR, PAIR:], 0.0) for gm in grams]
    a_r = [jnp.concatenate([jnp.where(incl, gm[PAIR:, :PAIR], 0.0),
                            jnp.where(incl, gm[PAIR:, PAIR:], 0.0)], axis=1) for gm in grams]

    xs = [jnp.concatenate([a_s[c], _mm(m_ak[c], v_s[c])], axis=1) for c in chunks]
    mps = m_ab
    for level in range(6):
        if level < 5:
            res = [_mm(mps[c], jnp.concatenate([mps[c], xs[c]], axis=1)) for c in chunks]
            mps = [rs[:, :PAIR] for rs in res]
            xs = [xs[c] + res[c][:, PAIR:] for c in chunks]
        else:
            xs = [xs[c] + _mm(mps[c], xs[c]) for c in chunks]
    gys = [_mm(a_r[c], jnp.concatenate(
        [xs[c], jnp.concatenate([jnp.zeros_like(v_s[c]), v_s[c]], axis=1)], axis=0)) for c in chunks]
    pqs = [_mm_tn(xs[c], bh_s[c]) for c in chunks]
    qs = [pqs[c][PAIR:] + _mm_tn(v_s[c], kh_s[c]) for c in chunks]

    state = state_ref[...]
    ys = []
    for c in chunks:
        y_s = _mm_nt(r_s[c] + gys[c][:, :PAIR], state) + gys[c][:, PAIR:]
        ys.append(y_s[:CHUNK] + y_s[CHUNK:])
        state = state * jnp.exp(cl_end[c]) + _mm(state, pqs[c][:PAIR]) + qs[c]
    state_ref[...] = state

    y = jnp.concatenate(ys, axis=0)
    inv_n = 1.0 / HEAD
    mean = _mm(y, ones_bd) * inv_n
    yc = y - mean
    var = _mm(yc * yc, ones_bd) * inv_n
    yn = yc * lax.rsqrt(var + GN_EPS) * lnw_ref[...] + lnb_ref[...]
    bonus = _mm(r * k * rk_ref[...], ones_bd) * v
    o_ref[0] = ((yn + bonus) * gate_ref[0]).astype(o_ref.dtype)


def _rwkv_scan(r, k, v, kkv, a, lw, gate, r_k, ln_w, ln_b):
    b, s, _ = r.shape
    tc = 256
    tile = pl.BlockSpec((1, tc, PAIR), lambda bi, p, ti: (bi, ti, p))
    vec = pl.BlockSpec((1, PAIR), lambda bi, p, ti: (0, p))
    return pl.pallas_call(
        _scan_kernel,
        grid=(b, D // PAIR, s // tc),
        in_specs=[tile] * 7 + [vec] * 3,
        out_specs=tile,
        out_shape=jax.ShapeDtypeStruct((b, s, D), BF16),
        scratch_shapes=[pltpu.VMEM((PAIR, PAIR), F32)],
        compiler_params=pltpu.CompilerParams(
            dimension_semantics=("arbitrary", "arbitrary", "arbitrary"),
            vmem_limit_bytes=VMEM_LIMIT),
        name="rwkv_scan",
    )(r, k, v, kkv, a, lw, gate, r_k, ln_w, ln_b)


def kernel(x, c, attn_w_in, attn_b_f, attn_w_out, rwkv_mu, rwkv_w_r, rwkv_w_k, rwkv_w_v, rwkv_w_o, rwkv_w0, rwkv_w1, rwkv_w2, rwkv_a0, rwkv_a1, rwkv_a2, rwkv_g1, rwkv_g2, rwkv_k_k, rwkv_k_a, rwkv_r_k, rwkv_ln_w, rwkv_ln_b, mod_w, mod_b, norm_mix, norm_ffn, ffn_w_in, ffn_w_out, norm_final):
    b, s, _ = x.shape
    mod = _modulation(c, mod_w, mod_b)
    mods = [[mod[l, :, i * D:(i + 1) * D].reshape(b, 1, D) for i in range(6)] for l in range(2)]
    row = lambda t: t.reshape(1, D)
    g_final = row(norm_final)

    sh_m, sc_m, gt_m, sh_f, sc_f, gt_f = mods[0]
    w_in = attn_w_in[0]
    qscale = HEAD ** -0.5 * LOG2E
    col_scale = jnp.ones((3 * D,), F32)
    col_scale = col_scale.at[0:FOX_W].set(qscale).at[3 * FOX_W:4 * FOX_W].set(qscale)
    w_main = (w_in[:, :3 * D] * col_scale[None, :]).astype(BF16)
    w_f = jnp.zeros((D, AUG), F32).at[:, :8].set(w_in[:, 3 * D:]).astype(BF16)
    b_f = jnp.zeros((1, AUG), F32).at[0, :8].set(attn_b_f[0])
    proj, vt, c1, c2, c3 = _inproj(x, row(norm_mix[0]), sh_m, sc_m, w_main, w_f, b_f)

    km = _moba_kmean(proj)
    km_t = jnp.tile(jnp.transpose(km, (0, 2, 1)), (1, 1, 8))
    same_head = (jnp.arange(FOX_W)[:, None] // HEAD) == (jnp.arange(AUG)[None, :] // N_MOBA_BLOCKS)
    km_mat = jnp.where(same_head[None], km_t, 0.0)
    pen = _moba_penalty(proj, km_mat)

    qa, ka = _attn_operands(proj, c1, c2, c3, pen)
    o = _attention(qa, ka, vt)

    w_gate = ffn_w_in[:, :, :D_FF].astype(BF16)
    w_up = ffn_w_in[:, :, D_FF:].astype(BF16)
    w_down = ffn_w_out.astype(BF16)
    x = _post(x, o, attn_w_out[0].astype(BF16), gt_m, row(norm_ffn[0]), sh_f, sc_f, gt_f,
              w_gate[0], w_up[0], w_down[0], g_final, final_norm=False)

    sh_m, sc_m, gt_m, sh_f, sc_f, gt_f = mods[1]
    pad_in = lambda w, n: jnp.zeros((D, n), BF16).at[:, :w.shape[1]].set(w.astype(BF16))
    pad_out = lambda w, n: jnp.zeros((n, D), BF16).at[:w.shape[0], :].set(w.astype(BF16))
    r, k, v, kkv, a, lw, gate = _rwkv_proj(
        x, row(norm_mix[1]), sh_m, sc_m, rwkv_mu[0],
        rwkv_w_r[0].astype(BF16), rwkv_w_k[0].astype(BF16), rwkv_w_v[0].astype(BF16),
        pad_in(rwkv_w1[0], 128), pad_out(rwkv_w2[0], 128),
        pad_in(rwkv_a1[0], 128), pad_out(rwkv_a2[0], 128),
        pad_in(rwkv_g1[0], 256), pad_out(rwkv_g2[0], 256),
        row(rwkv_w0[0]), row(rwkv_a0[0]), row(rwkv_k_k[0]), row(rwkv_k_a[0]))
    y = _rwkv_scan(r, k, v, kkv, a, lw, gate, row(rwkv_r_k[0]), row(rwkv_ln_w[0]),
                   row(rwkv_ln_b[0]))
    x = _post(x, y, rwkv_w_o[0].astype(BF16), gt_m, row(norm_ffn[1]), sh_f, sc_f, gt_f,
              w_gate[1], w_up[1], w_down[1], g_final, final_norm=True)
    return x
```

```python
import functools

import jax
import jax.numpy as jnp
import numpy as np
from jax import lax
from jax.experimental import pallas as pl
from jax.experimental.pallas import tpu as pltpu

F32 = jnp.float32
BF16 = jnp.bfloat16

D = 1024
HEAD = 64
PAIR = 2 * HEAD
N_PAIR_ATTN = 8
FOX_W = 512
D_FF = 2816
MOBA_BLOCK = 256
N_MOBA_BLOCKS = 16
MOBA_TOPK = 3
AUG = 128
AUG_PER_HEAD = 16
NORM_EPS = 1e-6
GN_EPS = HEAD * 1e-5
LOG2E = 1.4426950408889634
NEG = -1e30
PEN = -30000.0
CHUNK = 64
SCAN_GROUP = 4
SCAN_BLOCK = 4 * SCAN_GROUP * CHUNK
VMEM_LIMIT = 56 * 1024 * 1024


def _mm(a, b):
    return jnp.dot(a.astype(BF16), b.astype(BF16), preferred_element_type=F32)


def _mm_nt(a, b):
    return lax.dot_general(a.astype(BF16), b.astype(BF16), (((1,), (1,)), ((), ())),
                           preferred_element_type=F32)


def _mm_tn(a, b):
    return lax.dot_general(a.astype(BF16), b.astype(BF16), (((0,), (0,)), ((), ())),
                           preferred_element_type=F32)


def _sigmoid(x):
    return 1.0 / (1.0 + jnp.exp(-x))


def _softplus(x):
    return jnp.maximum(x, 0.0) + jnp.log(1.0 + jnp.exp(-jnp.abs(x)))


def _norm_mod(x, g, sh, sc):
    y = x * lax.rsqrt(jnp.mean(x * x, axis=-1, keepdims=True) + NORM_EPS)
    return (y * g) * (1.0 + sc) + sh


def _mod_kernel(c_ref, w_ref, b_ref, o_ref):
    c = c_ref[...]
    cond = c * _sigmoid(c)
    o_ref[0] = jnp.dot(cond, w_ref[0], precision=lax.Precision.HIGHEST,
                       preferred_element_type=F32) + b_ref[0]


def _modulation(c, mod_w, mod_b):
    depth, _, n = mod_w.shape
    b = c.shape[0]
    rows = 8
    c_pad = jnp.zeros((rows, D), F32).at[:b].set(c)
    tn = 1536
    out = pl.pallas_call(
        _mod_kernel,
        grid=(depth, n // tn),
        in_specs=[pl.BlockSpec((rows, D), lambda l, j: (0, 0)),
                  pl.BlockSpec((1, D, tn), lambda l, j: (l, 0, j)),
                  pl.BlockSpec((1, 1, tn), lambda l, j: (l, 0, j))],
        out_specs=pl.BlockSpec((1, rows, tn), lambda l, j: (l, 0, j)),
        out_shape=jax.ShapeDtypeStruct((depth, rows, n), F32),
        compiler_params=pltpu.CompilerParams(
            dimension_semantics=("arbitrary", "arbitrary"), vmem_limit_bytes=VMEM_LIMIT),
        name="mod",
    )(c_pad, mod_w, mod_b.reshape(depth, 1, n))
    return out[:, :b]


def _split3(z):
    z1 = z.astype(BF16)
    r1 = z - z1.astype(F32)
    z2 = r1.astype(BF16)
    return z1, z2, (r1 - z2.astype(F32)).astype(BF16)


def _mm_split3_lhs_exact(a_exact, b):
    aa = a_exact.astype(BF16)
    b1, b2, b3 = _split3(b)
    return (jnp.dot(aa, b1, preferred_element_type=F32)
            + jnp.dot(aa, b2, preferred_element_type=F32)
            + jnp.dot(aa, b3, preferred_element_type=F32))


def _moba_penalty(gate, own):
    lane = lax.broadcasted_iota(jnp.int32, gate.shape, 1)
    n = lane & (N_MOBA_BLOCKS - 1)
    past = n < own
    g = jnp.where(past, gate, -jnp.inf)
    rank = jnp.zeros(gate.shape, jnp.int32)
    for k in range(1, N_MOBA_BLOCKS):
        gk = jnp.where(n >= k, pltpu.roll(g, k, 1), pltpu.roll(g, AUG - N_MOBA_BLOCKS + k, 1))
        beats = (gk > g) | ((gk == g) & (n >= k))
        rank = rank + beats.astype(jnp.int32)
    keep = (past & (rank < MOBA_TOPK)) | (n == own)
    return jnp.where(keep, 0.0, PEN)


def _inproj_kernel(x_ref, g_ref, sh_ref, sc_ref, w_ref, wf_ref, bf_ref, sel_c_ref, const_c_ref,
                   sel_p_ref, qa_ref, ka_ref, vt_ref, carry_ref, km_ref):
    tm = x_ref.shape[1]
    si = pl.program_id(1)

    @pl.when(si == 0)
    def _():
        carry_ref[...] = jnp.zeros_like(carry_ref)
        km_ref[...] = jnp.zeros_like(km_ref)

    h = _norm_mod(x_ref[0], g_ref[...], sh_ref[0], sc_ref[0]).astype(BF16)
    proj = jnp.dot(h, w_ref[...], preferred_element_type=F32)
    q_fox, k_fox = proj[:, 0:FOX_W], proj[:, FOX_W:2 * FOX_W]
    q_mb, k_mb = proj[:, 3 * FOX_W:4 * FOX_W], proj[:, 4 * FOX_W:5 * FOX_W]
    v_all = jnp.concatenate([proj[:, 2 * FOX_W:3 * FOX_W], proj[:, 5 * FOX_W:6 * FOX_W]], axis=1)
    vt_ref[0, :, 0] = v_all.T.reshape(N_PAIR_ATTN, PAIR, tm).astype(BF16)

    f = jnp.dot(h, wf_ref[...], preferred_element_type=F32) + bf_ref[...]
    lf = -_softplus(-f)
    row = lax.broadcasted_iota(jnp.int32, (tm, tm), 0)
    col = lax.broadcasted_iota(jnp.int32, (tm, tm), 1)
    tri = jnp.where(row >= col, 1.0, 0.0).astype(F32)
    cum = _mm_split3_lhs_exact(tri, lf) + carry_ref[...]
    carry_ref[...] = cum[tm - 1:tm, :]
    c3 = jnp.concatenate(_split3(cum * LOG2E), axis=1)
    aug_fox = jnp.dot(c3, sel_c_ref[...], preferred_element_type=F32) + const_c_ref[...]

    qb = q_mb.astype(BF16)
    km1, km2, km3 = _split3(km_ref[...])
    gate = (jnp.dot(qb, km1, preferred_element_type=F32)
            + jnp.dot(qb, km2, preferred_element_type=F32)
            + jnp.dot(qb, km3, preferred_element_type=F32))
    pen = _moba_penalty(gate, si).astype(BF16)
    aug_mb_q = jnp.dot(pen, sel_p_ref[...], preferred_element_type=F32)
    lane_p = lax.broadcasted_iota(jnp.int32, (tm, AUG), 1)
    aug_mb_k = jnp.where((lane_p < 2 * AUG_PER_HEAD) & ((lane_p & (N_MOBA_BLOCKS - 1)) == si),
                         1.0, 0.0).astype(BF16)

    kb_t = k_mb.astype(BF16).astype(F32).T
    er = lax.broadcasted_iota(jnp.int32, (tm, AUG), 1)
    place = jnp.where((er & (N_MOBA_BLOCKS - 1)) == si, 1.0 / tm, 0.0)
    upd = jnp.dot(kb_t.astype(BF16), place.astype(BF16), preferred_element_type=F32)
    d_head = lax.broadcasted_iota(jnp.int32, (FOX_W, AUG), 0) >> 6
    l_head = lax.broadcasted_iota(jnp.int32, (FOX_W, AUG), 1) >> 4
    km_ref[...] += jnp.where(d_head == l_head, upd, 0.0)

    for p in range(4):
        lo, hi = p * PAIR, (p + 1) * PAIR
        base = p * (PAIR + AUG)
        qa_ref[0, :, base:base + PAIR] = q_fox[:, lo:hi].astype(BF16)
        qa_ref[0, :, base + PAIR:base + PAIR + AUG] = aug_fox[:, lo:hi].astype(BF16)
        ka_ref[0, :, base:base + PAIR] = k_fox[:, lo:hi].astype(BF16)
        ka_ref[0, :, base + PAIR:base + PAIR + AUG] = aug_fox[:, FOX_W + lo:FOX_W + hi].astype(BF16)
        base += 4 * (PAIR + AUG)
        qa_ref[0, :, base:base + PAIR] = q_mb[:, lo:hi].astype(BF16)
        qa_ref[0, :, base + PAIR:base + PAIR + AUG] = aug_mb_q[:, lo:hi].astype(BF16)
        ka_ref[0, :, base:base + PAIR] = k_mb[:, lo:hi].astype(BF16)
        ka_ref[0, :, base + PAIR:base + PAIR + AUG] = aug_mb_k


def _bias_lane_selectors():
    sel_c = np.zeros((3 * AUG, 2 * FOX_W), np.float32)
    const_c = np.zeros((1, 2 * FOX_W), np.float32)
    sel_p = np.zeros((AUG, FOX_W), np.float32)
    for head in range(8):
        base = (head // 2) * AUG + (head % 2) * AUG_PER_HEAD
        for i in range(3):
            sel_c[i * AUG + head, base + i] = 1.0
            sel_c[i * AUG + head, FOX_W + base + 3 + i] = -1.0
            const_c[0, base + 3 + i] = 1.0
            const_c[0, FOX_W + base + i] = 1.0
        for n in range(N_MOBA_BLOCKS):
            sel_p[head * N_MOBA_BLOCKS + n, base + n] = 1.0
    return jnp.asarray(sel_c, BF16), jnp.asarray(const_c, F32), jnp.asarray(sel_p, BF16)


def _inproj(x, g, sh, sc, w_main, w_f, b_f):
    b, s, _ = x.shape
    tm = MOBA_BLOCK
    n = w_main.shape[1]
    width = N_PAIR_ATTN * (PAIR + AUG)
    sel_c, const_c, sel_p = _bias_lane_selectors()
    row3 = lambda bi, si: (bi, 0, 0)
    const2 = lambda bi, si: (0, 0)
    tile3 = lambda bi, si: (bi, si, 0)
    full = lambda arr: pl.BlockSpec(arr.shape, const2)
    return pl.pallas_call(
        _inproj_kernel,
        grid=(b, s // tm),
        in_specs=[pl.BlockSpec((1, tm, D), tile3),
                  pl.BlockSpec((1, D), const2),
                  pl.BlockSpec((1, 1, D), row3),
                  pl.BlockSpec((1, 1, D), row3),
                  pl.BlockSpec((D, n), const2),
                  pl.BlockSpec((D, AUG), const2),
                  pl.BlockSpec((1, AUG), const2),
                  full(sel_c), full(const_c), full(sel_p)],
        out_specs=[pl.BlockSpec((1, tm, width), tile3),
                   pl.BlockSpec((1, tm, width), tile3),
                   pl.BlockSpec((1, N_PAIR_ATTN, 1, PAIR, tm), lambda bi, si: (bi, 0, si, 0, 0))],
        out_shape=[jax.ShapeDtypeStruct((b, s, width), BF16),
                   jax.ShapeDtypeStruct((b, s, width), BF16),
                   jax.ShapeDtypeStruct((b, N_PAIR_ATTN, s // tm, PAIR, tm), BF16)],
        scratch_shapes=[pltpu.VMEM((1, AUG), F32), pltpu.VMEM((FOX_W, AUG), F32)],
        compiler_params=pltpu.CompilerParams(
            dimension_semantics=("arbitrary", "arbitrary"), vmem_limit_bytes=VMEM_LIMIT),
        name="inproj",
    )(x, g, sh, sc, w_main, w_f, b_f, sel_c, const_c, sel_p)


def _attn_kernel(qa_ref, ka_ref, vt_ref, o_ref, s_ref, macc_ref, lacc_ref, acc_ref):
    tq = qa_ref.shape[1]
    tk = vt_ref.shape[4]
    qi = pl.program_id(2)
    qa = qa_ref[0]
    lane = lax.broadcasted_iota(jnp.int32, (1, PAIR + AUG), 1)
    head0 = (lane < HEAD) | ((lane >= PAIR) & (lane < PAIR + AUG_PER_HEAD))
    head1 = ((lane >= HEAD) & (lane < PAIR)) | (
        (lane >= PAIR + AUG_PER_HEAD) & (lane < PAIR + 2 * AUG_PER_HEAD))
    macc_ref[...] = jnp.full(macc_ref.shape, NEG, F32)

    def score_tiles(first_tile, n, diagonal):
        rows = n * tk
        start = pl.multiple_of(first_tile * tk, tk)
        k = ka_ref[0, pl.ds(start, rows), :]
        zero = jnp.zeros_like(k)
        k2 = jnp.concatenate([jnp.where(head0, k, zero), jnp.where(head1, k, zero)], axis=0)
        st = lax.dot_general(k2, qa, (((1,), (1,)), ((), ())), preferred_element_type=F32)
        if diagonal:
            key_pos = lax.broadcasted_iota(jnp.int32, (2 * rows, tq), 0) & (tk - 1)
            qry_pos = lax.broadcasted_iota(jnp.int32, (2 * rows, tq), 1)
            st = jnp.where(key_pos <= qry_pos, st, NEG)
        s_ref[0, pl.ds(start, rows), :] = st[:rows]
        s_ref[1, pl.ds(start, rows), :] = st[rows:]
        macc_ref[...] = jnp.maximum(macc_ref[...],
                                    jnp.max(st.reshape(2, rows // 8, 8, tq), axis=1))

    for bit in (8, 4, 2, 1):
        @pl.when((qi & bit) != 0)
        def _(bit=bit):
            score_tiles(qi & (-2 * bit), bit, False)
    score_tiles(qi, 1, True)
    m = jnp.max(macc_ref[...], axis=1, keepdims=True)

    def pv_tile(t):
        start = pl.multiple_of(t * tk, tk)
        vt = vt_ref[0, 0, t]
        ls, accs = [], []
        for h in range(2):
            p = jnp.exp2(s_ref[h, pl.ds(start, tk), :] - m[h])
            ls.append(jnp.sum(p.reshape(tk // 8, 8, tq), axis=0))
            accs.append(jnp.dot(vt[h * HEAD:(h + 1) * HEAD, :], p.astype(BF16),
                                preferred_element_type=F32))
        return jnp.stack(ls), jnp.concatenate(accs, axis=0)

    def pv_tiles(first_tile, n):
        parts = [pv_tile(first_tile + t) for t in range(n)]
        while len(parts) > 1:
            parts = [(a[0] + b[0], a[1] + b[1]) for a, b in zip(parts[0::2], parts[1::2])]
        return parts[0]

    def quad_step(j, carry):
        lq, aq = pv_tiles(4 * j, 4)
        return carry[0] + lq, carry[1] + aq

    n_tiles = qi + 1
    lacc, acc = lax.fori_loop(0, n_tiles // 4, quad_step,
                              (jnp.zeros((2, 8, tq), F32), jnp.zeros((PAIR, tq), F32)))
    lacc_ref[...] = lacc
    acc_ref[...] = acc
    for bit in (2, 1):
        @pl.when((n_tiles & bit) != 0)
        def _(bit=bit):
            lb, ab = pv_tiles(n_tiles & (-2 * bit), bit)
            lacc_ref[...] += lb
            acc_ref[...] += ab

    l = jnp.sum(lacc_ref[...], axis=1, keepdims=True)
    acc = acc_ref[...]
    ot = jnp.concatenate([acc[:HEAD] / l[0], acc[HEAD:] / l[1]], axis=0)
    o_ref[0] = ot.T.astype(o_ref.dtype)


def _attention(qa, ka, vt):
    b, s, _ = qa.shape
    tq = 256
    nk, tk = vt.shape[2], vt.shape[4]
    return pl.pallas_call(
        _attn_kernel,
        grid=(b, N_PAIR_ATTN, s // tq),
        in_specs=[pl.BlockSpec((1, tq, PAIR + AUG), lambda bi, p, qi: (bi, qi, p)),
                  pl.BlockSpec((1, s, PAIR + AUG), lambda bi, p, qi: (bi, 0, p)),
                  pl.BlockSpec((1, 1, nk, PAIR, tk), lambda bi, p, qi: (bi, p, 0, 0, 0))],
        out_specs=pl.BlockSpec((1, tq, PAIR), lambda bi, p, qi: (bi, qi, p)),
        out_shape=jax.ShapeDtypeStruct((b, s, N_PAIR_ATTN * PAIR), BF16),
        scratch_shapes=[pltpu.VMEM((2, s, tq), F32),
                        pltpu.VMEM((2, 8, tq), F32),
                        pltpu.VMEM((2, 8, tq), F32),
                        pltpu.VMEM((PAIR, tq), F32)],
        compiler_params=pltpu.CompilerParams(
            dimension_semantics=("arbitrary", "arbitrary", "arbitrary"),
            vmem_limit_bytes=VMEM_LIMIT),
        name="attn",
    )(qa, ka, vt)


def _post_kernel(x_ref, a_ref, wo_ref, gtm_ref, g_ref, sh_ref, sc_ref, gtf_ref,
                 wg_ref, wu_ref, wd_ref, gfin_ref, o_ref, *, final_norm, n_ff_chunks):
    x1 = x_ref[0] + gtm_ref[0] * jnp.dot(a_ref[0], wo_ref[...], preferred_element_type=F32)
    h = _norm_mod(x1, g_ref[...], sh_ref[0], sc_ref[0]).astype(BF16)
    cw = D_FF // n_ff_chunks
    ff = jnp.zeros_like(x1)
    for c in range(n_ff_chunks):
        gate = jnp.dot(h, wg_ref[:, c * cw:(c + 1) * cw], preferred_element_type=F32)
        up = jnp.dot(h, wu_ref[:, c * cw:(c + 1) * cw], preferred_element_type=F32)
        act = (gate * _sigmoid(gate) * up).astype(BF16)
        ff = ff + jnp.dot(act, wd_ref[c * cw:(c + 1) * cw, :], preferred_element_type=F32)
    x2 = x1 + gtf_ref[0] * ff
    if final_norm:
        x2 = x2 * lax.rsqrt(jnp.mean(x2 * x2, axis=-1, keepdims=True) + NORM_EPS) * gfin_ref[...]
    o_ref[0] = x2


def _post(x, a, w_o, gt_m, g_ffn, sh_f, sc_f, gt_f, w_gate, w_up, w_down, g_final, final_norm):
    b, s, _ = x.shape
    tm = 256
    row3 = lambda bi, si: (bi, 0, 0)
    const2 = lambda bi, si: (0, 0)
    tile3 = lambda bi, si: (bi, si, 0)
    kern = functools.partial(_post_kernel, final_norm=final_norm, n_ff_chunks=2)
    return pl.pallas_call(
        kern,
        grid=(b, s // tm),
        in_specs=[pl.BlockSpec((1, tm, D), tile3),
                  pl.BlockSpec((1, tm, D), tile3),
                  pl.BlockSpec((D, D), const2),
                  pl.BlockSpec((1, 1, D), row3),
                  pl.BlockSpec((1, D), const2),
                  pl.BlockSpec((1, 1, D), row3),
                  pl.BlockSpec((1, 1, D), row3),
                  pl.BlockSpec((1, 1, D), row3),
                  pl.BlockSpec((D, D_FF), const2),
                  pl.BlockSpec((D, D_FF), const2),
                  pl.BlockSpec((D_FF, D), const2),
                  pl.BlockSpec((1, D), const2)],
        out_specs=pl.BlockSpec((1, tm, D), tile3),
        out_shape=jax.ShapeDtypeStruct((b, s, D), F32),
        compiler_params=pltpu.CompilerParams(
            dimension_semantics=("arbitrary", "arbitrary"), vmem_limit_bytes=VMEM_LIMIT),
        name="post_final" if final_norm else "post",
    )(x, a, w_o, gt_m, g_ffn, sh_f, sc_f, gt_f, w_gate, w_up, w_down, g_final)


def _rwkvproj_kernel(x_ref, g_ref, sh_ref, sc_ref, mu_ref, wr_ref, wk_ref, wv_ref,
                     w1_ref, w2_ref, a1_ref, a2_ref, g1_ref, g2_ref,
                     w0_ref, a0_ref, kk_ref, ka_ref,
                     r_ref, k_ref, v_ref, kkv_ref, a_ref, lw_ref, gate_ref, carry_ref):
    tm = x_ref.shape[1]

    @pl.when(pl.program_id(1) == 0)
    def _():
        carry_ref[...] = jnp.zeros_like(carry_ref)

    h = _norm_mod(x_ref[0], g_ref[...], sh_ref[0], sc_ref[0])
    rowi = lax.broadcasted_iota(jnp.int32, (tm, 1), 0)
    h_prev = jnp.where(rowi == 0, carry_ref[...], pltpu.roll(h, 1, 0))
    carry_ref[...] = h[tm - 1:tm, :]
    xx = h_prev - h
    mix = lambda i: (h + xx * mu_ref[i:i + 1, :]).astype(BF16)
    r = jnp.dot(mix(0), wr_ref[...], preferred_element_type=F32)
    wl = w0_ref[...] + _mm(jnp.tanh(_mm(mix(1), w1_ref[...])), w2_ref[...])
    k = jnp.dot(mix(2), wk_ref[...], preferred_element_type=F32)
    v = jnp.dot(mix(3), wv_ref[...], preferred_element_type=F32)
    a = _sigmoid(a0_ref[...] + _mm(_mm(mix(4), a1_ref[...]), a2_ref[...]))
    gate = _mm(_sigmoid(_mm(mix(5), g1_ref[...])), g2_ref[...])
    w_log = -_softplus(-wl) - 0.5
    r_ref[0] = r.astype(r_ref.dtype)
    k_ref[0] = (k * (1.0 + (a - 1.0) * ka_ref[...])).astype(k_ref.dtype)
    v_ref[0] = v.astype(v_ref.dtype)
    kkv_ref[0] = k * kk_ref[...]
    a_ref[0] = a
    lw_ref[0] = -jnp.exp(w_log)
    gate_ref[0] = gate.astype(gate_ref.dtype)


def _rwkv_proj(x, g, sh, sc, mu, w_r, w_k, w_v, w1, w2, a1, a2, g1, g2, w0, a0, k_k, k_a):
    b, s, _ = x.shape
    tm = 256
    row3 = lambda bi, si: (bi, 0, 0)
    const2 = lambda bi, si: (0, 0)
    tile3 = lambda bi, si: (bi, si, 0)
    full = lambda arr: pl.BlockSpec(arr.shape, const2)
    outs = pl.pallas_call(
        _rwkvproj_kernel,
        grid=(b, s // tm),
        in_specs=[pl.BlockSpec((1, tm, D), tile3), full(g),
                  pl.BlockSpec((1, 1, D), row3), pl.BlockSpec((1, 1, D), row3),
                  full(mu), full(w_r), full(w_k), full(w_v),
                  full(w1), full(w2), full(a1), full(a2), full(g1), full(g2),
                  full(w0), full(a0), full(k_k), full(k_a)],
        out_specs=[pl.BlockSpec((1, tm, D), tile3)] * 7,
        out_shape=[jax.ShapeDtypeStruct((b, s, D), dt) for dt in (BF16, BF16, BF16, F32, F32, F32, BF16)],
        scratch_shapes=[pltpu.VMEM((1, D), F32)],
        compiler_params=pltpu.CompilerParams(
            dimension_semantics=("arbitrary", "arbitrary"), vmem_limit_bytes=VMEM_LIMIT),
        name="rwkv_proj",
    )(x, g, sh, sc, mu, w_r, w_k, w_v, w1, w2, a1, a2, g1, g2, w0, a0, k_k, k_a)
    return outs


def _scan_kernel(r_ref, k_ref, v_ref, kkv_ref, a_ref, lw_ref, gate_ref,
                 rk_ref, lnw_ref, lnb_ref, o_ref, state_ref):
    tc = r_ref.shape[1]
    group_rows = SCAN_GROUP * CHUNK
    n_groups = tc // group_rows

    @pl.when(pl.program_id(2) == 0)
    def _():
        state_ref[...] = jnp.zeros_like(state_ref)

    li = lax.broadcasted_iota(jnp.int32, (PAIR, PAIR), 0)
    lj = lax.broadcasted_iota(jnp.int32, (PAIR, PAIR), 1)
    ones_bd = jnp.where((li >> 6) == (lj >> 6), 1.0, 0.0).astype(BF16)
    strict = (li & (CHUNK - 1)) > (lj & (CHUNK - 1))
    incl = (li & (CHUNK - 1)) >= (lj & (CHUNK - 1))
    ti = lax.broadcasted_iota(jnp.int32, (group_rows, group_rows), 0)
    tj = lax.broadcasted_iota(jnp.int32, (group_rows, group_rows), 1)
    tri = jnp.where((ti >= tj) & ((ti >> 6) == (tj >> 6)), 1.0, 0.0).astype(BF16)
    is_h0 = lax.broadcasted_iota(jnp.int32, (1, PAIR), 1) < HEAD
    chunks = range(SCAN_GROUP)

    def stack(z):
        return jnp.concatenate([jnp.where(is_h0, z, 0.0), jnp.where(is_h0, 0.0, z)], axis=0)

    def state_free_part(g, out):
        rows = slice(g * group_rows, (g + 1) * group_rows)
        r = r_ref[0, rows, :].astype(F32)
        k = k_ref[0, rows, :].astype(F32)
        v = v_ref[0, rows, :].astype(F32)
        kkv = kkv_ref[0, rows, :]
        lw = lw_ref[0, rows, :]
        lw1 = lw.astype(BF16)
        lw2 = (lw - lw1.astype(F32)).astype(BF16)
        cl = (jnp.dot(tri, lw1, preferred_element_type=F32)
              + jnp.dot(tri, lw2, preferred_element_type=F32))
        n2 = _mm(kkv * kkv, ones_bd)
        kk = kkv / jnp.maximum(jnp.sqrt(n2), 1e-12)
        bvec = kk * a_ref[0, rows, :]
        egi = jnp.exp(-cl)
        a_t = -kk * jnp.exp(cl - lw)
        b_t = bvec * egi
        k_t = k * egi
        r_t = r * jnp.exp(cl)
        out["bonus"] = _mm(r * k * rk_ref[...], ones_bd) * v
        sls = [slice(c * CHUNK, (c + 1) * CHUNK) for c in chunks]
        cl_end = [cl[(c + 1) * CHUNK - 1:(c + 1) * CHUNK, :] for c in chunks]
        out["decay"] = [jnp.exp(ce) for ce in cl_end]
        tails = [jnp.exp(cl_end[c] - cl[sls[c]]) for c in chunks]
        a_s = [stack(a_t[sl]) for sl in sls]
        r_s = [stack(r_t[sl]) for sl in sls]
        v_s = [stack(v[sl]) for sl in sls]
        bh_s = [stack(bvec[sls[c]] * tails[c]) for c in chunks]
        kh_s = [stack(k[sls[c]] * tails[c]) for c in chunks]
        grams = [_mm_nt(jnp.concatenate([a_s[c], r_s[c]], axis=0),
                        jnp.concatenate([stack(b_t[sls[c]]), stack(k_t[sls[c]])], axis=0))
                 for c in chunks]
        yield
        m_ak = [jnp.where(strict, gm[:PAIR, PAIR:], 0.0) for gm in grams]
        a_r = [jnp.concatenate([jnp.where(incl, gm[PAIR:, :PAIR], 0.0),
                                jnp.where(incl, gm[PAIR:, PAIR:], 0.0)], axis=1) for gm in grams]
        xs = [jnp.concatenate([a_s[c], _mm(m_ak[c], v_s[c])], axis=1) for c in chunks]
        mps = [jnp.where(strict, gm[:PAIR, :PAIR], 0.0) for gm in grams]
        yield
        for level in range(6):
            if level < 5:
                res = [_mm(mps[c], jnp.concatenate([mps[c], xs[c]], axis=1)) for c in chunks]
                mps = [rs[:, :PAIR] for rs in res]
                xs = [xs[c] + res[c][:, PAIR:] for c in chunks]
            else:
                xs = [xs[c] + _mm(mps[c], xs[c]) for c in chunks]
            yield
        gys = [_mm(a_r[c], jnp.concatenate(
            [xs[c], jnp.concatenate([jnp.zeros_like(v_s[c]), v_s[c]], axis=1)], axis=0))
               for c in chunks]
        out["g"] = [r_s[c] + gys[c][:, :PAIR] for c in chunks]
        out["y0"] = [gy[:, PAIR:] for gy in gys]
        yield
        pqs = [_mm_tn(xs[c], bh_s[c]) for c in chunks]
        out["pb"] = [pq[:PAIR] for pq in pqs]
        out["q"] = [pqs[c][PAIR:] + _mm_tn(v_s[c], kh_s[c]) for c in chunks]
        yield

    carried_state = [state_ref[...]]

    def state_part(g, res):
        ys = []
        for c in chunks:
            state = carried_state[0]
            y_s = _mm_nt(res["g"][c], state) + res["y0"][c]
            ys.append(y_s[:CHUNK] + y_s[CHUNK:])
            carried_state[0] = state * res["decay"][c] + _mm(state, res["pb"][c]) + res["q"][c]
            yield
        rows = slice(g * group_rows, (g + 1) * group_rows)
        y = jnp.concatenate(ys, axis=0)
        inv_n = 1.0 / HEAD
        mean = _mm(y, ones_bd) * inv_n
        yc = y - mean
        var = _mm(yc * yc, ones_bd) * inv_n
        yn = yc * lax.rsqrt(var + GN_EPS) * lnw_ref[...] + lnb_ref[...]
        o_ref[0, rows, :] = ((yn + res["bonus"]) * gate_ref[0, rows, :].astype(F32)
                             ).astype(o_ref.dtype)
        yield

    results = [dict() for _ in range(n_groups)]
    for _ in state_free_part(0, results[0]):
        pass
    for g in range(n_groups):
        carried = state_part(g, results[g])
        ahead = state_free_part(g + 1, results[g + 1]) if g + 1 < n_groups else iter(())
        for i, _ in enumerate(ahead):
            if i % 2 == 1:
                next(carried, None)
        for _ in carried:
            pass
    state_ref[...] = carried_state[0]


def _rwkv_scan(r, k, v, kkv, a, lw, gate, r_k, ln_w, ln_b):
    b, s, _ = r.shape
    tc = SCAN_BLOCK
    tile = pl.BlockSpec((1, tc, PAIR), lambda bi, p, ti: (bi, ti, p))
    vec = pl.BlockSpec((1, PAIR), lambda bi, p, ti: (0, p))
    return pl.pallas_call(
        _scan_kernel,
        grid=(b, D // PAIR, s // tc),
        in_specs=[tile] * 7 + [vec] * 3,
        out_specs=tile,
        out_shape=jax.ShapeDtypeStruct((b, s, D), BF16),
        scratch_shapes=[pltpu.VMEM((PAIR, PAIR), F32)],
        compiler_params=pltpu.CompilerParams(
            dimension_semantics=("arbitrary", "arbitrary", "arbitrary"),
            vmem_limit_bytes=VMEM_LIMIT),
        name="rwkv_scan",
    )(r, k, v, kkv, a, lw, gate, r_k, ln_w, ln_b)


def kernel(x, c, attn_w_in, attn_b_f, attn_w_out, rwkv_mu, rwkv_w_r, rwkv_w_k, rwkv_w_v, rwkv_w_o, rwkv_w0, rwkv_w1, rwkv_w2, rwkv_a0, rwkv_a1, rwkv_a2, rwkv_g1, rwkv_g2, rwkv_k_k, rwkv_k_a, rwkv_r_k, rwkv_ln_w, rwkv_ln_b, mod_w, mod_b, norm_mix, norm_ffn, ffn_w_in, ffn_w_out, norm_final):
    b, s, _ = x.shape
    mod = _modulation(c, mod_w, mod_b)
    mods = [[mod[l, :, i * D:(i + 1) * D].reshape(b, 1, D) for i in range(6)] for l in range(2)]
    row = lambda t: t.reshape(1, D)
    g_final = row(norm_final)

    sh_m, sc_m, gt_m, sh_f, sc_f, gt_f = mods[0]
    w_in = attn_w_in[0]
    qscale = HEAD ** -0.5 * LOG2E
    col_scale = jnp.ones((3 * D,), F32)
    col_scale = col_scale.at[0:FOX_W].set(qscale).at[3 * FOX_W:4 * FOX_W].set(qscale)
    w_main = (w_in[:, :3 * D] * col_scale[None, :]).astype(BF16)
    w_f = jnp.zeros((D, AUG), F32).at[:, :8].set(w_in[:, 3 * D:]).astype(BF16)
    b_f = jnp.zeros((1, AUG), F32).at[0, :8].set(attn_b_f[0])
    qa, ka, vt = _inproj(x, row(norm_mix[0]), sh_m, sc_m, w_main, w_f, b_f)
    o = _attention(qa, ka, vt)

    w_gate = ffn_w_in[:, :, :D_FF].astype(BF16)
    w_up = ffn_w_in[:, :, D_FF:].astype(BF16)
    w_down = ffn_w_out.astype(BF16)
    x = _post(x, o, attn_w_out[0].astype(BF16), gt_m, row(norm_ffn[0]), sh_f, sc_f, gt_f,
              w_gate[0], w_up[0], w_down[0], g_final, final_norm=False)

    sh_m, sc_m, gt_m, sh_f, sc_f, gt_f = mods[1]
    pad_in = lambda w, n: jnp.zeros((D, n), BF16).at[:, :w.shape[1]].set(w.astype(BF16))
    pad_out = lambda w, n: jnp.zeros((n, D), BF16).at[:w.shape[0], :].set(w.astype(BF16))
    r, k, v, kkv, a, lw, gate = _rwkv_proj(
        x, row(norm_mix[1]), sh_m, sc_m, rwkv_mu[0],
        rwkv_w_r[0].astype(BF16), rwkv_w_k[0].astype(BF16), rwkv_w_v[0].astype(BF16),
        pad_in(rwkv_w1[0], 128), pad_out(rwkv_w2[0], 128),
        pad_in(rwkv_a1[0], 128), pad_out(rwkv_a2[0], 128),
        pad_in(rwkv_g1[0], 256), pad_out(rwkv_g2[0], 256),
        row(rwkv_w0[0]), row(rwkv_a0[0]), row(rwkv_k_k[0]), row(rwkv_k_a[0]))
    y = _rwkv_scan(r, k, v, kkv, a, lw, gate, row(rwkv_r_k[0]), row(rwkv_ln_w[0]),
                   row(rwkv_ln_b[0]))
    x = _post(x, y, rwkv_w_o[0].astype(BF16), gt_m, row(norm_ffn[1]), sh_f, sc_f, gt_f,
              w_gate[1], w_up[1], w_down[1], g_final, final_norm=True)
    return x
```

```python
import functools

import jax
import jax.numpy as jnp
import numpy as np
from jax import lax
from jax.experimental import pallas as pl
from jax.experimental.pallas import tpu as pltpu

F32 = jnp.float32
BF16 = jnp.bfloat16

D = 1024
HEAD = 64
PAIR = 2 * HEAD
N_PAIR_ATTN = 8
FOX_W = 512
D_FF = 2816
MOBA_BLOCK = 256
N_MOBA_BLOCKS = 16
MOBA_TOPK = 3
AUG = 128
AUG_PER_HEAD = 16
NORM_EPS = 1e-6
GN_EPS = HEAD * 1e-5
LOG2E = 1.4426950408889634
NEG = -1e30
PEN = -30000.0
CHUNK = 64
SCAN_GROUP = 8
SCAN_BLOCK = 2 * SCAN_GROUP * CHUNK
VMEM_LIMIT = 56 * 1024 * 1024


def _mm(a, b):
    return jnp.dot(a.astype(BF16), b.astype(BF16), preferred_element_type=F32)


def _mm_nt(a, b):
    return lax.dot_general(a.astype(BF16), b.astype(BF16), (((1,), (1,)), ((), ())),
                           preferred_element_type=F32)


def _mm_tn(a, b):
    return lax.dot_general(a.astype(BF16), b.astype(BF16), (((0,), (0,)), ((), ())),
                           preferred_element_type=F32)


def _sigmoid(x):
    return 1.0 / (1.0 + jnp.exp(-x))


def _softplus(x):
    return jnp.maximum(x, 0.0) + jnp.log(1.0 + jnp.exp(-jnp.abs(x)))


def _norm_mod(x, g, sh, sc):
    y = x * lax.rsqrt(jnp.mean(x * x, axis=-1, keepdims=True) + NORM_EPS)
    return (y * g) * (1.0 + sc) + sh


def _mod_kernel(c_ref, w_ref, b_ref, o_ref):
    c = c_ref[...]
    cond = c * _sigmoid(c)
    o_ref[0] = jnp.dot(cond, w_ref[0], precision=lax.Precision.HIGHEST,
                       preferred_element_type=F32) + b_ref[0]


def _modulation(c, mod_w, mod_b):
    depth, _, n = mod_w.shape
    b = c.shape[0]
    rows = 8
    c_pad = jnp.zeros((rows, D), F32).at[:b].set(c)
    tn = 1536
    out = pl.pallas_call(
        _mod_kernel,
        grid=(depth, n // tn),
        in_specs=[pl.BlockSpec((rows, D), lambda l, j: (0, 0)),
                  pl.BlockSpec((1, D, tn), lambda l, j: (l, 0, j)),
                  pl.BlockSpec((1, 1, tn), lambda l, j: (l, 0, j))],
        out_specs=pl.BlockSpec((1, rows, tn), lambda l, j: (l, 0, j)),
        out_shape=jax.ShapeDtypeStruct((depth, rows, n), F32),
        compiler_params=pltpu.CompilerParams(
            dimension_semantics=("arbitrary", "arbitrary"), vmem_limit_bytes=VMEM_LIMIT),
        name="mod",
    )(c_pad, mod_w, mod_b.reshape(depth, 1, n))
    return out[:, :b]


def _split3(z):
    z1 = z.astype(BF16)
    r1 = z - z1.astype(F32)
    z2 = r1.astype(BF16)
    return z1, z2, (r1 - z2.astype(F32)).astype(BF16)


def _mm_split3_lhs_exact(a_exact, b):
    aa = a_exact.astype(BF16)
    b1, b2, b3 = _split3(b)
    return (jnp.dot(aa, b1, preferred_element_type=F32)
            + jnp.dot(aa, b2, preferred_element_type=F32)
            + jnp.dot(aa, b3, preferred_element_type=F32))


def _moba_penalty(gate, own):
    lane = lax.broadcasted_iota(jnp.int32, gate.shape, 1)
    n = lane & (N_MOBA_BLOCKS - 1)
    past = n < own
    g = jnp.where(past, gate, -jnp.inf)
    rank = jnp.zeros(gate.shape, jnp.int32)
    for k in range(1, N_MOBA_BLOCKS):
        gk = jnp.where(n >= k, pltpu.roll(g, k, 1), pltpu.roll(g, AUG - N_MOBA_BLOCKS + k, 1))
        beats = (gk > g) | ((gk == g) & (n >= k))
        rank = rank + beats.astype(jnp.int32)
    keep = (past & (rank < MOBA_TOPK)) | (n == own)
    return jnp.where(keep, 0.0, PEN)


def _inproj_kernel(x_ref, g_ref, sh_ref, sc_ref, w_ref, wf_ref, bf_ref, sel_c_ref, const_c_ref,
                   sel_p_ref, qa_ref, ka_ref, vt_ref, carry_ref, km_ref):
    tm = x_ref.shape[1]
    si = pl.program_id(1)

    @pl.when(si == 0)
    def _():
        carry_ref[...] = jnp.zeros_like(carry_ref)
        km_ref[...] = jnp.zeros_like(km_ref)

    h = _norm_mod(x_ref[0], g_ref[...], sh_ref[0], sc_ref[0]).astype(BF16)
    proj = jnp.dot(h, w_ref[...], preferred_element_type=F32)
    q_fox, k_fox = proj[:, 0:FOX_W], proj[:, FOX_W:2 * FOX_W]
    q_mb, k_mb = proj[:, 3 * FOX_W:4 * FOX_W], proj[:, 4 * FOX_W:5 * FOX_W]
    v_all = jnp.concatenate([proj[:, 2 * FOX_W:3 * FOX_W], proj[:, 5 * FOX_W:6 * FOX_W]], axis=1)
    vt_ref[0, :, 0] = v_all.T.reshape(N_PAIR_ATTN, PAIR, tm).astype(BF16)

    f = jnp.dot(h, wf_ref[...], preferred_element_type=F32) + bf_ref[...]
    lf = -_softplus(-f)
    row = lax.broadcasted_iota(jnp.int32, (tm, tm), 0)
    col = lax.broadcasted_iota(jnp.int32, (tm, tm), 1)
    tri = jnp.where(row >= col, 1.0, 0.0).astype(F32)
    cum = _mm_split3_lhs_exact(tri, lf) + carry_ref[...]
    carry_ref[...] = cum[tm - 1:tm, :]
    c3 = jnp.concatenate(_split3(cum * LOG2E), axis=1)
    aug_fox = jnp.dot(c3, sel_c_ref[...], preferred_element_type=F32) + const_c_ref[...]

    qb = q_mb.astype(BF16)
    km1, km2, km3 = _split3(km_ref[...])
    gate = (jnp.dot(qb, km1, preferred_element_type=F32)
            + jnp.dot(qb, km2, preferred_element_type=F32)
            + jnp.dot(qb, km3, preferred_element_type=F32))
    pen = _moba_penalty(gate, si).astype(BF16)
    aug_mb_q = jnp.dot(pen, sel_p_ref[...], preferred_element_type=F32)
    lane_p = lax.broadcasted_iota(jnp.int32, (tm, AUG), 1)
    aug_mb_k = jnp.where((lane_p < 2 * AUG_PER_HEAD) & ((lane_p & (N_MOBA_BLOCKS - 1)) == si),
                         1.0, 0.0).astype(BF16)

    kb_t = k_mb.astype(BF16).astype(F32).T
    er = lax.broadcasted_iota(jnp.int32, (tm, AUG), 1)
    place = jnp.where((er & (N_MOBA_BLOCKS - 1)) == si, 1.0 / tm, 0.0)
    upd = jnp.dot(kb_t.astype(BF16), place.astype(BF16), preferred_element_type=F32)
    d_head = lax.broadcasted_iota(jnp.int32, (FOX_W, AUG), 0) >> 6
    l_head = lax.broadcasted_iota(jnp.int32, (FOX_W, AUG), 1) >> 4
    km_ref[...] += jnp.where(d_head == l_head, upd, 0.0)

    for p in range(4):
        lo, hi = p * PAIR, (p + 1) * PAIR
        base = p * (PAIR + AUG)
        qa_ref[0, :, base:base + PAIR] = q_fox[:, lo:hi].astype(BF16)
        qa_ref[0, :, base + PAIR:base + PAIR + AUG] = aug_fox[:, lo:hi].astype(BF16)
        ka_ref[0, :, base:base + PAIR] = k_fox[:, lo:hi].astype(BF16)
        ka_ref[0, :, base + PAIR:base + PAIR + AUG] = aug_fox[:, FOX_W + lo:FOX_W + hi].astype(BF16)
        base += 4 * (PAIR + AUG)
        qa_ref[0, :, base:base + PAIR] = q_mb[:, lo:hi].astype(BF16)
        qa_ref[0, :, base + PAIR:base + PAIR + AUG] = aug_mb_q[:, lo:hi].astype(BF16)
        ka_ref[0, :, base:base + PAIR] = k_mb[:, lo:hi].astype(BF16)
        ka_ref[0, :, base + PAIR:base + PAIR + AUG] = aug_mb_k


def _bias_lane_selectors():
    sel_c = np.zeros((3 * AUG, 2 * FOX_W), np.float32)
    const_c = np.zeros((1, 2 * FOX_W), np.float32)
    sel_p = np.zeros((AUG, FOX_W), np.float32)
    for head in range(8):
        base = (head // 2) * AUG + (head % 2) * AUG_PER_HEAD
        for i in range(3):
            sel_c[i * AUG + head, base + i] = 1.0
            sel_c[i * AUG + head, FOX_W + base + 3 + i] = -1.0
            const_c[0, base + 3 + i] = 1.0
            const_c[0, FOX_W + base + i] = 1.0
        for n in range(N_MOBA_BLOCKS):
            sel_p[head * N_MOBA_BLOCKS + n, base + n] = 1.0
    return jnp.asarray(sel_c, BF16), jnp.asarray(const_c, F32), jnp.asarray(sel_p, BF16)


def _inproj(x, g, sh, sc, w_main, w_f, b_f):
    b, s, _ = x.shape
    tm = MOBA_BLOCK
    n = w_main.shape[1]
    width = N_PAIR_ATTN * (PAIR + AUG)
    sel_c, const_c, sel_p = _bias_lane_selectors()
    row3 = lambda bi, si: (bi, 0, 0)
    const2 = lambda bi, si: (0, 0)
    tile3 = lambda bi, si: (bi, si, 0)
    full = lambda arr: pl.BlockSpec(arr.shape, const2)
    return pl.pallas_call(
        _inproj_kernel,
        grid=(b, s // tm),
        in_specs=[pl.BlockSpec((1, tm, D), tile3),
                  pl.BlockSpec((1, D), const2),
                  pl.BlockSpec((1, 1, D), row3),
                  pl.BlockSpec((1, 1, D), row3),
                  pl.BlockSpec((D, n), const2),
                  pl.BlockSpec((D, AUG), const2),
                  pl.BlockSpec((1, AUG), const2),
                  full(sel_c), full(const_c), full(sel_p)],
        out_specs=[pl.BlockSpec((1, tm, width), tile3),
                   pl.BlockSpec((1, tm, width), tile3),
                   pl.BlockSpec((1, N_PAIR_ATTN, 1, PAIR, tm), lambda bi, si: (bi, 0, si, 0, 0))],
        out_shape=[jax.ShapeDtypeStruct((b, s, width), BF16),
                   jax.ShapeDtypeStruct((b, s, width), BF16),
                   jax.ShapeDtypeStruct((b, N_PAIR_ATTN, s // tm, PAIR, tm), BF16)],
        scratch_shapes=[pltpu.VMEM((1, AUG), F32), pltpu.VMEM((FOX_W, AUG), F32)],
        compiler_params=pltpu.CompilerParams(
            dimension_semantics=("arbitrary", "arbitrary"), vmem_limit_bytes=VMEM_LIMIT),
        name="inproj",
    )(x, g, sh, sc, w_main, w_f, b_f, sel_c, const_c, sel_p)


def _attn_kernel(qa_ref, ka_ref, vt_ref, o_ref, s_even, s_odd, macc_ref, mprev_ref, lacc_ref,
                 acc_ref):
    tq = qa_ref.shape[1]
    tk = vt_ref.shape[4]
    nq = s_even.shape[1] // tq
    j = pl.program_id(2)
    qa = qa_ref[0]
    lane = lax.broadcasted_iota(jnp.int32, (1, PAIR + AUG), 1)
    head0 = (lane < HEAD) | ((lane >= PAIR) & (lane < PAIR + AUG_PER_HEAD))
    head1 = ((lane >= HEAD) & (lane < PAIR)) | (
        (lane >= PAIR + AUG_PER_HEAD) & (lane < PAIR + 2 * AUG_PER_HEAD))
    m_prev = mprev_ref[...]
    macc_ref[...] = jnp.full(macc_ref.shape, NEG, F32)
    lacc_ref[...] = jnp.zeros_like(lacc_ref)
    acc_ref[...] = jnp.zeros_like(acc_ref)

    def score_tiles(s_ref, first_tile, n, diagonal):
        rows = n * tk
        start = pl.multiple_of(first_tile * tk, tk)
        k = ka_ref[0, pl.ds(start, rows), :]
        zero = jnp.zeros_like(k)
        k2 = jnp.concatenate([jnp.where(head0, k, zero), jnp.where(head1, k, zero)], axis=0)
        st = lax.dot_general(k2, qa, (((1,), (1,)), ((), ())), preferred_element_type=F32)
        if diagonal:
            key_pos = lax.broadcasted_iota(jnp.int32, (2 * rows, tq), 0) & (tk - 1)
            qry_pos = lax.broadcasted_iota(jnp.int32, (2 * rows, tq), 1)
            st = jnp.where(key_pos <= qry_pos, st, NEG)
        s_ref[0, pl.ds(start, rows), :] = st[:rows]
        s_ref[1, pl.ds(start, rows), :] = st[rows:]
        macc_ref[...] = jnp.maximum(macc_ref[...],
                                    jnp.max(st.reshape(2, rows // 8, 8, tq), axis=1))

    def pv_tile(s_ref, t):
        start = pl.multiple_of(t * tk, tk)
        vt = vt_ref[0, 0, t]
        ls, accs = [], []
        for h in range(2):
            p = jnp.exp2(s_ref[h, pl.ds(start, tk), :] - m_prev[h])
            ls.append(jnp.sum(p.reshape(tk // 8, 8, tq), axis=0))
            accs.append(jnp.dot(vt[h * HEAD:(h + 1) * HEAD, :], p.astype(BF16),
                                preferred_element_type=F32))
        return jnp.stack(ls), jnp.concatenate(accs, axis=0)

    def pv_tiles(s_ref, first_tile, n):
        parts = [pv_tile(s_ref, first_tile + t) for t in range(n)]
        while len(parts) > 1:
            parts = [(a[0] + b[0], a[1] + b[1]) for a, b in zip(parts[0::2], parts[1::2])]
        lacc_ref[...] += parts[0][0]
        acc_ref[...] += parts[0][1]

    def step(s_cur, s_prev):
        for bit in (8, 4, 2, 1):
            @pl.when(((j & bit) != 0) & (j < nq))
            def _(bit=bit):
                first = j & (-2 * bit)
                score_tiles(s_cur, first, bit, False)
                pv_tiles(s_prev, first, bit)

        @pl.when(j < nq)
        def _():
            score_tiles(s_cur, j, 1, True)

        @pl.when(j == nq)
        def _():
            for first in range(0, nq, 8):
                pv_tiles(s_prev, first, 8)

    @pl.when((j & 1) == 0)
    def _():
        step(s_even, s_odd)

    @pl.when((j & 1) == 1)
    def _():
        step(s_odd, s_even)

    mprev_ref[...] = jnp.max(macc_ref[...], axis=1, keepdims=True)

    @pl.when(j > 0)
    def _():
        l = jnp.sum(lacc_ref[...], axis=1, keepdims=True)
        acc = acc_ref[...]
        ot = jnp.concatenate([acc[:HEAD] / l[0], acc[HEAD:] / l[1]], axis=0)
        o_ref[0] = ot.T.astype(o_ref.dtype)


def _attention(qa, ka, vt):
    b, s, _ = qa.shape
    tq = 256
    nk, tk = vt.shape[2], vt.shape[4]
    nq = s // tq
    assert nq == 16, "the key-tile groups (8, 4, 2, 1) cover at most 15 fully-past tiles"
    return pl.pallas_call(
        _attn_kernel,
        grid=(b, N_PAIR_ATTN, nq + 1),
        in_specs=[pl.BlockSpec((1, tq, PAIR + AUG), lambda bi, p, j: (bi, jnp.minimum(j, nq - 1), p)),
                  pl.BlockSpec((1, s, PAIR + AUG), lambda bi, p, j: (bi, 0, p)),
                  pl.BlockSpec((1, 1, nk, PAIR, tk), lambda bi, p, j: (bi, p, 0, 0, 0))],
        out_specs=pl.BlockSpec((1, tq, PAIR), lambda bi, p, j: (bi, jnp.maximum(j - 1, 0), p)),
        out_shape=jax.ShapeDtypeStruct((b, s, N_PAIR_ATTN * PAIR), BF16),
        scratch_shapes=[pltpu.VMEM((2, s, tq), F32),
                        pltpu.VMEM((2, s, tq), F32),
                        pltpu.VMEM((2, 8, tq), F32),
                        pltpu.VMEM((2, 1, tq), F32),
                        pltpu.VMEM((2, 8, tq), F32),
                        pltpu.VMEM((PAIR, tq), F32)],
        compiler_params=pltpu.CompilerParams(
            dimension_semantics=("arbitrary", "arbitrary", "arbitrary"),
            vmem_limit_bytes=VMEM_LIMIT),
        name="attn",
    )(qa, ka, vt)


def _post_kernel(x_ref, a_ref, wo_ref, gtm_ref, g_ref, sh_ref, sc_ref, gtf_ref,
                 wg_ref, wu_ref, wd_ref, gfin_ref, o_ref, *, final_norm, n_ff_chunks):
    x1 = x_ref[0] + gtm_ref[0] * jnp.dot(a_ref[0], wo_ref[...], preferred_element_type=F32)
    h = _norm_mod(x1, g_ref[...], sh_ref[0], sc_ref[0]).astype(BF16)
    cw = D_FF // n_ff_chunks
    ff = jnp.zeros_like(x1)
    for c in range(n_ff_chunks):
        gate = jnp.dot(h, wg_ref[:, c * cw:(c + 1) * cw], preferred_element_type=F32)
        up = jnp.dot(h, wu_ref[:, c * cw:(c + 1) * cw], preferred_element_type=F32)
        act = (gate * _sigmoid(gate) * up).astype(BF16)
        ff = ff + jnp.dot(act, wd_ref[c * cw:(c + 1) * cw, :], preferred_element_type=F32)
    x2 = x1 + gtf_ref[0] * ff
    if final_norm:
        x2 = x2 * lax.rsqrt(jnp.mean(x2 * x2, axis=-1, keepdims=True) + NORM_EPS) * gfin_ref[...]
    o_ref[0] = x2


def _post(x, a, w_o, gt_m, g_ffn, sh_f, sc_f, gt_f, w_gate, w_up, w_down, g_final, final_norm):
    b, s, _ = x.shape
    tm = 256
    row3 = lambda bi, si: (bi, 0, 0)
    const2 = lambda bi, si: (0, 0)
    tile3 = lambda bi, si: (bi, si, 0)
    kern = functools.partial(_post_kernel, final_norm=final_norm, n_ff_chunks=2)
    return pl.pallas_call(
        kern,
        grid=(b, s // tm),
        in_specs=[pl.BlockSpec((1, tm, D), tile3),
                  pl.BlockSpec((1, tm, D), tile3),
                  pl.BlockSpec((D, D), const2),
                  pl.BlockSpec((1, 1, D), row3),
                  pl.BlockSpec((1, D), const2),
                  pl.BlockSpec((1, 1, D), row3),
                  pl.BlockSpec((1, 1, D), row3),
                  pl.BlockSpec((1, 1, D), row3),
                  pl.BlockSpec((D, D_FF), const2),
                  pl.BlockSpec((D, D_FF), const2),
                  pl.BlockSpec((D_FF, D), const2),
                  pl.BlockSpec((1, D), const2)],
        out_specs=pl.BlockSpec((1, tm, D), tile3),
        out_shape=jax.ShapeDtypeStruct((b, s, D), F32),
        compiler_params=pltpu.CompilerParams(
            dimension_semantics=("arbitrary", "arbitrary"), vmem_limit_bytes=VMEM_LIMIT),
        name="post_final" if final_norm else "post",
    )(x, a, w_o, gt_m, g_ffn, sh_f, sc_f, gt_f, w_gate, w_up, w_down, g_final)


def _rwkvproj_kernel(x_ref, g_ref, sh_ref, sc_ref, mu_ref, wr_ref, wk_ref, wv_ref,
                     w1_ref, w2_ref, a1_ref, a2_ref, g1_ref, g2_ref,
                     w0_ref, a0_ref, kk_ref, ka_ref,
                     r_ref, k_ref, v_ref, kkv_ref, a_ref, lw_ref, gate_ref, carry_ref):
    tm = x_ref.shape[1]

    @pl.when(pl.program_id(1) == 0)
    def _():
        carry_ref[...] = jnp.zeros_like(carry_ref)

    h = _norm_mod(x_ref[0], g_ref[...], sh_ref[0], sc_ref[0])
    rowi = lax.broadcasted_iota(jnp.int32, (tm, 1), 0)
    h_prev = jnp.where(rowi == 0, carry_ref[...], pltpu.roll(h, 1, 0))
    carry_ref[...] = h[tm - 1:tm, :]
    xx = h_prev - h
    mix = lambda i: (h + xx * mu_ref[i:i + 1, :]).astype(BF16)
    r = jnp.dot(mix(0), wr_ref[...], preferred_element_type=F32)
    wl = w0_ref[...] + _mm(jnp.tanh(_mm(mix(1), w1_ref[...])), w2_ref[...])
    k = jnp.dot(mix(2), wk_ref[...], preferred_element_type=F32)
    v = jnp.dot(mix(3), wv_ref[...], preferred_element_type=F32)
    a = _sigmoid(a0_ref[...] + _mm(_mm(mix(4), a1_ref[...]), a2_ref[...]))
    gate = _mm(_sigmoid(_mm(mix(5), g1_ref[...])), g2_ref[...])
    w_log = -_softplus(-wl) - 0.5
    r_ref[0] = r.astype(r_ref.dtype)
    k_ref[0] = (k * (1.0 + (a - 1.0) * ka_ref[...])).astype(k_ref.dtype)
    v_ref[0] = v.astype(v_ref.dtype)
    kkv_ref[0] = k * kk_ref[...]
    a_ref[0] = a
    lw_ref[0] = -jnp.exp(w_log)
    gate_ref[0] = gate.astype(gate_ref.dtype)


def _rwkv_proj(x, g, sh, sc, mu, w_r, w_k, w_v, w1, w2, a1, a2, g1, g2, w0, a0, k_k, k_a):
    b, s, _ = x.shape
    tm = 256
    row3 = lambda bi, si: (bi, 0, 0)
    const2 = lambda bi, si: (0, 0)
    tile3 = lambda bi, si: (bi, si, 0)
    full = lambda arr: pl.BlockSpec(arr.shape, const2)
    outs = pl.pallas_call(
        _rwkvproj_kernel,
        grid=(b, s // tm),
        in_specs=[pl.BlockSpec((1, tm, D), tile3), full(g),
                  pl.BlockSpec((1, 1, D), row3), pl.BlockSpec((1, 1, D), row3),
                  full(mu), full(w_r), full(w_k), full(w_v),
                  full(w1), full(w2), full(a1), full(a2), full(g1), full(g2),
                  full(w0), full(a0), full(k_k), full(k_a)],
        out_specs=[pl.BlockSpec((1, tm, D), tile3)] * 7,
        out_shape=[jax.ShapeDtypeStruct((b, s, D), dt) for dt in (BF16, BF16, BF16, F32, F32, F32, BF16)],
        scratch_shapes=[pltpu.VMEM((1, D), F32)],
        compiler_params=pltpu.CompilerParams(
            dimension_semantics=("arbitrary", "arbitrary"), vmem_limit_bytes=VMEM_LIMIT),
        name="rwkv_proj",
    )(x, g, sh, sc, mu, w_r, w_k, w_v, w1, w2, a1, a2, g1, g2, w0, a0, k_k, k_a)
    return outs


def _scan_kernel(r_ref, k_ref, v_ref, kkv_ref, a_ref, lw_ref, gate_ref,
                 rk_ref, lnw_ref, lnb_ref, o_ref, state_ref):
    tc = r_ref.shape[1]
    group_rows = SCAN_GROUP * CHUNK
    n_groups = tc // group_rows

    @pl.when(pl.program_id(2) == 0)
    def _():
        state_ref[...] = jnp.zeros_like(state_ref)

    li = lax.broadcasted_iota(jnp.int32, (PAIR, PAIR), 0)
    lj = lax.broadcasted_iota(jnp.int32, (PAIR, PAIR), 1)
    ones_bd = jnp.where((li >> 6) == (lj >> 6), 1.0, 0.0).astype(BF16)
    strict = (li & (CHUNK - 1)) > (lj & (CHUNK - 1))
    incl = (li & (CHUNK - 1)) >= (lj & (CHUNK - 1))
    eye = jnp.where(li == lj, 1.0, 0.0).astype(F32)
    ti = lax.broadcasted_iota(jnp.int32, (group_rows, group_rows), 0)
    tj = lax.broadcasted_iota(jnp.int32, (group_rows, group_rows), 1)
    tri = jnp.where((ti >= tj) & ((ti >> 6) == (tj >> 6)), 1.0, 0.0).astype(BF16)
    is_h0 = lax.broadcasted_iota(jnp.int32, (1, PAIR), 1) < HEAD
    chunks = range(SCAN_GROUP)

    def stack(z):
        return jnp.concatenate([jnp.where(is_h0, z, 0.0), jnp.where(is_h0, 0.0, z)], axis=0)

    def state_free_part(g, out):
        rows = slice(g * group_rows, (g + 1) * group_rows)
        r = r_ref[0, rows, :].astype(F32)
        k = k_ref[0, rows, :].astype(F32)
        v = v_ref[0, rows, :].astype(F32)
        kkv = kkv_ref[0, rows, :]
        lw = lw_ref[0, rows, :]
        lw1 = lw.astype(BF16)
        lw2 = (lw - lw1.astype(F32)).astype(BF16)
        cl = (jnp.dot(tri, lw1, preferred_element_type=F32)
              + jnp.dot(tri, lw2, preferred_element_type=F32))
        n2 = _mm(kkv * kkv, ones_bd)
        kk = kkv / jnp.maximum(jnp.sqrt(n2), 1e-12)
        bvec = kk * a_ref[0, rows, :]
        egi = jnp.exp(-cl)
        a_t = -kk * jnp.exp(cl - lw)
        b_t = bvec * egi
        k_t = k * egi
        r_t = r * jnp.exp(cl)
        out["bonus"] = _mm(r * k * rk_ref[...], ones_bd) * v
        sls = [slice(c * CHUNK, (c + 1) * CHUNK) for c in chunks]
        cl_end = [cl[(c + 1) * CHUNK - 1:(c + 1) * CHUNK, :] for c in chunks]
        out["decay"] = [jnp.exp(ce) for ce in cl_end]
        tails = [jnp.exp(cl_end[c] - cl[sls[c]]) for c in chunks]
        a_s = [stack(a_t[sl]) for sl in sls]
        r_s = [stack(r_t[sl]) for sl in sls]
        v_s = [stack(v[sl]) for sl in sls]
        bh_s = [stack(bvec[sls[c]] * tails[c]) for c in chunks]
        kh_s = [stack(k[sls[c]] * tails[c]) for c in chunks]
        grams = [_mm_nt(jnp.concatenate([a_s[c], r_s[c]], axis=0),
                        jnp.concatenate([stack(b_t[sls[c]]), stack(k_t[sls[c]])], axis=0))
                 for c in chunks]
        yield
        m_ak = [jnp.where(strict, gm[:PAIR, PAIR:], 0.0) for gm in grams]
        a_r = [jnp.concatenate([jnp.where(incl, gm[PAIR:, :PAIR], 0.0),
                                jnp.where(incl, gm[PAIR:, PAIR:], 0.0)], axis=1) for gm in grams]
        rhs = [jnp.concatenate([a_s[c], _mm(m_ak[c], v_s[c])], axis=1) for c in chunks]
        mps = [jnp.where(strict, gm[:PAIR, :PAIR], 0.0) for gm in grams]
        ts = [eye + mp for mp in mps]
        mps = [_mm(mp, mp) for mp in mps]
        yield
        for level in range(1, 6):
            if level < 5:
                res = [_mm(mps[c], jnp.concatenate([mps[c], ts[c]], axis=1)) for c in chunks]
                mps = [rs[:, :PAIR] for rs in res]
                ts = [ts[c] + res[c][:, PAIR:] for c in chunks]
            else:
                ts = [ts[c] + _mm(mps[c], ts[c]) for c in chunks]
            yield
        xs = [_mm(ts[c], rhs[c]) for c in chunks]
        gys = [_mm(a_r[c], jnp.concatenate(
            [xs[c], jnp.concatenate([jnp.zeros_like(v_s[c]), v_s[c]], axis=1)], axis=0))
               for c in chunks]
        out["g"] = [r_s[c] + gys[c][:, :PAIR] for c in chunks]
        out["y0"] = [gy[:, PAIR:] for gy in gys]
        yield
        pqs = [_mm_tn(xs[c], bh_s[c]) for c in chunks]
        out["pb"] = [pq[:PAIR] for pq in pqs]
        out["q"] = [pqs[c][PAIR:] + _mm_tn(v_s[c], kh_s[c]) for c in chunks]
        yield

    carried_state = [state_ref[...]]

    def state_part(g, res):
        ys = []
        for c in chunks:
            state = carried_state[0]
            y_s = _mm_nt(res["g"][c], state) + res["y0"][c]
            ys.append(y_s[:CHUNK] + y_s[CHUNK:])
            carried_state[0] = state * res["decay"][c] + _mm(state, res["pb"][c]) + res["q"][c]
            yield
        rows = slice(g * group_rows, (g + 1) * group_rows)
        y = jnp.concatenate(ys, axis=0)
        inv_n = 1.0 / HEAD
        mean = _mm(y, ones_bd) * inv_n
        yc = y - mean
        var = _mm(yc * yc, ones_bd) * inv_n
        yn = yc * lax.rsqrt(var + GN_EPS) * lnw_ref[...] + lnb_ref[...]
        o_ref[0, rows, :] = ((yn + res["bonus"]) * gate_ref[0, rows, :].astype(F32)
                             ).astype(o_ref.dtype)
        yield

    results = [dict() for _ in range(n_groups)]
    for _ in state_free_part(0, results[0]):
        pass
    for g in range(n_groups):
        carried = state_part(g, results[g])
        ahead = state_free_part(g + 1, results[g + 1]) if g + 1 < n_groups else iter(())
        n_carried, n_ahead = SCAN_GROUP + 1, 9
        for i, _ in enumerate(ahead):
            if (i * n_carried) // n_ahead != ((i + 1) * n_carried) // n_ahead:
                next(carried, None)
        for _ in carried:
            pass
    state_ref[...] = carried_state[0]


def _rwkv_scan(r, k, v, kkv, a, lw, gate, r_k, ln_w, ln_b):
    b, s, _ = r.shape
    tc = SCAN_BLOCK
    tile = pl.BlockSpec((1, tc, PAIR), lambda bi, p, ti: (bi, ti, p))
    vec = pl.BlockSpec((1, PAIR), lambda bi, p, ti: (0, p))
    return pl.pallas_call(
        _scan_kernel,
        grid=(b, D // PAIR, s // tc),
        in_specs=[tile] * 7 + [vec] * 3,
        out_specs=tile,
        out_shape=jax.ShapeDtypeStruct((b, s, D), BF16),
        scratch_shapes=[pltpu.VMEM((PAIR, PAIR), F32)],
        compiler_params=pltpu.CompilerParams(
            dimension_semantics=("arbitrary", "arbitrary", "arbitrary"),
            vmem_limit_bytes=VMEM_LIMIT),
        name="rwkv_scan",
    )(r, k, v, kkv, a, lw, gate, r_k, ln_w, ln_b)


def kernel(x, c, attn_w_in, attn_b_f, attn_w_out, rwkv_mu, rwkv_w_r, rwkv_w_k, rwkv_w_v, rwkv_w_o, rwkv_w0, rwkv_w1, rwkv_w2, rwkv_a0, rwkv_a1, rwkv_a2, rwkv_g1, rwkv_g2, rwkv_k_k, rwkv_k_a, rwkv_r_k, rwkv_ln_w, rwkv_ln_b, mod_w, mod_b, norm_mix, norm_ffn, ffn_w_in, ffn_w_out, norm_final):
    b, s, _ = x.shape
    mod = _modulation(c, mod_w, mod_b)
    mods = [[mod[l, :, i * D:(i + 1) * D].reshape(b, 1, D) for i in range(6)] for l in range(2)]
    row = lambda t: t.reshape(1, D)
    g_final = row(norm_final)

    sh_m, sc_m, gt_m, sh_f, sc_f, gt_f = mods[0]
    w_in = attn_w_in[0]
    qscale = HEAD ** -0.5 * LOG2E
    col_scale = jnp.ones((3 * D,), F32)
    col_scale = col_scale.at[0:FOX_W].set(qscale).at[3 * FOX_W:4 * FOX_W].set(qscale)
    w_main = (w_in[:, :3 * D] * col_scale[None, :]).astype(BF16)
    w_f = jnp.zeros((D, AUG), F32).at[:, :8].set(w_in[:, 3 * D:]).astype(BF16)
    b_f = jnp.zeros((1, AUG), F32).at[0, :8].set(attn_b_f[0])
    qa, ka, vt = _inproj(x, row(norm_mix[0]), sh_m, sc_m, w_main, w_f, b_f)
    o = _attention(qa, ka, vt)

    w_gate = ffn_w_in[:, :, :D_FF].astype(BF16)
    w_up = ffn_w_in[:, :, D_FF:].astype(BF16)
    w_down = ffn_w_out.astype(BF16)
    x = _post(x, o, attn_w_out[0].astype(BF16), gt_m, row(norm_ffn[0]), sh_f, sc_f, gt_f,
              w_gate[0], w_up[0], w_down[0], g_final, final_norm=False)

    sh_m, sc_m, gt_m, sh_f, sc_f, gt_f = mods[1]
    pad_in = lambda w, n: jnp.zeros((D, n), BF16).at[:, :w.shape[1]].set(w.astype(BF16))
    pad_out = lambda w, n: jnp.zeros((n, D), BF16).at[:w.shape[0], :].set(w.astype(BF16))
    r, k, v, kkv, a, lw, gate = _rwkv_proj(
        x, row(norm_mix[1]), sh_m, sc_m, rwkv_mu[0],
        rwkv_w_r[0].astype(BF16), rwkv_w_k[0].astype(BF16), rwkv_w_v[0].astype(BF16),
        pad_in(rwkv_w1[0], 128), pad_out(rwkv_w2[0], 128),
        pad_in(rwkv_a1[0], 128), pad_out(rwkv_a2[0], 128),
        pad_in(rwkv_g1[0], 256), pad_out(rwkv_g2[0], 256),
        row(rwkv_w0[0]), row(rwkv_a0[0]), row(rwkv_k_k[0]), row(rwkv_k_a[0]))
    y = _rwkv_scan(r, k, v, kkv, a, lw, gate, row(rwkv_r_k[0]), row(rwkv_ln_w[0]),
                   row(rwkv_ln_b[0]))
    x = _post(x, y, rwkv_w_o[0].astype(BF16), gt_m, row(norm_ffn[1]), sh_f, sc_f, gt_f,
              w_gate[1], w_up[1], w_down[1], g_final, final_norm=True)
    return x
```

```python
import functools

import jax
import jax.numpy as jnp
import numpy as np
from jax import lax
from jax.experimental import pallas as pl
from jax.experimental.pallas import tpu as pltpu

F32 = jnp.float32
BF16 = jnp.bfloat16

D = 1024
HEAD = 64
PAIR = 2 * HEAD
N_PAIR_ATTN = 8
FOX_W = 512
D_FF = 2816
MOBA_BLOCK = 256
N_MOBA_BLOCKS = 16
MOBA_TOPK = 3
AUG = 128
AUG_PER_HEAD = 16
NORM_EPS = 1e-6
GN_EPS = HEAD * 1e-5
LOG2E = 1.4426950408889634
NEG = -1e30
PEN = -30000.0
CHUNK = 64
SCAN_GROUP = 8
SCAN_BLOCK = 8 * SCAN_GROUP * CHUNK
RWKV_PROJ_COLS = 256
VMEM_LIMIT = 56 * 1024 * 1024


def _mm(a, b):
    return jnp.dot(a.astype(BF16), b.astype(BF16), preferred_element_type=F32)


def _mm_nt(a, b):
    return lax.dot_general(a.astype(BF16), b.astype(BF16), (((1,), (1,)), ((), ())),
                           preferred_element_type=F32)


def _mm_tn(a, b):
    return lax.dot_general(a.astype(BF16), b.astype(BF16), (((0,), (0,)), ((), ())),
                           preferred_element_type=F32)


def _sigmoid(x):
    return 1.0 / (1.0 + jnp.exp(-x))


def _softplus(x):
    return jnp.maximum(x, 0.0) + jnp.log(1.0 + jnp.exp(-jnp.abs(x)))


def _norm_mod(x, g, sh, sc):
    y = x * lax.rsqrt(jnp.mean(x * x, axis=-1, keepdims=True) + NORM_EPS)
    return (y * g) * (1.0 + sc) + sh


def _mod_kernel(c_ref, w_ref, b_ref, o_ref):
    c = c_ref[...]
    cond = c * _sigmoid(c)
    o_ref[0] = jnp.dot(cond, w_ref[0], precision=lax.Precision.HIGHEST,
                       preferred_element_type=F32) + b_ref[0]


def _modulation(c, mod_w, mod_b):
    depth, _, n = mod_w.shape
    b = c.shape[0]
    rows = 8
    c_pad = jnp.zeros((rows, D), F32).at[:b].set(c)
    tn = 1536
    out = pl.pallas_call(
        _mod_kernel,
        grid=(depth, n // tn),
        in_specs=[pl.BlockSpec((rows, D), lambda l, j: (0, 0)),
                  pl.BlockSpec((1, D, tn), lambda l, j: (l, 0, j)),
                  pl.BlockSpec((1, 1, tn), lambda l, j: (l, 0, j))],
        out_specs=pl.BlockSpec((1, rows, tn), lambda l, j: (l, 0, j)),
        out_shape=jax.ShapeDtypeStruct((depth, rows, n), F32),
        compiler_params=pltpu.CompilerParams(
            dimension_semantics=("arbitrary", "arbitrary"), vmem_limit_bytes=VMEM_LIMIT),
        name="mod",
    )(c_pad, mod_w, mod_b.reshape(depth, 1, n))
    return out[:, :b]


def _split3(z):
    z1 = z.astype(BF16)
    r1 = z - z1.astype(F32)
    z2 = r1.astype(BF16)
    return z1, z2, (r1 - z2.astype(F32)).astype(BF16)


def _mm_split3_lhs_exact(a_exact, b):
    aa = a_exact.astype(BF16)
    b1, b2, b3 = _split3(b)
    return (jnp.dot(aa, b1, preferred_element_type=F32)
            + jnp.dot(aa, b2, preferred_element_type=F32)
            + jnp.dot(aa, b3, preferred_element_type=F32))


def _moba_penalty(gate, own):
    lane = lax.broadcasted_iota(jnp.int32, gate.shape, 1)
    n = lane & (N_MOBA_BLOCKS - 1)
    past = n < own
    g = jnp.where(past, gate, -jnp.inf)
    rank = jnp.zeros(gate.shape, jnp.int32)
    for k in range(1, N_MOBA_BLOCKS):
        gk = jnp.where(n >= k, pltpu.roll(g, k, 1), pltpu.roll(g, AUG - N_MOBA_BLOCKS + k, 1))
        beats = (gk > g) | ((gk == g) & (n >= k))
        rank = rank + beats.astype(jnp.int32)
    keep = (past & (rank < MOBA_TOPK)) | (n == own)
    return jnp.where(keep, 0.0, PEN)


def _inproj_kernel(x_ref, g_ref, sh_ref, sc_ref, w_ref, wf_ref, bf_ref, sel_c_ref, const_c_ref,
                   sel_p_ref, qa_ref, ka_ref, vt_ref, carry_ref, km_ref):
    tm = x_ref.shape[1]
    si = pl.program_id(1)

    @pl.when(si == 0)
    def _():
        carry_ref[...] = jnp.zeros_like(carry_ref)
        km_ref[...] = jnp.zeros_like(km_ref)

    h = _norm_mod(x_ref[0], g_ref[...], sh_ref[0], sc_ref[0]).astype(BF16)
    proj = jnp.dot(h, w_ref[...], preferred_element_type=F32)
    q_fox, k_fox = proj[:, 0:FOX_W], proj[:, FOX_W:2 * FOX_W]
    q_mb, k_mb = proj[:, 3 * FOX_W:4 * FOX_W], proj[:, 4 * FOX_W:5 * FOX_W]
    v_all = jnp.concatenate([proj[:, 2 * FOX_W:3 * FOX_W], proj[:, 5 * FOX_W:6 * FOX_W]], axis=1)
    vt_ref[0, :, 0] = v_all.T.reshape(N_PAIR_ATTN, PAIR, tm).astype(BF16)

    f = jnp.dot(h, wf_ref[...], preferred_element_type=F32) + bf_ref[...]
    lf = -_softplus(-f)
    row = lax.broadcasted_iota(jnp.int32, (tm, tm), 0)
    col = lax.broadcasted_iota(jnp.int32, (tm, tm), 1)
    tri = jnp.where(row >= col, 1.0, 0.0).astype(F32)
    cum = _mm_split3_lhs_exact(tri, lf) + carry_ref[...]
    carry_ref[...] = cum[tm - 1:tm, :]
    c3 = jnp.concatenate(_split3(cum * LOG2E), axis=1)
    aug_fox = jnp.dot(c3, sel_c_ref[...], preferred_element_type=F32) + const_c_ref[...]

    qb = q_mb.astype(BF16)
    km1, km2, km3 = _split3(km_ref[...])
    gate = (jnp.dot(qb, km1, preferred_element_type=F32)
            + jnp.dot(qb, km2, preferred_element_type=F32)
            + jnp.dot(qb, km3, preferred_element_type=F32))
    pen = _moba_penalty(gate, si).astype(BF16)
    aug_mb_q = jnp.dot(pen, sel_p_ref[...], preferred_element_type=F32)
    lane_p = lax.broadcasted_iota(jnp.int32, (tm, AUG), 1)
    aug_mb_k = jnp.where((lane_p < 2 * AUG_PER_HEAD) & ((lane_p & (N_MOBA_BLOCKS - 1)) == si),
                         1.0, 0.0).astype(BF16)

    kb_t = k_mb.astype(BF16).astype(F32).T
    er = lax.broadcasted_iota(jnp.int32, (tm, AUG), 1)
    place = jnp.where((er & (N_MOBA_BLOCKS - 1)) == si, 1.0 / tm, 0.0)
    upd = jnp.dot(kb_t.astype(BF16), place.astype(BF16), preferred_element_type=F32)
    d_head = lax.broadcasted_iota(jnp.int32, (FOX_W, AUG), 0) >> 6
    l_head = lax.broadcasted_iota(jnp.int32, (FOX_W, AUG), 1) >> 4
    km_ref[...] += jnp.where(d_head == l_head, upd, 0.0)

    for p in range(4):
        lo, hi = p * PAIR, (p + 1) * PAIR
        base = p * (PAIR + AUG)
        qa_ref[0, :, base:base + PAIR] = q_fox[:, lo:hi].astype(BF16)
        qa_ref[0, :, base + PAIR:base + PAIR + AUG] = aug_fox[:, lo:hi].astype(BF16)
        ka_ref[0, :, base:base + PAIR] = k_fox[:, lo:hi].astype(BF16)
        ka_ref[0, :, base + PAIR:base + PAIR + AUG] = aug_fox[:, FOX_W + lo:FOX_W + hi].astype(BF16)
        base += 4 * (PAIR + AUG)
        qa_ref[0, :, base:base + PAIR] = q_mb[:, lo:hi].astype(BF16)
        qa_ref[0, :, base + PAIR:base + PAIR + AUG] = aug_mb_q[:, lo:hi].astype(BF16)
        ka_ref[0, :, base:base + PAIR] = k_mb[:, lo:hi].astype(BF16)
        ka_ref[0, :, base + PAIR:base + PAIR + AUG] = aug_mb_k


def _bias_lane_selectors():
    sel_c = np.zeros((3 * AUG, 2 * FOX_W), np.float32)
    const_c = np.zeros((1, 2 * FOX_W), np.float32)
    sel_p = np.zeros((AUG, FOX_W), np.float32)
    for head in range(8):
        base = (head // 2) * AUG + (head % 2) * AUG_PER_HEAD
        for i in range(3):
            sel_c[i * AUG + head, base + i] = 1.0
            sel_c[i * AUG + head, FOX_W + base + 3 + i] = -1.0
            const_c[0, base + 3 + i] = 1.0
            const_c[0, FOX_W + base + i] = 1.0
        for n in range(N_MOBA_BLOCKS):
            sel_p[head * N_MOBA_BLOCKS + n, base + n] = 1.0
    return jnp.asarray(sel_c, BF16), jnp.asarray(const_c, F32), jnp.asarray(sel_p, BF16)


def _inproj(x, g, sh, sc, w_main, w_f, b_f):
    b, s, _ = x.shape
    tm = MOBA_BLOCK
    n = w_main.shape[1]
    width = N_PAIR_ATTN * (PAIR + AUG)
    sel_c, const_c, sel_p = _bias_lane_selectors()
    row3 = lambda bi, si: (bi, 0, 0)
    const2 = lambda bi, si: (0, 0)
    tile3 = lambda bi, si: (bi, si, 0)
    full = lambda arr: pl.BlockSpec(arr.shape, const2)
    return pl.pallas_call(
        _inproj_kernel,
        grid=(b, s // tm),
        in_specs=[pl.BlockSpec((1, tm, D), tile3),
                  pl.BlockSpec((1, D), const2),
                  pl.BlockSpec((1, 1, D), row3),
                  pl.BlockSpec((1, 1, D), row3),
                  pl.BlockSpec((D, n), const2),
                  pl.BlockSpec((D, AUG), const2),
                  pl.BlockSpec((1, AUG), const2),
                  full(sel_c), full(const_c), full(sel_p)],
        out_specs=[pl.BlockSpec((1, tm, width), tile3),
                   pl.BlockSpec((1, tm, width), tile3),
                   pl.BlockSpec((1, N_PAIR_ATTN, 1, PAIR, tm), lambda bi, si: (bi, 0, si, 0, 0))],
        out_shape=[jax.ShapeDtypeStruct((b, s, width), BF16),
                   jax.ShapeDtypeStruct((b, s, width), BF16),
                   jax.ShapeDtypeStruct((b, N_PAIR_ATTN, s // tm, PAIR, tm), BF16)],
        scratch_shapes=[pltpu.VMEM((1, AUG), F32), pltpu.VMEM((FOX_W, AUG), F32)],
        compiler_params=pltpu.CompilerParams(
            dimension_semantics=("arbitrary", "arbitrary"), vmem_limit_bytes=VMEM_LIMIT),
        name="inproj",
    )(x, g, sh, sc, w_main, w_f, b_f, sel_c, const_c, sel_p)


def _attn_kernel(qa_ref, ka_ref, vt_ref, o_ref, s_even, s_odd, macc_ref, mprev_ref, lacc_ref,
                 acc_ref):
    tq = qa_ref.shape[1]
    tk = vt_ref.shape[4]
    nq = s_even.shape[1] // tq
    j = pl.program_id(2)
    qa = qa_ref[0]
    lane = lax.broadcasted_iota(jnp.int32, (1, PAIR + AUG), 1)
    head0 = (lane < HEAD) | ((lane >= PAIR) & (lane < PAIR + AUG_PER_HEAD))
    head1 = ((lane >= HEAD) & (lane < PAIR)) | (
        (lane >= PAIR + AUG_PER_HEAD) & (lane < PAIR + 2 * AUG_PER_HEAD))
    m_prev = mprev_ref[...]
    macc_ref[...] = jnp.full(macc_ref.shape, NEG, F32)
    lacc_ref[...] = jnp.zeros_like(lacc_ref)
    acc_ref[...] = jnp.zeros_like(acc_ref)

    def score_tiles(s_ref, first_tile, n, diagonal):
        rows = n * tk
        start = pl.multiple_of(first_tile * tk, tk)
        k = ka_ref[0, pl.ds(start, rows), :]
        zero = jnp.zeros_like(k)
        k2 = jnp.concatenate([jnp.where(head0, k, zero), jnp.where(head1, k, zero)], axis=0)
        st = lax.dot_general(k2, qa, (((1,), (1,)), ((), ())), preferred_element_type=F32)
        full = rows - tk if diagonal else rows
        for h in range(2):
            macc = macc_ref[h]
            if full:
                part = st[h * rows:h * rows + full]
                s_ref[h, pl.ds(start, full), :] = part
                macc = jnp.maximum(macc, jnp.max(part.reshape(full // 8, 8, tq), axis=0))
            if diagonal:
                key_pos = lax.broadcasted_iota(jnp.int32, (tk, tq), 0)
                qry_pos = lax.broadcasted_iota(jnp.int32, (tk, tq), 1)
                part = jnp.where(key_pos <= qry_pos, st[h * rows + full:(h + 1) * rows], NEG)
                s_ref[h, pl.ds(start + full, tk), :] = part
                macc = jnp.maximum(macc, jnp.max(part.reshape(tk // 8, 8, tq), axis=0))
            macc_ref[h] = macc

    def pv_tile(s_ref, t):
        start = pl.multiple_of(t * tk, tk)
        vt = vt_ref[0, 0, t]
        ls, accs = [], []
        for h in range(2):
            p = jnp.exp2(s_ref[h, pl.ds(start, tk), :] - m_prev[h])
            ls.append(jnp.sum(p.reshape(tk // 8, 8, tq), axis=0))
            accs.append(jnp.dot(vt[h * HEAD:(h + 1) * HEAD, :], p.astype(BF16),
                                preferred_element_type=F32))
        return jnp.stack(ls), jnp.concatenate(accs, axis=0)

    def pv_tiles(s_ref, first_tile, n):
        parts = [pv_tile(s_ref, first_tile + t) for t in range(n)]
        while len(parts) > 1:
            parts = [(a[0] + b[0], a[1] + b[1]) for a, b in zip(parts[0::2], parts[1::2])]
        lacc_ref[...] += parts[0][0]
        acc_ref[...] += parts[0][1]

    def step(s_cur, s_prev):
        for bit in (8, 4, 2, 1):
            in_group = ((j & bit) != 0) & (j < nq)
            lowest = (j & (bit - 1)) == 0
            first = j & (-2 * bit)

            if bit > 1:
                @pl.when(in_group & jnp.logical_not(lowest))
                def _(bit=bit, first=first):
                    score_tiles(s_cur, first, bit, False)
                    pv_tiles(s_prev, first, bit)

            @pl.when(in_group & lowest)
            def _(bit=bit, first=first):
                score_tiles(s_cur, first, bit + 1, True)
                pv_tiles(s_prev, first, bit)

        @pl.when(j == 0)
        def _():
            score_tiles(s_cur, j, 1, True)

        @pl.when(j == nq)
        def _():
            for first in range(0, nq, 8):
                pv_tiles(s_prev, first, 8)

    @pl.when((j & 1) == 0)
    def _():
        step(s_even, s_odd)

    @pl.when((j & 1) == 1)
    def _():
        step(s_odd, s_even)

    mprev_ref[...] = jnp.max(macc_ref[...], axis=1, keepdims=True)

    @pl.when(j > 0)
    def _():
        l = jnp.sum(lacc_ref[...], axis=1, keepdims=True)
        acc = acc_ref[...]
        ot = jnp.concatenate([acc[:HEAD] / l[0], acc[HEAD:] / l[1]], axis=0)
        o_ref[0] = ot.T.astype(o_ref.dtype)


def _attention(qa, ka, vt):
    b, s, _ = qa.shape
    tq = 256
    nk, tk = vt.shape[2], vt.shape[4]
    nq = s // tq
    assert nq == 16, "the key-tile groups (8, 4, 2, 1) cover at most 15 fully-past tiles"
    return pl.pallas_call(
        _attn_kernel,
        grid=(b, N_PAIR_ATTN, nq + 1),
        in_specs=[pl.BlockSpec((1, tq, PAIR + AUG), lambda bi, p, j: (bi, jnp.minimum(j, nq - 1), p)),
                  pl.BlockSpec((1, s, PAIR + AUG), lambda bi, p, j: (bi, 0, p)),
                  pl.BlockSpec((1, 1, nk, PAIR, tk), lambda bi, p, j: (bi, p, 0, 0, 0))],
        out_specs=pl.BlockSpec((1, tq, PAIR), lambda bi, p, j: (bi, jnp.maximum(j - 1, 0), p)),
        out_shape=jax.ShapeDtypeStruct((b, s, N_PAIR_ATTN * PAIR), BF16),
        scratch_shapes=[pltpu.VMEM((2, s, tq), F32),
                        pltpu.VMEM((2, s, tq), F32),
                        pltpu.VMEM((2, 8, tq), F32),
                        pltpu.VMEM((2, 1, tq), F32),
                        pltpu.VMEM((2, 8, tq), F32),
                        pltpu.VMEM((PAIR, tq), F32)],
        compiler_params=pltpu.CompilerParams(
            dimension_semantics=("arbitrary", "arbitrary", "arbitrary"),
            vmem_limit_bytes=VMEM_LIMIT),
        name="attn",
    )(qa, ka, vt)


def _post_kernel(x_ref, a_ref, wo_ref, gtm_ref, g_ref, sh_ref, sc_ref, gtf_ref,
                 wg_ref, wu_ref, wd_ref, gfin_ref, o_ref, *, final_norm, n_ff_chunks):
    x1 = x_ref[0] + gtm_ref[0] * jnp.dot(a_ref[0], wo_ref[...], preferred_element_type=F32)
    h = _norm_mod(x1, g_ref[...], sh_ref[0], sc_ref[0]).astype(BF16)
    cw = D_FF // n_ff_chunks
    ff = jnp.zeros_like(x1)
    for c in range(n_ff_chunks):
        gate = jnp.dot(h, wg_ref[:, c * cw:(c + 1) * cw], preferred_element_type=F32)
        up = jnp.dot(h, wu_ref[:, c * cw:(c + 1) * cw], preferred_element_type=F32)
        act = (gate * _sigmoid(gate) * up).astype(BF16)
        ff = ff + jnp.dot(act, wd_ref[c * cw:(c + 1) * cw, :], preferred_element_type=F32)
    x2 = x1 + gtf_ref[0] * ff
    if final_norm:
        x2 = x2 * lax.rsqrt(jnp.mean(x2 * x2, axis=-1, keepdims=True) + NORM_EPS) * gfin_ref[...]
    o_ref[0] = x2


def _post(x, a, w_o, gt_m, g_ffn, sh_f, sc_f, gt_f, w_gate, w_up, w_down, g_final, final_norm):
    b, s, _ = x.shape
    tm = 256
    row3 = lambda bi, si: (bi, 0, 0)
    const2 = lambda bi, si: (0, 0)
    tile3 = lambda bi, si: (bi, si, 0)
    kern = functools.partial(_post_kernel, final_norm=final_norm, n_ff_chunks=2)
    return pl.pallas_call(
        kern,
        grid=(b, s // tm),
        in_specs=[pl.BlockSpec((1, tm, D), tile3),
                  pl.BlockSpec((1, tm, D), tile3),
                  pl.BlockSpec((D, D), const2),
                  pl.BlockSpec((1, 1, D), row3),
                  pl.BlockSpec((1, D), const2),
                  pl.BlockSpec((1, 1, D), row3),
                  pl.BlockSpec((1, 1, D), row3),
                  pl.BlockSpec((1, 1, D), row3),
                  pl.BlockSpec((D, D_FF), const2),
                  pl.BlockSpec((D, D_FF), const2),
                  pl.BlockSpec((D_FF, D), const2),
                  pl.BlockSpec((1, D), const2)],
        out_specs=pl.BlockSpec((1, tm, D), tile3),
        out_shape=jax.ShapeDtypeStruct((b, s, D), F32),
        compiler_params=pltpu.CompilerParams(
            dimension_semantics=("arbitrary", "arbitrary"), vmem_limit_bytes=VMEM_LIMIT),
        name="post_final" if final_norm else "post",
    )(x, a, w_o, gt_m, g_ffn, sh_f, sc_f, gt_f, w_gate, w_up, w_down, g_final)


def _rwkvproj_kernel(x_ref, g_ref, sh_ref, sc_ref, mu_ref, wr_ref, wk_ref, wv_ref,
                     w1_ref, w2_ref, a1_ref, a2_ref, g1_ref, g2_ref,
                     w0_ref, a0_ref, kk_ref, ka_ref,
                     r_ref, k_ref, v_ref, kkv_ref, a_ref, lw_ref, gate_ref, carry_ref):
    tm = x_ref.shape[1]

    @pl.when(pl.program_id(1) == 0)
    def _():
        carry_ref[...] = jnp.zeros_like(carry_ref)

    h = _norm_mod(x_ref[0], g_ref[...], sh_ref[0], sc_ref[0])
    rowi = lax.broadcasted_iota(jnp.int32, (tm, 1), 0)
    h_prev = jnp.where(rowi == 0, carry_ref[...], pltpu.roll(h, 1, 0))
    carry_ref[...] = h[tm - 1:tm, :]
    xx = h_prev - h
    mix = lambda i: (h + xx * mu_ref[i:i + 1, :]).astype(BF16)
    xr, xk, xv = mix(0), mix(2), mix(3)
    hid_w = jnp.tanh(_mm(mix(1), w1_ref[...])).astype(BF16)
    hid_a = _mm(mix(4), a1_ref[...]).astype(BF16)
    hid_g = _sigmoid(_mm(mix(5), g1_ref[...])).astype(BF16)
    cw = RWKV_PROJ_COLS
    for c in range(D // cw):
        cols = slice(c * cw, (c + 1) * cw)
        k = jnp.dot(xk, wk_ref[:, cols], preferred_element_type=F32)
        a = _sigmoid(a0_ref[:, cols] + jnp.dot(hid_a, a2_ref[:, cols], preferred_element_type=F32))
        wl = w0_ref[:, cols] + jnp.dot(hid_w, w2_ref[:, cols], preferred_element_type=F32)
        k_ref[0, :, cols] = (k * (1.0 + (a - 1.0) * ka_ref[:, cols])).astype(k_ref.dtype)
        kkv_ref[0, :, cols] = k * kk_ref[:, cols]
        a_ref[0, :, cols] = a
        lw_ref[0, :, cols] = -jnp.exp(-_softplus(-wl) - 0.5)
        r_ref[0, :, cols] = jnp.dot(xr, wr_ref[:, cols],
                                    preferred_element_type=F32).astype(r_ref.dtype)
        v_ref[0, :, cols] = jnp.dot(xv, wv_ref[:, cols],
                                    preferred_element_type=F32).astype(v_ref.dtype)
        gate_ref[0, :, cols] = jnp.dot(hid_g, g2_ref[:, cols],
                                       preferred_element_type=F32).astype(gate_ref.dtype)


def _rwkv_proj(x, g, sh, sc, mu, w_r, w_k, w_v, w1, w2, a1, a2, g1, g2, w0, a0, k_k, k_a):
    b, s, _ = x.shape
    tm = 256
    row3 = lambda bi, si: (bi, 0, 0)
    const2 = lambda bi, si: (0, 0)
    tile3 = lambda bi, si: (bi, si, 0)
    full = lambda arr: pl.BlockSpec(arr.shape, const2)
    outs = pl.pallas_call(
        _rwkvproj_kernel,
        grid=(b, s // tm),
        in_specs=[pl.BlockSpec((1, tm, D), tile3), full(g),
                  pl.BlockSpec((1, 1, D), row3), pl.BlockSpec((1, 1, D), row3),
                  full(mu), full(w_r), full(w_k), full(w_v),
                  full(w1), full(w2), full(a1), full(a2), full(g1), full(g2),
                  full(w0), full(a0), full(k_k), full(k_a)],
        out_specs=[pl.BlockSpec((1, tm, D), tile3)] * 7,
        out_shape=[jax.ShapeDtypeStruct((b, s, D), dt) for dt in (BF16, BF16, BF16, F32, F32, F32, BF16)],
        scratch_shapes=[pltpu.VMEM((1, D), F32)],
        compiler_params=pltpu.CompilerParams(
            dimension_semantics=("arbitrary", "arbitrary"), vmem_limit_bytes=VMEM_LIMIT),
        name="rwkv_proj",
    )(x, g, sh, sc, mu, w_r, w_k, w_v, w1, w2, a1, a2, g1, g2, w0, a0, k_k, k_a)
    return outs


def _scan_kernel(r_ref, k_ref, v_ref, kkv_ref, a_ref, lw_ref, gate_ref,
                 rk_ref, lnw_ref, lnb_ref, o_ref, state_ref):
    tc = r_ref.shape[1]
    group_rows = SCAN_GROUP * CHUNK
    n_groups = tc // group_rows

    @pl.when(pl.program_id(2) == 0)
    def _():
        state_ref[...] = jnp.zeros_like(state_ref)

    li = lax.broadcasted_iota(jnp.int32, (PAIR, PAIR), 0)
    lj = lax.broadcasted_iota(jnp.int32, (PAIR, PAIR), 1)
    ones_bd = jnp.where((li >> 6) == (lj >> 6), 1.0, 0.0).astype(BF16)
    strict = (li & (CHUNK - 1)) > (lj & (CHUNK - 1))
    incl = (li & (CHUNK - 1)) >= (lj & (CHUNK - 1))
    eye = jnp.where(li == lj, 1.0, 0.0).astype(F32)
    ti = lax.broadcasted_iota(jnp.int32, (group_rows, group_rows), 0)
    tj = lax.broadcasted_iota(jnp.int32, (group_rows, group_rows), 1)
    tri = jnp.where((ti >= tj) & ((ti >> 6) == (tj >> 6)), 1.0, 0.0).astype(BF16)
    is_h0 = lax.broadcasted_iota(jnp.int32, (1, PAIR), 1) < HEAD
    chunks = range(SCAN_GROUP)

    def stack(z):
        return jnp.concatenate([jnp.where(is_h0, z, 0.0), jnp.where(is_h0, 0.0, z)], axis=0)

    def state_free_part(g, out):
        rows = slice(g * group_rows, (g + 1) * group_rows)
        r = r_ref[0, rows, :].astype(F32)
        k = k_ref[0, rows, :].astype(F32)
        v = v_ref[0, rows, :].astype(F32)
        kkv = kkv_ref[0, rows, :]
        lw = lw_ref[0, rows, :]
        lw1 = lw.astype(BF16)
        lw2 = (lw - lw1.astype(F32)).astype(BF16)
        cl = (jnp.dot(tri, lw1, preferred_element_type=F32)
              + jnp.dot(tri, lw2, preferred_element_type=F32))
        n2 = _mm(kkv * kkv, ones_bd)
        kk = kkv / jnp.maximum(jnp.sqrt(n2), 1e-12)
        bvec = kk * a_ref[0, rows, :]
        egi = jnp.exp(-cl)
        a_t = -kk * jnp.exp(cl - lw)
        b_t = bvec * egi
        k_t = k * egi
        r_t = r * jnp.exp(cl)
        out["bonus"] = _mm(r * k * rk_ref[...], ones_bd) * v
        sls = [slice(c * CHUNK, (c + 1) * CHUNK) for c in chunks]
        cl_end = [cl[(c + 1) * CHUNK - 1:(c + 1) * CHUNK, :] for c in chunks]
        out["decay"] = [jnp.exp(ce) for ce in cl_end]
        tails = [jnp.exp(cl_end[c] - cl[sls[c]]) for c in chunks]
        a_s = [stack(a_t[sl]) for sl in sls]
        r_s = [stack(r_t[sl]) for sl in sls]
        v_s = [stack(v[sl]) for sl in sls]
        bh_s = [stack(bvec[sls[c]] * tails[c]) for c in chunks]
        kh_s = [stack(k[sls[c]] * tails[c]) for c in chunks]
        grams = [_mm_nt(jnp.concatenate([a_s[c], r_s[c]], axis=0),
                        jnp.concatenate([stack(b_t[sls[c]]), stack(k_t[sls[c]])], axis=0))
                 for c in chunks]
        yield
        m_ak = [jnp.where(strict, gm[:PAIR, PAIR:], 0.0) for gm in grams]
        a_r = [jnp.concatenate([jnp.where(incl, gm[PAIR:, :PAIR], 0.0),
                                jnp.where(incl, gm[PAIR:, PAIR:], 0.0)], axis=1) for gm in grams]
        rhs = [jnp.concatenate([a_s[c], _mm(m_ak[c], v_s[c])], axis=1) for c in chunks]
        mps = [jnp.where(strict, gm[:PAIR, :PAIR], 0.0) for gm in grams]
        ts = [eye + mp for mp in mps]
        mps = [_mm(mp, mp) for mp in mps]
        yield
        for level in range(1, 6):
            if level < 5:
                res = [_mm(mps[c], jnp.concatenate([mps[c], ts[c]], axis=1)) for c in chunks]
                mps = [rs[:, :PAIR] for rs in res]
                ts = [ts[c] + res[c][:, PAIR:] for c in chunks]
            else:
                ts = [ts[c] + _mm(mps[c], ts[c]) for c in chunks]
            yield
        xs = [_mm(ts[c], rhs[c]) for c in chunks]
        gys = [_mm(a_r[c], jnp.concatenate(
            [xs[c], jnp.concatenate([jnp.zeros_like(v_s[c]), v_s[c]], axis=1)], axis=0))
               for c in chunks]
        out["g"] = [r_s[c] + gys[c][:, :PAIR] for c in chunks]
        out["y0"] = [gy[:, PAIR:] for gy in gys]
        yield
        pqs = [_mm_tn(xs[c], bh_s[c]) for c in chunks]
        out["pb"] = [pq[:PAIR] for pq in pqs]
        out["q"] = [pqs[c][PAIR:] + _mm_tn(v_s[c], kh_s[c]) for c in chunks]
        yield

    carried_state = [state_ref[...]]

    def state_part(g, res):
        ys = []
        for c in chunks:
            state = carried_state[0]
            y_s = _mm_nt(res["g"][c], state) + res["y0"][c]
            ys.append(y_s[:CHUNK] + y_s[CHUNK:])
            carried_state[0] = state * res["decay"][c] + _mm(state, res["pb"][c]) + res["q"][c]
            yield
        rows = slice(g * group_rows, (g + 1) * group_rows)
        y = jnp.concatenate(ys, axis=0)
        inv_n = 1.0 / HEAD
        mean = _mm(y, ones_bd) * inv_n
        yc = y - mean
        var = _mm(yc * yc, ones_bd) * inv_n
        yn = yc * lax.rsqrt(var + GN_EPS) * lnw_ref[...] + lnb_ref[...]
        o_ref[0, rows, :] = ((yn + res["bonus"]) * gate_ref[0, rows, :].astype(F32)
                             ).astype(o_ref.dtype)
        yield

    results = [dict() for _ in range(n_groups)]
    for _ in state_free_part(0, results[0]):
        pass
    for g in range(n_groups):
        carried = state_part(g, results[g])
        ahead = state_free_part(g + 1, results[g + 1]) if g + 1 < n_groups else iter(())
        n_carried, n_ahead = SCAN_GROUP + 1, 9
        for i, _ in enumerate(ahead):
            if (i * n_carried) // n_ahead != ((i + 1) * n_carried) // n_ahead:
                next(carried, None)
        for _ in carried:
            pass
    state_ref[...] = carried_state[0]


def _rwkv_scan(r, k, v, kkv, a, lw, gate, r_k, ln_w, ln_b):
    b, s, _ = r.shape
    tc = SCAN_BLOCK
    tile = pl.BlockSpec((1, tc, PAIR), lambda bi, p, ti: (bi, ti, p))
    vec = pl.BlockSpec((1, PAIR), lambda bi, p, ti: (0, p))
    return pl.pallas_call(
        _scan_kernel,
        grid=(b, D // PAIR, s // tc),
        in_specs=[tile] * 7 + [vec] * 3,
        out_specs=tile,
        out_shape=jax.ShapeDtypeStruct((b, s, D), BF16),
        scratch_shapes=[pltpu.VMEM((PAIR, PAIR), F32)],
        compiler_params=pltpu.CompilerParams(
            dimension_semantics=("arbitrary", "arbitrary", "arbitrary"),
            vmem_limit_bytes=VMEM_LIMIT),
        name="rwkv_scan",
    )(r, k, v, kkv, a, lw, gate, r_k, ln_w, ln_b)


def kernel(x, c, attn_w_in, attn_b_f, attn_w_out, rwkv_mu, rwkv_w_r, rwkv_w_k, rwkv_w_v, rwkv_w_o, rwkv_w0, rwkv_w1, rwkv_w2, rwkv_a0, rwkv_a1, rwkv_a2, rwkv_g1, rwkv_g2, rwkv_k_k, rwkv_k_a, rwkv_r_k, rwkv_ln_w, rwkv_ln_b, mod_w, mod_b, norm_mix, norm_ffn, ffn_w_in, ffn_w_out, norm_final):
    b, s, _ = x.shape
    mod = _modulation(c, mod_w, mod_b)
    mods = [[mod[l, :, i * D:(i + 1) * D].reshape(b, 1, D) for i in range(6)] for l in range(2)]
    row = lambda t: t.reshape(1, D)
    g_final = row(norm_final)

    sh_m, sc_m, gt_m, sh_f, sc_f, gt_f = mods[0]
    w_in = attn_w_in[0]
    qscale = HEAD ** -0.5 * LOG2E
    col_scale = jnp.ones((3 * D,), F32)
    col_scale = col_scale.at[0:FOX_W].set(qscale).at[3 * FOX_W:4 * FOX_W].set(qscale)
    w_main = (w_in[:, :3 * D] * col_scale[None, :]).astype(BF16)
    w_f = jnp.zeros((D, AUG), F32).at[:, :8].set(w_in[:, 3 * D:]).astype(BF16)
    b_f = jnp.zeros((1, AUG), F32).at[0, :8].set(attn_b_f[0])
    qa, ka, vt = _inproj(x, row(norm_mix[0]), sh_m, sc_m, w_main, w_f, b_f)
    o = _attention(qa, ka, vt)

    w_gate = ffn_w_in[:, :, :D_FF].astype(BF16)
    w_up = ffn_w_in[:, :, D_FF:].astype(BF16)
    w_down = ffn_w_out.astype(BF16)
    x = _post(x, o, attn_w_out[0].astype(BF16), gt_m, row(norm_ffn[0]), sh_f, sc_f, gt_f,
              w_gate[0], w_up[0], w_down[0], g_final, final_norm=False)

    sh_m, sc_m, gt_m, sh_f, sc_f, gt_f = mods[1]
    pad_in = lambda w, n: jnp.zeros((D, n), BF16).at[:, :w.shape[1]].set(w.astype(BF16))
    pad_out = lambda w, n: jnp.zeros((n, D), BF16).at[:w.shape[0], :].set(w.astype(BF16))
    r, k, v, kkv, a, lw, gate = _rwkv_proj(
        x, row(norm_mix[1]), sh_m, sc_m, rwkv_mu[0],
        rwkv_w_r[0].astype(BF16), rwkv_w_k[0].astype(BF16), rwkv_w_v[0].astype(BF16),
        pad_in(rwkv_w1[0], 128), pad_out(rwkv_w2[0], 128),
        pad_in(rwkv_a1[0], 128), pad_out(rwkv_a2[0], 128),
        pad_in(rwkv_g1[0], 256), pad_out(rwkv_g2[0], 256),
        row(rwkv_w0[0]), row(rwkv_a0[0]), row(rwkv_k_k[0]), row(rwkv_k_a[0]))
    y = _rwkv_scan(r, k, v, kkv, a, lw, gate, row(rwkv_r_k[0]), row(rwkv_ln_w[0]),
                   row(rwkv_ln_b[0]))
    x = _post(x, y, rwkv_w_o[0].astype(BF16), gt_m, row(norm_ffn[1]), sh_f, sc_f, gt_f,
              w_gate[1], w_up[1], w_down[1], g_final, final_norm=True)
    return x
```

```python
import functools

import jax
import jax.numpy as jnp
import numpy as np
from jax import lax
from jax.experimental import pallas as pl
from jax.experimental.pallas import tpu as pltpu

F32 = jnp.float32
BF16 = jnp.bfloat16

D = 1024
HEAD = 64
PAIR = 2 * HEAD
N_PAIR_ATTN = 8
FOX_W = 512
D_FF = 2816
MOBA_BLOCK = 256
N_MOBA_BLOCKS = 16
MOBA_TOPK = 3
AUG = 128
AUG_PER_HEAD = 16
NORM_EPS = 1e-6
GN_EPS = HEAD * 1e-5
LOG2E = 1.4426950408889634
NEG = -1e30
PEN = -30000.0
CHUNK = 64
SCAN_GROUP = 8
SCAN_BLOCK = 8 * SCAN_GROUP * CHUNK
ATTN_TQ = 256
RWKV_PROJ_COLS = 256
VMEM_LIMIT = 56 * 1024 * 1024


def _mm(a, b):
    return jnp.dot(a.astype(BF16), b.astype(BF16), preferred_element_type=F32)


def _mm_nt(a, b):
    return lax.dot_general(a.astype(BF16), b.astype(BF16), (((1,), (1,)), ((), ())),
                           preferred_element_type=F32)


def _mm_tn(a, b):
    return lax.dot_general(a.astype(BF16), b.astype(BF16), (((0,), (0,)), ((), ())),
                           preferred_element_type=F32)


def _sigmoid(x):
    return 1.0 / (1.0 + jnp.exp(-x))


def _softplus(x):
    return jnp.maximum(x, 0.0) + jnp.log(1.0 + jnp.exp(-jnp.abs(x)))


def _norm_mod(x, g, sh, sc):
    y = x * lax.rsqrt(jnp.mean(x * x, axis=-1, keepdims=True) + NORM_EPS)
    return (y * g) * (1.0 + sc) + sh


def _mod_kernel(c_ref, w_ref, b_ref, o_ref):
    c = c_ref[...]
    cond = c * _sigmoid(c)
    o_ref[0] = jnp.dot(cond, w_ref[0], precision=lax.Precision.HIGHEST,
                       preferred_element_type=F32) + b_ref[0]


def _modulation(c, mod_w, mod_b):
    depth, _, n = mod_w.shape
    b = c.shape[0]
    rows = 8
    c_pad = jnp.zeros((rows, D), F32).at[:b].set(c)
    tn = 1536
    out = pl.pallas_call(
        _mod_kernel,
        grid=(depth, n // tn),
        in_specs=[pl.BlockSpec((rows, D), lambda l, j: (0, 0)),
                  pl.BlockSpec((1, D, tn), lambda l, j: (l, 0, j)),
                  pl.BlockSpec((1, 1, tn), lambda l, j: (l, 0, j))],
        out_specs=pl.BlockSpec((1, rows, tn), lambda l, j: (l, 0, j)),
        out_shape=jax.ShapeDtypeStruct((depth, rows, n), F32),
        compiler_params=pltpu.CompilerParams(
            dimension_semantics=("arbitrary", "arbitrary"), vmem_limit_bytes=VMEM_LIMIT),
        name="mod",
    )(c_pad, mod_w, mod_b.reshape(depth, 1, n))
    return out[:, :b]


def _split3(z):
    z1 = z.astype(BF16)
    r1 = z - z1.astype(F32)
    z2 = r1.astype(BF16)
    return z1, z2, (r1 - z2.astype(F32)).astype(BF16)


def _mm_split3_lhs_exact(a_exact, b):
    aa = a_exact.astype(BF16)
    b1, b2, b3 = _split3(b)
    return (jnp.dot(aa, b1, preferred_element_type=F32)
            + jnp.dot(aa, b2, preferred_element_type=F32)
            + jnp.dot(aa, b3, preferred_element_type=F32))


def _moba_penalty_steps(gate, own, out):
    lane = lax.broadcasted_iota(jnp.int32, gate.shape, 1)
    n = lane & (N_MOBA_BLOCKS - 1)
    past = n < own
    g = jnp.where(past, gate, -jnp.inf)
    rank = jnp.zeros(gate.shape, jnp.int32)
    for k in range(1, N_MOBA_BLOCKS):
        gk = jnp.where(n >= k, pltpu.roll(g, k, 1), pltpu.roll(g, AUG - N_MOBA_BLOCKS + k, 1))
        beats = (gk > g) | ((gk == g) & (n >= k))
        rank = rank + beats.astype(jnp.int32)
        yield
    keep = (past & (rank < MOBA_TOPK)) | (n == own)
    out[0] = jnp.where(keep, 0.0, PEN)


def _inproj_kernel(x_ref, g_ref, sh_ref, sc_ref, w_ref, wf_ref, bf_ref, sel_c_ref, const_c_ref,
                   sel_p_ref, qa_ref, ka_ref, vt_ref, carry_ref, km_ref):
    tm = x_ref.shape[1]
    si = pl.program_id(1)

    @pl.when(si == 0)
    def _():
        carry_ref[...] = jnp.zeros_like(carry_ref)
        km_ref[...] = jnp.zeros_like(km_ref)

    h = _norm_mod(x_ref[0], g_ref[...], sh_ref[0], sc_ref[0]).astype(BF16)
    width = PAIR + AUG

    def proj(lo, hi):
        return jnp.dot(h, w_ref[:, lo:hi], preferred_element_type=F32)

    def put(ref, pair, value):
        ref[0, :, pair * width:pair * width + PAIR] = value.astype(BF16)

    def put_bias(ref, pair, value):
        ref[0, :, pair * width + PAIR:(pair + 1) * width] = value.astype(BF16)

    q_mb = proj(3 * FOX_W, 4 * FOX_W)
    qb = q_mb.astype(BF16)
    km1, km2, km3 = _split3(km_ref[...])
    gate = (jnp.dot(qb, km1, preferred_element_type=F32)
            + jnp.dot(qb, km2, preferred_element_type=F32)
            + jnp.dot(qb, km3, preferred_element_type=F32))
    pen = [None]
    ranking = _moba_penalty_steps(gate, si, pen)

    def fox_bias():
        f = jnp.dot(h, wf_ref[...], preferred_element_type=F32) + bf_ref[...]
        lf = -_softplus(-f)
        row = lax.broadcasted_iota(jnp.int32, (tm, tm), 0)
        col = lax.broadcasted_iota(jnp.int32, (tm, tm), 1)
        tri = jnp.where(row >= col, 1.0, 0.0).astype(F32)
        cum = _mm_split3_lhs_exact(tri, lf) + carry_ref[...]
        carry_ref[...] = cum[tm - 1:tm, :]
        c3 = jnp.concatenate(_split3(cum * LOG2E), axis=1)
        aug = jnp.dot(c3, sel_c_ref[...], preferred_element_type=F32) + const_c_ref[...]
        for p in range(4):
            put_bias(qa_ref, p, aug[:, p * PAIR:(p + 1) * PAIR])
            put_bias(ka_ref, p, aug[:, FOX_W + p * PAIR:FOX_W + (p + 1) * PAIR])

    def fox_qk(half):
        q = proj(half * 2 * PAIR, (half + 1) * 2 * PAIR)
        k = proj(FOX_W + half * 2 * PAIR, FOX_W + (half + 1) * 2 * PAIR)
        for i in range(2):
            put(qa_ref, 2 * half + i, q[:, i * PAIR:(i + 1) * PAIR])
            put(ka_ref, 2 * half + i, k[:, i * PAIR:(i + 1) * PAIR])

    def values(group, half):
        lo = (2 + 3 * group) * FOX_W + half * 2 * PAIR
        v = proj(lo, lo + 2 * PAIR)
        first = 4 * group + 2 * half
        vt_ref[0, first:first + 2, 0] = v.T.reshape(2, PAIR, tm).astype(BF16)

    def moba_k():
        k_mb = proj(4 * FOX_W, 5 * FOX_W)
        lane_p = lax.broadcasted_iota(jnp.int32, (tm, AUG), 1)
        onehot = jnp.where((lane_p < 2 * AUG_PER_HEAD) & ((lane_p & (N_MOBA_BLOCKS - 1)) == si),
                           1.0, 0.0)
        for p in range(4):
            put(ka_ref, 4 + p, k_mb[:, p * PAIR:(p + 1) * PAIR])
            put_bias(ka_ref, 4 + p, onehot)
        kb_t = k_mb.astype(BF16).astype(F32).T
        place = jnp.where((lane_p & (N_MOBA_BLOCKS - 1)) == si, 1.0 / tm, 0.0)
        upd = jnp.dot(kb_t.astype(BF16), place.astype(BF16), preferred_element_type=F32)
        d_head = lax.broadcasted_iota(jnp.int32, (FOX_W, AUG), 0) >> 6
        l_head = lax.broadcasted_iota(jnp.int32, (FOX_W, AUG), 1) >> 4
        km_ref[...] += jnp.where(d_head == l_head, upd, 0.0)

    others = [fox_bias, lambda: fox_qk(0), lambda: fox_qk(1), moba_k,
              lambda: values(0, 0), lambda: values(0, 1), lambda: values(1, 0),
              lambda: values(1, 1)]
    for i, _ in enumerate(ranking):
        if i % 2 == 0 and others:
            others.pop(0)()
    for task in others:
        task()
    aug_mb_q = jnp.dot(pen[0].astype(BF16), sel_p_ref[...], preferred_element_type=F32)
    for p in range(4):
        put(qa_ref, 4 + p, q_mb[:, p * PAIR:(p + 1) * PAIR])
        put_bias(qa_ref, 4 + p, aug_mb_q[:, p * PAIR:(p + 1) * PAIR])


def _bias_lane_selectors():
    sel_c = np.zeros((3 * AUG, 2 * FOX_W), np.float32)
    const_c = np.zeros((1, 2 * FOX_W), np.float32)
    sel_p = np.zeros((AUG, FOX_W), np.float32)
    for head in range(8):
        base = (head // 2) * AUG + (head % 2) * AUG_PER_HEAD
        for i in range(3):
            sel_c[i * AUG + head, base + i] = 1.0
            sel_c[i * AUG + head, FOX_W + base + 3 + i] = -1.0
            const_c[0, base + 3 + i] = 1.0
            const_c[0, FOX_W + base + i] = 1.0
        for n in range(N_MOBA_BLOCKS):
            sel_p[head * N_MOBA_BLOCKS + n, base + n] = 1.0
    return jnp.asarray(sel_c, BF16), jnp.asarray(const_c, F32), jnp.asarray(sel_p, BF16)


def _inproj(x, g, sh, sc, w_main, w_f, b_f):
    b, s, _ = x.shape
    tm = MOBA_BLOCK
    n = w_main.shape[1]
    width = N_PAIR_ATTN * (PAIR + AUG)
    sel_c, const_c, sel_p = _bias_lane_selectors()
    row3 = lambda bi, si: (bi, 0, 0)
    const2 = lambda bi, si: (0, 0)
    tile3 = lambda bi, si: (bi, si, 0)
    full = lambda arr: pl.BlockSpec(arr.shape, const2)
    return pl.pallas_call(
        _inproj_kernel,
        grid=(b, s // tm),
        in_specs=[pl.BlockSpec((1, tm, D), tile3),
                  pl.BlockSpec((1, D), const2),
                  pl.BlockSpec((1, 1, D), row3),
                  pl.BlockSpec((1, 1, D), row3),
                  pl.BlockSpec((D, n), const2),
                  pl.BlockSpec((D, AUG), const2),
                  pl.BlockSpec((1, AUG), const2),
                  full(sel_c), full(const_c), full(sel_p)],
        out_specs=[pl.BlockSpec((1, tm, width), tile3),
                   pl.BlockSpec((1, tm, width), tile3),
                   pl.BlockSpec((1, N_PAIR_ATTN, 1, PAIR, tm), lambda bi, si: (bi, 0, si, 0, 0))],
        out_shape=[jax.ShapeDtypeStruct((b, s, width), BF16),
                   jax.ShapeDtypeStruct((b, s, width), BF16),
                   jax.ShapeDtypeStruct((b, N_PAIR_ATTN, s // tm, PAIR, tm), BF16)],
        scratch_shapes=[pltpu.VMEM((1, AUG), F32), pltpu.VMEM((FOX_W, AUG), F32)],
        compiler_params=pltpu.CompilerParams(
            dimension_semantics=("arbitrary", "arbitrary"), vmem_limit_bytes=VMEM_LIMIT),
        name="inproj",
    )(x, g, sh, sc, w_main, w_f, b_f, sel_c, const_c, sel_p)


def _attn_kernel(qa_ref, ka_ref, vt_ref, o_ref, s_even, s_odd, macc_ref, mprev_ref, lacc_ref,
                 acc_ref):
    tq = qa_ref.shape[1]
    tk = vt_ref.shape[4]
    unit = tq // tk
    nq = s_even.shape[1] // tq
    j = pl.program_id(2)
    qa = qa_ref[0]
    lane = lax.broadcasted_iota(jnp.int32, (1, PAIR + AUG), 1)
    head0 = (lane < HEAD) | ((lane >= PAIR) & (lane < PAIR + AUG_PER_HEAD))
    head1 = ((lane >= HEAD) & (lane < PAIR)) | (
        (lane >= PAIR + AUG_PER_HEAD) & (lane < PAIR + 2 * AUG_PER_HEAD))
    m_prev = mprev_ref[...]
    macc_ref[...] = jnp.full(macc_ref.shape, NEG, F32)
    lacc_ref[...] = jnp.zeros_like(lacc_ref)
    acc_ref[...] = jnp.zeros_like(acc_ref)

    def score_tiles(s_ref, first_tile, n, diagonal):
        rows = n * tk
        start = pl.multiple_of(first_tile * tk, tk)
        k = ka_ref[0, pl.ds(start, rows), :]
        zero = jnp.zeros_like(k)
        k2 = jnp.concatenate([jnp.where(head0, k, zero), jnp.where(head1, k, zero)], axis=0)
        st = lax.dot_general(k2, qa, (((1,), (1,)), ((), ())), preferred_element_type=F32)
        full = rows - tq if diagonal else rows
        for h in range(2):
            macc = macc_ref[h]
            if full:
                part = st[h * rows:h * rows + full]
                s_ref[h, pl.ds(start, full), :] = part
                macc = jnp.maximum(macc, jnp.max(part.reshape(full // 8, 8, tq), axis=0))
            if diagonal:
                key_pos = lax.broadcasted_iota(jnp.int32, (tq, tq), 0)
                qry_pos = lax.broadcasted_iota(jnp.int32, (tq, tq), 1)
                part = jnp.where(key_pos <= qry_pos, st[h * rows + full:(h + 1) * rows], NEG)
                s_ref[h, pl.ds(start + full, tq), :] = part
                macc = jnp.maximum(macc, jnp.max(part.reshape(tq // 8, 8, tq), axis=0))
            macc_ref[h] = macc

    def pv_tile(s_ref, t):
        start = pl.multiple_of(t * tk, tk)
        vt = vt_ref[0, 0, t]
        ls, accs = [], []
        for h in range(2):
            p = jnp.exp2(s_ref[h, pl.ds(start, tk), :] - m_prev[h])
            ls.append(jnp.sum(p.reshape(tk // 8, 8, tq), axis=0))
            accs.append(jnp.dot(vt[h * HEAD:(h + 1) * HEAD, :], p.astype(BF16),
                                preferred_element_type=F32))
        return jnp.stack(ls), jnp.concatenate(accs, axis=0)

    def pv_tiles(s_ref, first_tile, n):
        parts = [pv_tile(s_ref, first_tile + t) for t in range(n)]
        while len(parts) > 1:
            parts = [(a[0] + b[0], a[1] + b[1]) for a, b in zip(parts[0::2], parts[1::2])]
        lacc_ref[...] += parts[0][0]
        acc_ref[...] += parts[0][1]

    def step(s_cur, s_prev):
        bit = nq // 2
        while bit >= 1:
            in_group = ((j & bit) != 0) & (j < nq)
            lowest = (j & (bit - 1)) == 0
            first = (j & (-2 * bit)) * unit
            n = bit * unit

            if bit > 1:
                @pl.when(in_group & jnp.logical_not(lowest))
                def _(n=n, first=first):
                    score_tiles(s_cur, first, n, False)
                    pv_tiles(s_prev, first, n)

            @pl.when(in_group & lowest)
            def _(n=n, first=first):
                score_tiles(s_cur, first, n + unit, True)
                pv_tiles(s_prev, first, n)
            bit //= 2

        @pl.when(j == 0)
        def _():
            score_tiles(s_cur, 0, unit, True)

        @pl.when(j == nq)
        def _():
            for first in range(0, nq * unit, 8):
                pv_tiles(s_prev, first, 8)

    @pl.when((j & 1) == 0)
    def _():
        step(s_even, s_odd)

    @pl.when((j & 1) == 1)
    def _():
        step(s_odd, s_even)

    mprev_ref[...] = jnp.max(macc_ref[...], axis=1, keepdims=True)

    @pl.when(j > 0)
    def _():
        l = jnp.sum(lacc_ref[...], axis=1, keepdims=True)
        acc = acc_ref[...]
        ot = jnp.concatenate([acc[:HEAD] / l[0], acc[HEAD:] / l[1]], axis=0)
        o_ref[0] = ot.T.astype(o_ref.dtype)


def _attention(qa, ka, vt):
    b, s, _ = qa.shape
    tq = ATTN_TQ
    nk, tk = vt.shape[2], vt.shape[4]
    nq = s // tq
    assert nq & (nq - 1) == 0 and (nq * tq // tk) % 8 == 0, "query tiles must be a power of two"
    return pl.pallas_call(
        _attn_kernel,
        grid=(b, N_PAIR_ATTN, nq + 1),
        in_specs=[pl.BlockSpec((1, tq, PAIR + AUG), lambda bi, p, j: (bi, jnp.minimum(j, nq - 1), p)),
                  pl.BlockSpec((1, s, PAIR + AUG), lambda bi, p, j: (bi, 0, p)),
                  pl.BlockSpec((1, 1, nk, PAIR, tk), lambda bi, p, j: (bi, p, 0, 0, 0))],
        out_specs=pl.BlockSpec((1, tq, PAIR), lambda bi, p, j: (bi, jnp.maximum(j - 1, 0), p)),
        out_shape=jax.ShapeDtypeStruct((b, s, N_PAIR_ATTN * PAIR), BF16),
        scratch_shapes=[pltpu.VMEM((2, s, tq), F32),
                        pltpu.VMEM((2, s, tq), F32),
                        pltpu.VMEM((2, 8, tq), F32),
                        pltpu.VMEM((2, 1, tq), F32),
                        pltpu.VMEM((2, 8, tq), F32),
                        pltpu.VMEM((PAIR, tq), F32)],
        compiler_params=pltpu.CompilerParams(
            dimension_semantics=("arbitrary", "arbitrary", "arbitrary"),
            vmem_limit_bytes=VMEM_LIMIT),
        name="attn",
    )(qa, ka, vt)


def _post_kernel(x_ref, a_ref, wo_ref, gtm_ref, g_ref, sh_ref, sc_ref, gtf_ref,
                 win_ref, wd_ref, gfin_ref, o_ref, *, final_norm, n_ff_chunks):
    x1 = x_ref[0] + gtm_ref[0] * jnp.dot(a_ref[0], wo_ref[...], preferred_element_type=F32)
    h = _norm_mod(x1, g_ref[...], sh_ref[0], sc_ref[0]).astype(BF16)
    cw = D_FF // n_ff_chunks
    ff = jnp.zeros_like(x1)
    for c in range(n_ff_chunks):
        gate = jnp.dot(h, win_ref[0, :, c * cw:(c + 1) * cw], preferred_element_type=F32)
        up = jnp.dot(h, win_ref[0, :, D_FF + c * cw:D_FF + (c + 1) * cw],
                     preferred_element_type=F32)
        act = (gate * _sigmoid(gate) * up).astype(BF16)
        ff = ff + jnp.dot(act, wd_ref[0, c * cw:(c + 1) * cw, :], preferred_element_type=F32)
    x2 = x1 + gtf_ref[0] * ff
    if final_norm:
        x2 = x2 * lax.rsqrt(jnp.mean(x2 * x2, axis=-1, keepdims=True) + NORM_EPS) * gfin_ref[...]
    o_ref[0] = x2


def _post(x, a, w_o, gt_m, g_ffn, sh_f, sc_f, gt_f, w_in, w_down, layer, g_final, final_norm):
    b, s, _ = x.shape
    tm = 256
    row3 = lambda bi, si: (bi, 0, 0)
    const2 = lambda bi, si: (0, 0)
    tile3 = lambda bi, si: (bi, si, 0)
    of_layer = lambda bi, si: (layer, 0, 0)
    kern = functools.partial(_post_kernel, final_norm=final_norm, n_ff_chunks=2)
    return pl.pallas_call(
        kern,
        grid=(b, s // tm),
        in_specs=[pl.BlockSpec((1, tm, D), tile3),
                  pl.BlockSpec((1, tm, D), tile3),
                  pl.BlockSpec((D, D), const2),
                  pl.BlockSpec((1, 1, D), row3),
                  pl.BlockSpec((1, D), const2),
                  pl.BlockSpec((1, 1, D), row3),
                  pl.BlockSpec((1, 1, D), row3),
                  pl.BlockSpec((1, 1, D), row3),
                  pl.BlockSpec((1, D, 2 * D_FF), of_layer),
                  pl.BlockSpec((1, D_FF, D), of_layer),
                  pl.BlockSpec((1, D), const2)],
        out_specs=pl.BlockSpec((1, tm, D), tile3),
        out_shape=jax.ShapeDtypeStruct((b, s, D), F32),
        compiler_params=pltpu.CompilerParams(
            dimension_semantics=("arbitrary", "arbitrary"), vmem_limit_bytes=VMEM_LIMIT),
        name="post_final" if final_norm else "post",
    )(x, a, w_o, gt_m, g_ffn, sh_f, sc_f, gt_f, w_in, w_down, g_final)


def _rwkvproj_kernel(x_ref, g_ref, sh_ref, sc_ref, mu_ref, wr_ref, wk_ref, wv_ref,
                     w1_ref, w2_ref, a1_ref, a2_ref, g1_ref, g2_ref,
                     w0_ref, a0_ref, kk_ref, ka_ref,
                     r_ref, k_ref, v_ref, kkv_ref, a_ref, lw_ref, gate_ref, carry_ref):
    tm = x_ref.shape[1]

    @pl.when(pl.program_id(1) == 0)
    def _():
        carry_ref[...] = jnp.zeros_like(carry_ref)

    h = _norm_mod(x_ref[0], g_ref[...], sh_ref[0], sc_ref[0])
    rowi = lax.broadcasted_iota(jnp.int32, (tm, 1), 0)
    h_prev = jnp.where(rowi == 0, carry_ref[...], pltpu.roll(h, 1, 0))
    carry_ref[...] = h[tm - 1:tm, :]
    xx = h_prev - h
    mix = lambda i: (h + xx * mu_ref[i:i + 1, :]).astype(BF16)
    xr, xk, xv = mix(0), mix(2), mix(3)
    hid_w = jnp.tanh(_mm(mix(1), w1_ref[...])).astype(BF16)
    hid_a = _mm(mix(4), a1_ref[...]).astype(BF16)
    hid_g = _sigmoid(_mm(mix(5), g1_ref[...])).astype(BF16)
    cw = RWKV_PROJ_COLS
    for c in range(D // cw):
        cols = slice(c * cw, (c + 1) * cw)
        k = jnp.dot(xk, wk_ref[:, cols], preferred_element_type=F32)
        a = _sigmoid(a0_ref[:, cols] + jnp.dot(hid_a, a2_ref[:, cols], preferred_element_type=F32))
        wl = w0_ref[:, cols] + jnp.dot(hid_w, w2_ref[:, cols], preferred_element_type=F32)
        k_ref[0, :, cols] = (k * (1.0 + (a - 1.0) * ka_ref[:, cols])).astype(k_ref.dtype)
        kkv_ref[0, :, cols] = k * kk_ref[:, cols]
        a_ref[0, :, cols] = a
        lw_ref[0, :, cols] = -jnp.exp(-_softplus(-wl) - 0.5)
        r_ref[0, :, cols] = jnp.dot(xr, wr_ref[:, cols],
                                    preferred_element_type=F32).astype(r_ref.dtype)
        v_ref[0, :, cols] = jnp.dot(xv, wv_ref[:, cols],
                                    preferred_element_type=F32).astype(v_ref.dtype)
        gate_ref[0, :, cols] = jnp.dot(hid_g, g2_ref[:, cols],
                                       preferred_element_type=F32).astype(gate_ref.dtype)


def _rwkv_proj(x, g, sh, sc, mu, w_r, w_k, w_v, w1, w2, a1, a2, g1, g2, w0, a0, k_k, k_a):
    b, s, _ = x.shape
    tm = 256
    row3 = lambda bi, si: (bi, 0, 0)
    const2 = lambda bi, si: (0, 0)
    tile3 = lambda bi, si: (bi, si, 0)
    full = lambda arr: pl.BlockSpec(arr.shape, const2)
    outs = pl.pallas_call(
        _rwkvproj_kernel,
        grid=(b, s // tm),
        in_specs=[pl.BlockSpec((1, tm, D), tile3), full(g),
                  pl.BlockSpec((1, 1, D), row3), pl.BlockSpec((1, 1, D), row3),
                  full(mu), full(w_r), full(w_k), full(w_v),
                  full(w1), full(w2), full(a1), full(a2), full(g1), full(g2),
                  full(w0), full(a0), full(k_k), full(k_a)],
        out_specs=[pl.BlockSpec((1, tm, D), tile3)] * 7,
        out_shape=[jax.ShapeDtypeStruct((b, s, D), dt) for dt in (BF16, BF16, BF16, F32, F32, F32, BF16)],
        scratch_shapes=[pltpu.VMEM((1, D), F32)],
        compiler_params=pltpu.CompilerParams(
            dimension_semantics=("arbitrary", "arbitrary"), vmem_limit_bytes=VMEM_LIMIT),
        name="rwkv_proj",
    )(x, g, sh, sc, mu, w_r, w_k, w_v, w1, w2, a1, a2, g1, g2, w0, a0, k_k, k_a)
    return outs


def _scan_kernel(r_ref, k_ref, v_ref, kkv_ref, a_ref, lw_ref, gate_ref,
                 rk_ref, lnw_ref, lnb_ref, o_ref, state_ref):
    tc = r_ref.shape[1]
    group_rows = SCAN_GROUP * CHUNK
    n_groups = tc // group_rows

    @pl.when(pl.program_id(2) == 0)
    def _():
        state_ref[...] = jnp.zeros_like(state_ref)

    li = lax.broadcasted_iota(jnp.int32, (PAIR, PAIR), 0)
    lj = lax.broadcasted_iota(jnp.int32, (PAIR, PAIR), 1)
    ones_bd = jnp.where((li >> 6) == (lj >> 6), 1.0, 0.0).astype(BF16)
    strict = (li & (CHUNK - 1)) > (lj & (CHUNK - 1))
    incl = (li & (CHUNK - 1)) >= (lj & (CHUNK - 1))
    eye = jnp.where(li == lj, 1.0, 0.0).astype(F32)
    tri_rows = 4 * CHUNK
    ti = lax.broadcasted_iota(jnp.int32, (tri_rows, tri_rows), 0)
    tj = lax.broadcasted_iota(jnp.int32, (tri_rows, tri_rows), 1)
    tri = jnp.where((ti >= tj) & ((ti >> 6) == (tj >> 6)), 1.0, 0.0).astype(BF16)
    is_h0 = lax.broadcasted_iota(jnp.int32, (1, PAIR), 1) < HEAD
    chunks = range(SCAN_GROUP)

    def stack(z):
        return jnp.concatenate([jnp.where(is_h0, z, 0.0), jnp.where(is_h0, 0.0, z)], axis=0)

    def state_free_part(g, out):
        rows = slice(g * group_rows, (g + 1) * group_rows)
        r = r_ref[0, rows, :].astype(F32)
        k = k_ref[0, rows, :].astype(F32)
        v = v_ref[0, rows, :].astype(F32)
        kkv = kkv_ref[0, rows, :]
        lw = lw_ref[0, rows, :]
        lw1 = lw.astype(BF16)
        lw2 = (lw - lw1.astype(F32)).astype(BF16)
        cl = jnp.concatenate(
            [jnp.dot(tri, lw1[i:i + tri_rows], preferred_element_type=F32)
             + jnp.dot(tri, lw2[i:i + tri_rows], preferred_element_type=F32)
             for i in range(0, group_rows, tri_rows)], axis=0)
        n2 = _mm(kkv * kkv, ones_bd)
        kk = kkv / jnp.maximum(jnp.sqrt(n2), 1e-12)
        bvec = kk * a_ref[0, rows, :]
        egi = jnp.exp(-cl)
        a_t = -kk * jnp.exp(cl - lw)
        b_t = bvec * egi
        k_t = k * egi
        r_t = r * jnp.exp(cl)
        out["bonus"] = _mm(r * k * rk_ref[...], ones_bd) * v
        sls = [slice(c * CHUNK, (c + 1) * CHUNK) for c in chunks]
        cl_end = [cl[(c + 1) * CHUNK - 1:(c + 1) * CHUNK, :] for c in chunks]
        out["decay"] = [jnp.exp(ce) for ce in cl_end]
        tails = [jnp.exp(cl_end[c] - cl[sls[c]]) for c in chunks]
        a_s = [stack(a_t[sl]) for sl in sls]
        r_s = [stack(r_t[sl]) for sl in sls]
        v_s = [stack(v[sl]) for sl in sls]
        bh_s = [stack(bvec[sls[c]] * tails[c]) for c in chunks]
        kh_s = [stack(k[sls[c]] * tails[c]) for c in chunks]
        grams = [_mm_nt(jnp.concatenate([a_s[c], r_s[c]], axis=0),
                        jnp.concatenate([stack(b_t[sls[c]]), stack(k_t[sls[c]])], axis=0))
                 for c in chunks]
        yield
        m_ak = [jnp.where(strict, gm[:PAIR, PAIR:], 0.0) for gm in grams]
        a_r = [jnp.concatenate([jnp.where(incl, gm[PAIR:, :PAIR], 0.0),
                                jnp.where(incl, gm[PAIR:, PAIR:], 0.0)], axis=1) for gm in grams]
        rhs = [jnp.concatenate([a_s[c], _mm(m_ak[c], v_s[c])], axis=1) for c in chunks]
        mps = [jnp.where(strict, gm[:PAIR, :PAIR], 0.0) for gm in grams]
        ts = [eye + mp for mp in mps]
        mps = [_mm(mp, mp) for mp in mps]
        yield
        for level in range(1, 6):
            if level < 5:
                res = [_mm(mps[c], jnp.concatenate([mps[c], ts[c]], axis=1)) for c in chunks]
                mps = [rs[:, :PAIR] for rs in res]
                ts = [ts[c] + res[c][:, PAIR:] for c in chunks]
            else:
                ts = [ts[c] + _mm(mps[c], ts[c]) for c in chunks]
            yield
        xs = [_mm(ts[c], rhs[c]) for c in chunks]
        gys = [_mm(a_r[c], jnp.concatenate(
            [xs[c], jnp.concatenate([jnp.zeros_like(v_s[c]), v_s[c]], axis=1)], axis=0))
               for c in chunks]
        out["g"] = [r_s[c] + gys[c][:, :PAIR] for c in chunks]
        out["y0"] = [gy[:, PAIR:] for gy in gys]
        yield
        pqs = [_mm_tn(xs[c], bh_s[c]) for c in chunks]
        out["pb"] = [pq[:PAIR] for pq in pqs]
        out["q"] = [pqs[c][PAIR:] + _mm_tn(v_s[c], kh_s[c]) for c in chunks]
        yield

    carried_state = [state_ref[...]]

    def state_part(g, res):
        ys = []
        for c in chunks:
            state = carried_state[0]
            y_s = _mm_nt(res["g"][c], state) + res["y0"][c]
            ys.append(y_s[:CHUNK] + y_s[CHUNK:])
            carried_state[0] = state * res["decay"][c] + _mm(state, res["pb"][c]) + res["q"][c]
            yield
        rows = slice(g * group_rows, (g + 1) * group_rows)
        y = jnp.concatenate(ys, axis=0)
        inv_n = 1.0 / HEAD
        mean = _mm(y, ones_bd) * inv_n
        yc = y - mean
        var = _mm(yc * yc, ones_bd) * inv_n
        yn = yc * lax.rsqrt(var + GN_EPS) * lnw_ref[...] + lnb_ref[...]
        o_ref[0, rows, :] = ((yn + res["bonus"]) * gate_ref[0, rows, :].astype(F32)
                             ).astype(o_ref.dtype)
        yield

    results = [dict() for _ in range(n_groups)]
    for _ in state_free_part(0, results[0]):
        pass
    for g in range(n_groups):
        carried = state_part(g, results[g])
        ahead = state_free_part(g + 1, results[g + 1]) if g + 1 < n_groups else iter(())
        n_carried, n_ahead = SCAN_GROUP + 1, 9
        for i, _ in enumerate(ahead):
            if (i * n_carried) // n_ahead != ((i + 1) * n_carried) // n_ahead:
                next(carried, None)
        for _ in carried:
            pass
    state_ref[...] = carried_state[0]


def _rwkv_scan(r, k, v, kkv, a, lw, gate, r_k, ln_w, ln_b):
    b, s, _ = r.shape
    tc = SCAN_BLOCK
    tile = pl.BlockSpec((1, tc, PAIR), lambda bi, p, ti: (bi, ti, p))
    vec = pl.BlockSpec((1, PAIR), lambda bi, p, ti: (0, p))
    return pl.pallas_call(
        _scan_kernel,
        grid=(b, D // PAIR, s // tc),
        in_specs=[tile] * 7 + [vec] * 3,
        out_specs=tile,
        out_shape=jax.ShapeDtypeStruct((b, s, D), BF16),
        scratch_shapes=[pltpu.VMEM((PAIR, PAIR), F32)],
        compiler_params=pltpu.CompilerParams(
            dimension_semantics=("arbitrary", "arbitrary", "arbitrary"),
            vmem_limit_bytes=VMEM_LIMIT),
        name="rwkv_scan",
    )(r, k, v, kkv, a, lw, gate, r_k, ln_w, ln_b)


def kernel(x, c, attn_w_in, attn_b_f, attn_w_out, rwkv_mu, rwkv_w_r, rwkv_w_k, rwkv_w_v, rwkv_w_o, rwkv_w0, rwkv_w1, rwkv_w2, rwkv_a0, rwkv_a1, rwkv_a2, rwkv_g1, rwkv_g2, rwkv_k_k, rwkv_k_a, rwkv_r_k, rwkv_ln_w, rwkv_ln_b, mod_w, mod_b, norm_mix, norm_ffn, ffn_w_in, ffn_w_out, norm_final):
    b, s, _ = x.shape
    mod = _modulation(c, mod_w, mod_b)
    mods = [[mod[l, :, i * D:(i + 1) * D].reshape(b, 1, D) for i in range(6)] for l in range(2)]
    row = lambda t: t.reshape(1, D)
    g_final = row(norm_final)

    sh_m, sc_m, gt_m, sh_f, sc_f, gt_f = mods[0]
    w_in = attn_w_in[0]
    qscale = HEAD ** -0.5 * LOG2E
    col_scale = jnp.ones((3 * D,), F32)
    col_scale = col_scale.at[0:FOX_W].set(qscale).at[3 * FOX_W:4 * FOX_W].set(qscale)
    w_main = (w_in[:, :3 * D] * col_scale[None, :]).astype(BF16)
    w_f = jnp.zeros((D, AUG), F32).at[:, :8].set(w_in[:, 3 * D:]).astype(BF16)
    b_f = jnp.zeros((1, AUG), F32).at[0, :8].set(attn_b_f[0])
    qa, ka, vt = _inproj(x, row(norm_mix[0]), sh_m, sc_m, w_main, w_f, b_f)
    o = _attention(qa, ka, vt)

    w_ffn_in = ffn_w_in.astype(BF16)
    w_ffn_out = ffn_w_out.astype(BF16)
    x = _post(x, o, attn_w_out[0].astype(BF16), gt_m, row(norm_ffn[0]), sh_f, sc_f, gt_f,
              w_ffn_in, w_ffn_out, 0, g_final, final_norm=False)

    sh_m, sc_m, gt_m, sh_f, sc_f, gt_f = mods[1]
    pad_in = lambda w, n: jnp.zeros((D, n), BF16).at[:, :w.shape[1]].set(w.astype(BF16))
    pad_out = lambda w, n: jnp.zeros((n, D), BF16).at[:w.shape[0], :].set(w.astype(BF16))
    r, k, v, kkv, a, lw, gate = _rwkv_proj(
        x, row(norm_mix[1]), sh_m, sc_m, rwkv_mu[0],
        rwkv_w_r[0].astype(BF16), rwkv_w_k[0].astype(BF16), rwkv_w_v[0].astype(BF16),
        pad_in(rwkv_w1[0], 128), pad_out(rwkv_w2[0], 128),
        pad_in(rwkv_a1[0], 128), pad_out(rwkv_a2[0], 128),
        pad_in(rwkv_g1[0], 256), pad_out(rwkv_g2[0], 256),
        row(rwkv_w0[0]), row(rwkv_a0[0]), row(rwkv_k_k[0]), row(rwkv_k_a[0]))
    y = _rwkv_scan(r, k, v, kkv, a, lw, gate, row(rwkv_r_k[0]), row(rwkv_ln_w[0]),
                   row(rwkv_ln_b[0]))
    x = _post(x, y, rwkv_w_o[0].astype(BF16), gt_m, row(norm_ffn[1]), sh_f, sc_f, gt_f,
              w_ffn_in, w_ffn_out, 1, g_final, final_norm=True)
    return x
```

```python
import functools
import math

import jax
import jax.numpy as jnp
import numpy as np
from jax import lax
from jax.experimental import pallas as pl
from jax.experimental.pallas import tpu as pltpu

F32 = jnp.float32
BF16 = jnp.bfloat16

D = 1024
HEAD = 64
PAIR = 2 * HEAD
N_PAIR_ATTN = 8
FOX_W = 512
D_FF = 2816
MXU_WIDTH = 256
FF_SLAB_EDGES = (0, 5 * MXU_WIDTH, D_FF)
MOBA_BLOCK = 256
N_MOBA_BLOCKS = 16
MOBA_TOPK = 3
AUG = 128
AUG_PER_HEAD = 16
NORM_EPS = 1e-6
GN_EPS = HEAD * 1e-5
LOG2E = 1.4426950408889634
NEG = -1e30
PEN = -30000.0
CHUNK = 64
SCAN_GROUP = 8
SCAN_BLOCK = 8 * SCAN_GROUP * CHUNK
ATTN_TQ = 256
RWKV_PROJ_COLS = 256
VMEM_LIMIT = 56 * 1024 * 1024


def _mm(a, b):
    return jnp.dot(a.astype(BF16), b.astype(BF16), preferred_element_type=F32)


def _mm_nt(a, b):
    return lax.dot_general(a.astype(BF16), b.astype(BF16), (((1,), (1,)), ((), ())),
                           preferred_element_type=F32)


def _mm_tn(a, b):
    return lax.dot_general(a.astype(BF16), b.astype(BF16), (((0,), (0,)), ((), ())),
                           preferred_element_type=F32)


def _sigmoid(x):
    return 1.0 / (1.0 + jnp.exp(-x))


def _softplus(x):
    return jnp.maximum(x, 0.0) + jnp.log(1.0 + jnp.exp(-jnp.abs(x)))


def _norm_mod(x, g, sh, sc):
    y = x * lax.rsqrt(jnp.mean(x * x, axis=-1, keepdims=True) + NORM_EPS)
    return (y * g) * (1.0 + sc) + sh


def _mod_kernel(c_ref, w_ref, b_ref, o_ref):
    c = c_ref[...]
    cond = c * _sigmoid(c)
    o_ref[0] = jnp.dot(cond, w_ref[0], precision=lax.Precision.HIGHEST,
                       preferred_element_type=F32) + b_ref[0]


def _modulation(c, mod_w, mod_b):
    depth, _, n = mod_w.shape
    b = c.shape[0]
    rows = 8
    c_pad = jnp.zeros((rows, D), F32).at[:b].set(c)
    tn = 1536
    out = pl.pallas_call(
        _mod_kernel,
        grid=(depth, n // tn),
        in_specs=[pl.BlockSpec((rows, D), lambda l, j: (0, 0)),
                  pl.BlockSpec((1, D, tn), lambda l, j: (l, 0, j)),
                  pl.BlockSpec((1, 1, tn), lambda l, j: (l, 0, j))],
        out_specs=pl.BlockSpec((1, rows, tn), lambda l, j: (l, 0, j)),
        out_shape=jax.ShapeDtypeStruct((depth, rows, n), F32),
        compiler_params=pltpu.CompilerParams(
            dimension_semantics=("arbitrary", "arbitrary"), vmem_limit_bytes=VMEM_LIMIT),
        name="mod",
    )(c_pad, mod_w, mod_b.reshape(depth, 1, n))
    return out[:, :b]


def _split3(z):
    z1 = z.astype(BF16)
    r1 = z - z1.astype(F32)
    z2 = r1.astype(BF16)
    return z1, z2, (r1 - z2.astype(F32)).astype(BF16)


def _mm_split3_lhs_exact(a_exact, b):
    aa = a_exact.astype(BF16)
    b1, b2, b3 = _split3(b)
    return (jnp.dot(aa, b1, preferred_element_type=F32)
            + jnp.dot(aa, b2, preferred_element_type=F32)
            + jnp.dot(aa, b3, preferred_element_type=F32))


def _moba_penalty_steps(gate, own, out):
    lane = lax.broadcasted_iota(jnp.int32, gate.shape, 1)
    n = lane & (N_MOBA_BLOCKS - 1)
    past = n < own
    g = jnp.where(past, gate, -jnp.inf)
    rank = jnp.zeros(gate.shape, jnp.int32)
    for k in range(1, N_MOBA_BLOCKS):
        gk = jnp.where(n >= k, pltpu.roll(g, k, 1), pltpu.roll(g, AUG - N_MOBA_BLOCKS + k, 1))
        beats = (gk > g) | ((gk == g) & (n >= k))
        rank = rank + beats.astype(jnp.int32)
        yield
    keep = (past & (rank < MOBA_TOPK)) | (n == own)
    out[0] = jnp.where(keep, 0.0, PEN)


def _inproj_kernel(x_ref, g_ref, sh_ref, sc_ref, w_ref, wf_ref, bf_ref, sel_c_ref, const_c_ref,
                   sel_p_ref, qa_ref, ka_ref, vt_ref, carry_ref, km_ref):
    tm = x_ref.shape[1]
    si = pl.program_id(1)

    @pl.when(si == 0)
    def _():
        carry_ref[...] = jnp.zeros_like(carry_ref)
        km_ref[...] = jnp.zeros_like(km_ref)

    h = _norm_mod(x_ref[0], g_ref[...], sh_ref[0], sc_ref[0]).astype(BF16)
    width = PAIR + AUG

    def proj(lo, hi):
        return jnp.dot(h, w_ref[:, lo:hi], preferred_element_type=F32)

    def put(ref, pair, value):
        ref[0, :, pair * width:pair * width + PAIR] = value.astype(BF16)

    def put_bias(ref, pair, value):
        ref[0, :, pair * width + PAIR:(pair + 1) * width] = value.astype(BF16)

    q_mb = proj(3 * FOX_W, 4 * FOX_W)
    qb = q_mb.astype(BF16)
    km1, km2, km3 = _split3(km_ref[...])
    gate = (jnp.dot(qb, km1, preferred_element_type=F32)
            + jnp.dot(qb, km2, preferred_element_type=F32)
            + jnp.dot(qb, km3, preferred_element_type=F32))
    pen = [None]
    ranking = _moba_penalty_steps(gate, si, pen)

    def fox_bias():
        f = jnp.dot(h, wf_ref[...], preferred_element_type=F32) + bf_ref[...]
        lf = -_softplus(-f)
        row = lax.broadcasted_iota(jnp.int32, (tm, tm), 0)
        col = lax.broadcasted_iota(jnp.int32, (tm, tm), 1)
        tri = jnp.where(row >= col, 1.0, 0.0).astype(F32)
        cum = _mm_split3_lhs_exact(tri, lf) + carry_ref[...]
        carry_ref[...] = cum[tm - 1:tm, :]
        c3 = jnp.concatenate(_split3(cum * LOG2E), axis=1)
        aug = jnp.dot(c3, sel_c_ref[...], preferred_element_type=F32) + const_c_ref[...]
        for p in range(4):
            put_bias(qa_ref, p, aug[:, p * PAIR:(p + 1) * PAIR])
            put_bias(ka_ref, p, aug[:, FOX_W + p * PAIR:FOX_W + (p + 1) * PAIR])

    def fox_qk(half):
        q = proj(half * 2 * PAIR, (half + 1) * 2 * PAIR)
        k = proj(FOX_W + half * 2 * PAIR, FOX_W + (half + 1) * 2 * PAIR)
        for i in range(2):
            put(qa_ref, 2 * half + i, q[:, i * PAIR:(i + 1) * PAIR])
            put(ka_ref, 2 * half + i, k[:, i * PAIR:(i + 1) * PAIR])

    def values(group, half):
        lo = (2 + 3 * group) * FOX_W + half * 2 * PAIR
        v = proj(lo, lo + 2 * PAIR)
        first = 4 * group + 2 * half
        vt_ref[0, first:first + 2, 0] = v.T.reshape(2, PAIR, tm).astype(BF16)

    def moba_k():
        k_mb = proj(4 * FOX_W, 5 * FOX_W)
        lane_p = lax.broadcasted_iota(jnp.int32, (tm, AUG), 1)
        onehot = jnp.where((lane_p < 2 * AUG_PER_HEAD) & ((lane_p & (N_MOBA_BLOCKS - 1)) == si),
                           1.0, 0.0)
        for p in range(4):
            put(ka_ref, 4 + p, k_mb[:, p * PAIR:(p + 1) * PAIR])
            put_bias(ka_ref, 4 + p, onehot)
        kb_t = k_mb.astype(BF16).astype(F32).T
        place = jnp.where((lane_p & (N_MOBA_BLOCKS - 1)) == si, 1.0 / tm, 0.0)
        upd = jnp.dot(kb_t.astype(BF16), place.astype(BF16), preferred_element_type=F32)
        d_head = lax.broadcasted_iota(jnp.int32, (FOX_W, AUG), 0) >> 6
        l_head = lax.broadcasted_iota(jnp.int32, (FOX_W, AUG), 1) >> 4
        km_ref[...] += jnp.where(d_head == l_head, upd, 0.0)

    others = [fox_bias, lambda: fox_qk(0), lambda: fox_qk(1), moba_k,
              lambda: values(0, 0), lambda: values(0, 1), lambda: values(1, 0),
              lambda: values(1, 1)]
    for i, _ in enumerate(ranking):
        if i % 2 == 0 and others:
            others.pop(0)()
    for task in others:
        task()
    aug_mb_q = jnp.dot(pen[0].astype(BF16), sel_p_ref[...], preferred_element_type=F32)
    for p in range(4):
        put(qa_ref, 4 + p, q_mb[:, p * PAIR:(p + 1) * PAIR])
        put_bias(qa_ref, 4 + p, aug_mb_q[:, p * PAIR:(p + 1) * PAIR])


def _bias_lane_selectors():
    sel_c = np.zeros((3 * AUG, 2 * FOX_W), np.float32)
    const_c = np.zeros((1, 2 * FOX_W), np.float32)
    sel_p = np.zeros((AUG, FOX_W), np.float32)
    for head in range(8):
        base = (head // 2) * AUG + (head % 2) * AUG_PER_HEAD
        for i in range(3):
            sel_c[i * AUG + head, base + i] = 1.0
            sel_c[i * AUG + head, FOX_W + base + 3 + i] = -1.0
            const_c[0, base + 3 + i] = 1.0
            const_c[0, FOX_W + base + i] = 1.0
        for n in range(N_MOBA_BLOCKS):
            sel_p[head * N_MOBA_BLOCKS + n, base + n] = 1.0
    return jnp.asarray(sel_c, BF16), jnp.asarray(const_c, F32), jnp.asarray(sel_p, BF16)


def _inproj(x, g, sh, sc, w_main, w_f, b_f):
    b, s, _ = x.shape
    tm = MOBA_BLOCK
    n = w_main.shape[1]
    width = N_PAIR_ATTN * (PAIR + AUG)
    sel_c, const_c, sel_p = _bias_lane_selectors()
    row3 = lambda bi, si: (bi, 0, 0)
    const2 = lambda bi, si: (0, 0)
    tile3 = lambda bi, si: (bi, si, 0)
    full = lambda arr: pl.BlockSpec(arr.shape, const2)
    return pl.pallas_call(
        _inproj_kernel,
        grid=(b, s // tm),
        in_specs=[pl.BlockSpec((1, tm, D), tile3),
                  pl.BlockSpec((1, D), const2),
                  pl.BlockSpec((1, 1, D), row3),
                  pl.BlockSpec((1, 1, D), row3),
                  pl.BlockSpec((D, n), const2),
                  pl.BlockSpec((D, AUG), const2),
                  pl.BlockSpec((1, AUG), const2),
                  full(sel_c), full(const_c), full(sel_p)],
        out_specs=[pl.BlockSpec((1, tm, width), tile3),
                   pl.BlockSpec((1, tm, width), tile3),
                   pl.BlockSpec((1, N_PAIR_ATTN, 1, PAIR, tm), lambda bi, si: (bi, 0, si, 0, 0))],
        out_shape=[jax.ShapeDtypeStruct((b, s, width), BF16),
                   jax.ShapeDtypeStruct((b, s, width), BF16),
                   jax.ShapeDtypeStruct((b, N_PAIR_ATTN, s // tm, PAIR, tm), BF16)],
        scratch_shapes=[pltpu.VMEM((1, AUG), F32), pltpu.VMEM((FOX_W, AUG), F32)],
        compiler_params=pltpu.CompilerParams(
            dimension_semantics=("arbitrary", "arbitrary"), vmem_limit_bytes=VMEM_LIMIT),
        name="inproj",
    )(x, g, sh, sc, w_main, w_f, b_f, sel_c, const_c, sel_p)


def _attn_kernel(qa_ref, ka_ref, vt_ref, o_ref, s_even, s_odd, macc_ref, mprev_ref, lacc_ref,
                 acc_ref):
    tq = qa_ref.shape[1]
    tk = vt_ref.shape[4]
    unit = tq // tk
    nq = s_even.shape[1] // tq
    j = pl.program_id(2)
    qa = qa_ref[0]
    lane = lax.broadcasted_iota(jnp.int32, (1, PAIR + AUG), 1)
    head0 = (lane < HEAD) | ((lane >= PAIR) & (lane < PAIR + AUG_PER_HEAD))
    head1 = ((lane >= HEAD) & (lane < PAIR)) | (
        (lane >= PAIR + AUG_PER_HEAD) & (lane < PAIR + 2 * AUG_PER_HEAD))
    m_prev = mprev_ref[...]
    macc_ref[...] = jnp.full(macc_ref.shape, NEG, F32)
    lacc_ref[...] = jnp.zeros_like(lacc_ref)
    acc_ref[...] = jnp.zeros_like(acc_ref)

    def score_tiles(s_ref, first_tile, n, diagonal):
        rows = n * tk
        start = pl.multiple_of(first_tile * tk, tk)
        k = ka_ref[0, pl.ds(start, rows), :]
        zero = jnp.zeros_like(k)
        k2 = jnp.concatenate([jnp.where(head0, k, zero), jnp.where(head1, k, zero)], axis=0)
        st = lax.dot_general(k2, qa, (((1,), (1,)), ((), ())), preferred_element_type=F32)
        full = rows - tq if diagonal else rows
        for h in range(2):
            macc = macc_ref[h]
            if full:
                part = st[h * rows:h * rows + full]
                s_ref[h, pl.ds(start, full), :] = part
                macc = jnp.maximum(macc, jnp.max(part.reshape(full // 8, 8, tq), axis=0))
            if diagonal:
                key_pos = lax.broadcasted_iota(jnp.int32, (tq, tq), 0)
                qry_pos = lax.broadcasted_iota(jnp.int32, (tq, tq), 1)
                part = jnp.where(key_pos <= qry_pos, st[h * rows + full:(h + 1) * rows], NEG)
                s_ref[h, pl.ds(start + full, tq), :] = part
                macc = jnp.maximum(macc, jnp.max(part.reshape(tq // 8, 8, tq), axis=0))
            macc_ref[h] = macc

    def pv_tile(s_ref, t):
        start = pl.multiple_of(t * tk, tk)
        vt = vt_ref[0, 0, t]
        ls, accs = [], []
        for h in range(2):
            p = jnp.exp2(s_ref[h, pl.ds(start, tk), :] - m_prev[h])
            ls.append(jnp.sum(p.reshape(tk // 8, 8, tq), axis=0))
            accs.append(jnp.dot(vt[h * HEAD:(h + 1) * HEAD, :], p.astype(BF16),
                                preferred_element_type=F32))
        return jnp.stack(ls), jnp.concatenate(accs, axis=0)

    def pv_tiles(s_ref, first_tile, n):
        parts = [pv_tile(s_ref, first_tile + t) for t in range(n)]
        while len(parts) > 1:
            parts = [(a[0] + b[0], a[1] + b[1]) for a, b in zip(parts[0::2], parts[1::2])]
        lacc_ref[...] += parts[0][0]
        acc_ref[...] += parts[0][1]

    def step(s_cur, s_prev):
        bit = nq // 2
        while bit >= 1:
            in_group = ((j & bit) != 0) & (j < nq)
            lowest = (j & (bit - 1)) == 0
            first = (j & (-2 * bit)) * unit
            n = bit * unit

            if bit > 1:
                @pl.when(in_group & jnp.logical_not(lowest))
                def _(n=n, first=first):
                    score_tiles(s_cur, first, n, False)
                    pv_tiles(s_prev, first, n)

            @pl.when(in_group & lowest)
            def _(n=n, first=first):
                score_tiles(s_cur, first, n + unit, True)
                pv_tiles(s_prev, first, n)
            bit //= 2

        @pl.when(j == 0)
        def _():
            score_tiles(s_cur, 0, unit, True)

        @pl.when(j == nq)
        def _():
            for first in range(0, nq * unit, 8):
                pv_tiles(s_prev, first, 8)

    @pl.when((j & 1) == 0)
    def _():
        step(s_even, s_odd)

    @pl.when((j & 1) == 1)
    def _():
        step(s_odd, s_even)

    mprev_ref[...] = jnp.max(macc_ref[...], axis=1, keepdims=True)

    @pl.when(j > 0)
    def _():
        l = jnp.sum(lacc_ref[...], axis=1, keepdims=True)
        acc = acc_ref[...]
        ot = jnp.concatenate([acc[:HEAD] / l[0], acc[HEAD:] / l[1]], axis=0)
        o_ref[0] = ot.T.astype(o_ref.dtype)


def _attention(qa, ka, vt):
    b, s, _ = qa.shape
    tq = ATTN_TQ
    nk, tk = vt.shape[2], vt.shape[4]
    nq = s // tq
    assert nq & (nq - 1) == 0 and (nq * tq // tk) % 8 == 0, "query tiles must be a power of two"
    return pl.pallas_call(
        _attn_kernel,
        grid=(b, N_PAIR_ATTN, nq + 1),
        in_specs=[pl.BlockSpec((1, tq, PAIR + AUG), lambda bi, p, j: (bi, jnp.minimum(j, nq - 1), p)),
                  pl.BlockSpec((1, s, PAIR + AUG), lambda bi, p, j: (bi, 0, p)),
                  pl.BlockSpec((1, 1, nk, PAIR, tk), lambda bi, p, j: (bi, p, 0, 0, 0))],
        out_specs=pl.BlockSpec((1, tq, PAIR), lambda bi, p, j: (bi, jnp.maximum(j - 1, 0), p)),
        out_shape=jax.ShapeDtypeStruct((b, s, N_PAIR_ATTN * PAIR), BF16),
        scratch_shapes=[pltpu.VMEM((2, s, tq), F32),
                        pltpu.VMEM((2, s, tq), F32),
                        pltpu.VMEM((2, 8, tq), F32),
                        pltpu.VMEM((2, 1, tq), F32),
                        pltpu.VMEM((2, 8, tq), F32),
                        pltpu.VMEM((PAIR, tq), F32)],
        compiler_params=pltpu.CompilerParams(
            dimension_semantics=("arbitrary", "arbitrary", "arbitrary"),
            vmem_limit_bytes=VMEM_LIMIT),
        name="attn",
    )(qa, ka, vt)


def _post_kernel(x_ref, a_ref, wo_ref, gtm_ref, g_ref, sh_ref, sc_ref, gtf_ref,
                 win_ref, wd_ref, gfin_ref, o_ref, *, final_norm):
    x1 = x_ref[0] + gtm_ref[0] * jnp.dot(a_ref[0], wo_ref[...], preferred_element_type=F32)
    h = _norm_mod(x1, g_ref[...], sh_ref[0], sc_ref[0]).astype(BF16)
    ff = jnp.zeros_like(x1)
    for lo, hi in zip(FF_SLAB_EDGES[:-1], FF_SLAB_EDGES[1:]):
        gate = jnp.dot(h, win_ref[0, :, lo:hi], preferred_element_type=F32)
        up = jnp.dot(h, win_ref[0, :, D_FF + lo:D_FF + hi], preferred_element_type=F32)
        act = (gate * _sigmoid(gate) * up).astype(BF16)
        ff = ff + jnp.dot(act, wd_ref[0, lo:hi, :], preferred_element_type=F32)
    x2 = x1 + gtf_ref[0] * ff
    if final_norm:
        x2 = x2 * lax.rsqrt(jnp.mean(x2 * x2, axis=-1, keepdims=True) + NORM_EPS) * gfin_ref[...]
    o_ref[0] = x2


def _post(x, a, w_o, gt_m, g_ffn, sh_f, sc_f, gt_f, w_in, w_down, layer, g_final, final_norm):
    b, s, _ = x.shape
    tm = 512
    row3 = lambda bi, si: (bi, 0, 0)
    const2 = lambda bi, si: (0, 0)
    tile3 = lambda bi, si: (bi, si, 0)
    of_layer = lambda bi, si: (layer, 0, 0)
    once = pl.Buffered(1)
    kern = functools.partial(_post_kernel, final_norm=final_norm)
    return pl.pallas_call(
        kern,
        grid=(b, s // tm),
        in_specs=[pl.BlockSpec((1, tm, D), tile3),
                  pl.BlockSpec((1, tm, D), tile3),
                  pl.BlockSpec((D, D), const2),
                  pl.BlockSpec((1, 1, D), row3),
                  pl.BlockSpec((1, D), const2),
                  pl.BlockSpec((1, 1, D), row3),
                  pl.BlockSpec((1, 1, D), row3),
                  pl.BlockSpec((1, 1, D), row3),
                  pl.BlockSpec((1, D, 2 * D_FF), of_layer, pipeline_mode=once),
                  pl.BlockSpec((1, D_FF, D), of_layer, pipeline_mode=once),
                  pl.BlockSpec((1, D), const2)],
        out_specs=pl.BlockSpec((1, tm, D), tile3),
        out_shape=jax.ShapeDtypeStruct((b, s, D), F32),
        compiler_params=pltpu.CompilerParams(
            dimension_semantics=("arbitrary", "arbitrary"), vmem_limit_bytes=VMEM_LIMIT),
        name="post_final" if final_norm else "post",
    )(x, a, w_o, gt_m, g_ffn, sh_f, sc_f, gt_f, w_in, w_down, g_final)


def _rwkvproj_kernel(x_ref, g_ref, sh_ref, sc_ref, mu_ref, wr_ref, wk_ref, wv_ref,
                     w1_ref, w2_ref, a1_ref, a2_ref, g1_ref, g2_ref,
                     w0_ref, a0_ref, kk_ref, ka_ref,
                     r_ref, k_ref, v_ref, kkv_ref, a_ref, lw_ref, gate_ref, carry_ref):
    tm = x_ref.shape[1]

    @pl.when(pl.program_id(1) == 0)
    def _():
        carry_ref[...] = jnp.zeros_like(carry_ref)

    h = _norm_mod(x_ref[0], g_ref[...], sh_ref[0], sc_ref[0])
    rowi = lax.broadcasted_iota(jnp.int32, (tm, 1), 0)
    h_prev = jnp.where(rowi == 0, carry_ref[...], pltpu.roll(h, 1, 0))
    carry_ref[...] = h[tm - 1:tm, :]
    xx = h_prev - h
    mix = lambda i: (h + xx * mu_ref[i:i + 1, :]).astype(BF16)
    xr, xk, xv = mix(0), mix(2), mix(3)
    hid_w = jnp.tanh(_mm(mix(1), w1_ref[...])).astype(BF16)
    hid_a = _mm(mix(4), a1_ref[...]).astype(BF16)
    hid_g = _sigmoid(_mm(mix(5), g1_ref[...])).astype(BF16)
    cw = RWKV_PROJ_COLS
    for c in range(D // cw):
        cols = slice(c * cw, (c + 1) * cw)
        k = jnp.dot(xk, wk_ref[:, cols], preferred_element_type=F32)
        a = _sigmoid(a0_ref[:, cols] + jnp.dot(hid_a, a2_ref[:, cols], preferred_element_type=F32))
        wl = w0_ref[:, cols] + jnp.dot(hid_w, w2_ref[:, cols], preferred_element_type=F32)
        k_ref[0, :, cols] = (k * (1.0 + (a - 1.0) * ka_ref[:, cols])).astype(k_ref.dtype)
        kkv_ref[0, :, cols] = k * kk_ref[:, cols]
        a_ref[0, :, cols] = a
        lw_ref[0, :, cols] = -math.exp(-0.5) * _sigmoid(wl)
        r_ref[0, :, cols] = jnp.dot(xr, wr_ref[:, cols],
                                    preferred_element_type=F32).astype(r_ref.dtype)
        v_ref[0, :, cols] = jnp.dot(xv, wv_ref[:, cols],
                                    preferred_element_type=F32).astype(v_ref.dtype)
        gate_ref[0, :, cols] = jnp.dot(hid_g, g2_ref[:, cols],
                                       preferred_element_type=F32).astype(gate_ref.dtype)


def _rwkv_proj(x, g, sh, sc, mu, w_r, w_k, w_v, w1, w2, a1, a2, g1, g2, w0, a0, k_k, k_a):
    b, s, _ = x.shape
    tm = 512
    row3 = lambda bi, si: (bi, 0, 0)
    const2 = lambda bi, si: (0, 0)
    tile3 = lambda bi, si: (bi, si, 0)
    full = lambda arr: pl.BlockSpec(arr.shape, const2)
    outs = pl.pallas_call(
        _rwkvproj_kernel,
        grid=(b, s // tm),
        in_specs=[pl.BlockSpec((1, tm, D), tile3), full(g),
                  pl.BlockSpec((1, 1, D), row3), pl.BlockSpec((1, 1, D), row3),
                  full(mu), full(w_r), full(w_k), full(w_v),
                  full(w1), full(w2), full(a1), full(a2), full(g1), full(g2),
                  full(w0), full(a0), full(k_k), full(k_a)],
        out_specs=[pl.BlockSpec((1, tm, D), tile3)] * 7,
        out_shape=[jax.ShapeDtypeStruct((b, s, D), dt) for dt in (BF16, BF16, BF16, F32, F32, F32, BF16)],
        scratch_shapes=[pltpu.VMEM((1, D), F32)],
        compiler_params=pltpu.CompilerParams(
            dimension_semantics=("arbitrary", "arbitrary"), vmem_limit_bytes=VMEM_LIMIT),
        name="rwkv_proj",
    )(x, g, sh, sc, mu, w_r, w_k, w_v, w1, w2, a1, a2, g1, g2, w0, a0, k_k, k_a)
    return outs


def _scan_kernel(r_ref, k_ref, v_ref, kkv_ref, a_ref, lw_ref, gate_ref,
                 rk_ref, lnw_ref, lnb_ref, o_ref, state_ref):
    tc = r_ref.shape[1]
    group_rows = SCAN_GROUP * CHUNK
    n_groups = tc // group_rows

    @pl.when(pl.program_id(2) == 0)
    def _():
        state_ref[...] = jnp.zeros_like(state_ref)

    li = lax.broadcasted_iota(jnp.int32, (PAIR, PAIR), 0)
    lj = lax.broadcasted_iota(jnp.int32, (PAIR, PAIR), 1)
    ones_bd = jnp.where((li >> 6) == (lj >> 6), 1.0, 0.0).astype(BF16)
    strict = (li & (CHUNK - 1)) > (lj & (CHUNK - 1))
    incl = (li & (CHUNK - 1)) >= (lj & (CHUNK - 1))
    eye = jnp.where(li == lj, 1.0, 0.0).astype(F32)
    tri_rows = 4 * CHUNK
    ti = lax.broadcasted_iota(jnp.int32, (tri_rows, tri_rows), 0)
    tj = lax.broadcasted_iota(jnp.int32, (tri_rows, tri_rows), 1)
    tri = jnp.where((ti >= tj) & ((ti >> 6) == (tj >> 6)), 1.0, 0.0).astype(BF16)
    is_h0 = lax.broadcasted_iota(jnp.int32, (1, PAIR), 1) < HEAD
    chunks = range(SCAN_GROUP)

    def stack(z):
        return jnp.concatenate([jnp.where(is_h0, z, 0.0), jnp.where(is_h0, 0.0, z)], axis=0)

    def state_free_part(g, out):
        rows = slice(g * group_rows, (g + 1) * group_rows)
        r = r_ref[0, rows, :].astype(F32)
        k = k_ref[0, rows, :].astype(F32)
        v = v_ref[0, rows, :].astype(F32)
        kkv = kkv_ref[0, rows, :]
        lw = lw_ref[0, rows, :]
        lw1 = lw.astype(BF16)
        lw2 = (lw - lw1.astype(F32)).astype(BF16)
        cl = jnp.concatenate(
            [jnp.dot(tri, lw1[i:i + tri_rows], preferred_element_type=F32)
             + jnp.dot(tri, lw2[i:i + tri_rows], preferred_element_type=F32)
             for i in range(0, group_rows, tri_rows)], axis=0)
        n2 = _mm(kkv * kkv, ones_bd)
        kk = kkv / jnp.maximum(jnp.sqrt(n2), 1e-12)
        bvec = kk * a_ref[0, rows, :]
        egi = jnp.exp(-cl)
        a_t = -kk * jnp.exp(cl - lw)
        b_t = bvec * egi
        k_t = k * egi
        r_t = r * jnp.exp(cl)
        out["bonus"] = _mm(r * k * rk_ref[...], ones_bd) * v
        sls = [slice(c * CHUNK, (c + 1) * CHUNK) for c in chunks]
        cl_end = [cl[(c + 1) * CHUNK - 1:(c + 1) * CHUNK, :] for c in chunks]
        out["decay"] = [jnp.exp(ce) for ce in cl_end]
        tails = [jnp.exp(cl_end[c] - cl[sls[c]]) for c in chunks]
        a_s = [stack(a_t[sl]) for sl in sls]
        r_s = [stack(r_t[sl]) for sl in sls]
        v_s = [stack(v[sl]) for sl in sls]
        bh_s = [stack(bvec[sls[c]] * tails[c]) for c in chunks]
        kh_s = [stack(k[sls[c]] * tails[c]) for c in chunks]
        grams = [_mm_nt(jnp.concatenate([a_s[c], r_s[c]], axis=0),
                        jnp.concatenate([stack(b_t[sls[c]]), stack(k_t[sls[c]])], axis=0))
                 for c in chunks]
        yield
        m_ak = [jnp.where(strict, gm[:PAIR, PAIR:], 0.0) for gm in grams]
        a_r = [jnp.concatenate([jnp.where(incl, gm[PAIR:, :PAIR], 0.0),
                                jnp.where(incl, gm[PAIR:, PAIR:], 0.0)], axis=1) for gm in grams]
        rhs = [jnp.concatenate([a_s[c], _mm(m_ak[c], v_s[c])], axis=1) for c in chunks]
        mps = [jnp.where(strict, gm[:PAIR, :PAIR], 0.0) for gm in grams]
        ts = [eye + mp for mp in mps]
        mps = [_mm(mp, mp) for mp in mps]
        yield
        for level in range(1, 6):
            if level < 5:
                res = [_mm(mps[c], jnp.concatenate([mps[c], ts[c]], axis=1)) for c in chunks]
                mps = [rs[:, :PAIR] for rs in res]
                ts = [ts[c] + res[c][:, PAIR:] for c in chunks]
            else:
                ts = [ts[c] + _mm(mps[c], ts[c]) for c in chunks]
            yield
        xs = [_mm(ts[c], rhs[c]) for c in chunks]
        gys = [_mm(a_r[c], jnp.concatenate(
            [xs[c], jnp.concatenate([jnp.zeros_like(v_s[c]), v_s[c]], axis=1)], axis=0))
               for c in chunks]
        out["g"] = [r_s[c] + gys[c][:, :PAIR] for c in chunks]
        out["y0"] = [gy[:, PAIR:] for gy in gys]
        yield
        out["pb"] = [_mm_tn(xs[c][:, :PAIR], bh_s[c]) for c in chunks]
        out["q"] = [_mm_tn(jnp.concatenate([xs[c][:, PAIR:], v_s[c]], axis=0),
                           jnp.concatenate([bh_s[c], kh_s[c]], axis=0)) for c in chunks]
        yield

    carried_state = [state_ref[...]]

    def state_part(g, res):
        ys = []
        for c in chunks:
            state = carried_state[0]
            y_s = _mm_nt(res["g"][c], state) + res["y0"][c]
            ys.append(y_s[:CHUNK] + y_s[CHUNK:])
            carried_state[0] = state * res["decay"][c] + _mm(state, res["pb"][c]) + res["q"][c]
            yield
        rows = slice(g * group_rows, (g + 1) * group_rows)
        y = jnp.concatenate(ys, axis=0)
        inv_n = 1.0 / HEAD
        mean = _mm(y, ones_bd) * inv_n
        yc = y - mean
        var = _mm(yc * yc, ones_bd) * inv_n
        yn = yc * lax.rsqrt(var + GN_EPS) * lnw_ref[...] + lnb_ref[...]
        o_ref[0, rows, :] = ((yn + res["bonus"]) * gate_ref[0, rows, :].astype(F32)
                             ).astype(o_ref.dtype)
        yield

    results = [dict() for _ in range(n_groups)]
    for _ in state_free_part(0, results[0]):
        pass
    for g in range(n_groups):
        carried = state_part(g, results[g])
        ahead = state_free_part(g + 1, results[g + 1]) if g + 1 < n_groups else iter(())
        n_carried, n_ahead = SCAN_GROUP + 1, 9
        for i, _ in enumerate(ahead):
            if (i * n_carried) // n_ahead != ((i + 1) * n_carried) // n_ahead:
                next(carried, None)
        for _ in carried:
            pass
    state_ref[...] = carried_state[0]


def _rwkv_scan(r, k, v, kkv, a, lw, gate, r_k, ln_w, ln_b):
    b, s, _ = r.shape
    tc = SCAN_BLOCK
    tile = pl.BlockSpec((1, tc, PAIR), lambda bi, p, ti: (bi, ti, p))
    vec = pl.BlockSpec((1, PAIR), lambda bi, p, ti: (0, p))
    return pl.pallas_call(
        _scan_kernel,
        grid=(b, D // PAIR, s // tc),
        in_specs=[tile] * 7 + [vec] * 3,
        out_specs=tile,
        out_shape=jax.ShapeDtypeStruct((b, s, D), BF16),
        scratch_shapes=[pltpu.VMEM((PAIR, PAIR), F32)],
        compiler_params=pltpu.CompilerParams(
            dimension_semantics=("arbitrary", "arbitrary", "arbitrary"),
            vmem_limit_bytes=VMEM_LIMIT),
        name="rwkv_scan",
    )(r, k, v, kkv, a, lw, gate, r_k, ln_w, ln_b)


def kernel(x, c, attn_w_in, attn_b_f, attn_w_out, rwkv_mu, rwkv_w_r, rwkv_w_k, rwkv_w_v, rwkv_w_o, rwkv_w0, rwkv_w1, rwkv_w2, rwkv_a0, rwkv_a1, rwkv_a2, rwkv_g1, rwkv_g2, rwkv_k_k, rwkv_k_a, rwkv_r_k, rwkv_ln_w, rwkv_ln_b, mod_w, mod_b, norm_mix, norm_ffn, ffn_w_in, ffn_w_out, norm_final):
    b, s, _ = x.shape
    mod = _modulation(c, mod_w, mod_b)
    mods = [[mod[l, :, i * D:(i + 1) * D].reshape(b, 1, D) for i in range(6)] for l in range(2)]
    row = lambda t: t.reshape(1, D)
    g_final = row(norm_final)

    sh_m, sc_m, gt_m, sh_f, sc_f, gt_f = mods[0]
    w_in = attn_w_in[0]
    qscale = HEAD ** -0.5 * LOG2E
    col_scale = jnp.ones((3 * D,), F32)
    col_scale = col_scale.at[0:FOX_W].set(qscale).at[3 * FOX_W:4 * FOX_W].set(qscale)
    w_main = (w_in[:, :3 * D] * col_scale[None, :]).astype(BF16)
    w_f = jnp.zeros((D, AUG), F32).at[:, :8].set(w_in[:, 3 * D:]).astype(BF16)
    b_f = jnp.zeros((1, AUG), F32).at[0, :8].set(attn_b_f[0])
    qa, ka, vt = _inproj(x, row(norm_mix[0]), sh_m, sc_m, w_main, w_f, b_f)
    o = _attention(qa, ka, vt)

    w_ffn_in = ffn_w_in.astype(BF16)
    w_ffn_out = ffn_w_out.astype(BF16)
    x = _post(x, o, attn_w_out[0].astype(BF16), gt_m, row(norm_ffn[0]), sh_f, sc_f, gt_f,
              w_ffn_in, w_ffn_out, 0, g_final, final_norm=False)

    sh_m, sc_m, gt_m, sh_f, sc_f, gt_f = mods[1]
    pad_in = lambda w, n: jnp.zeros((D, n), BF16).at[:, :w.shape[1]].set(w.astype(BF16))
    pad_out = lambda w, n: jnp.zeros((n, D), BF16).at[:w.shape[0], :].set(w.astype(BF16))
    r, k, v, kkv, a, lw, gate = _rwkv_proj(
        x, row(norm_mix[1]), sh_m, sc_m, rwkv_mu[0],
        rwkv_w_r[0].astype(BF16), rwkv_w_k[0].astype(BF16), rwkv_w_v[0].astype(BF16),
        pad_in(rwkv_w1[0], 128), pad_out(rwkv_w2[0], 128),
        pad_in(rwkv_a1[0], 128), pad_out(rwkv_a2[0], 128),
        pad_in(rwkv_g1[0], 256), pad_out(rwkv_g2[0], 256),
        row(rwkv_w0[0]), row(rwkv_a0[0]), row(rwkv_k_k[0]), row(rwkv_k_a[0]))
    y = _rwkv_scan(r, k, v, kkv, a, lw, gate, row(rwkv_r_k[0]), row(rwkv_ln_w[0]),
                   row(rwkv_ln_b[0]))
    x = _post(x, y, rwkv_w_o[0].astype(BF16), gt_m, row(norm_ffn[1]), sh_f, sc_f, gt_f,
              w_ffn_in, w_ffn_out, 1, g_final, final_norm=True)
    return x
```

```python
import functools
import math

import jax
import jax.numpy as jnp
import numpy as np
from jax import lax
from jax.experimental import pallas as pl
from jax.experimental.pallas import tpu as pltpu

F32 = jnp.float32
BF16 = jnp.bfloat16

D = 1024
HEAD = 64
PAIR = 2 * HEAD
N_PAIR_ATTN = 8
FOX_W = 512
D_FF = 2816
MXU_WIDTH = 256
FF_SLAB_EDGES = (0, 5 * MXU_WIDTH, D_FF)
MOBA_BLOCK = 256
N_MOBA_BLOCKS = 16
MOBA_TOPK = 3
AUG = 128
AUG_PER_HEAD = 16
NORM_EPS = 1e-6
GN_EPS = HEAD * 1e-5
LOG2E = 1.4426950408889634
NEG = -1e30
PEN = -30000.0
CHUNK = 64
SCAN_GROUP = 8
SCAN_BLOCK = 8 * SCAN_GROUP * CHUNK
SUM_ROWS = 16
ATTN_TQ = 256
RWKV_PROJ_COLS = 256
VMEM_LIMIT = 56 * 1024 * 1024


def _mm(a, b):
    return jnp.dot(a.astype(BF16), b.astype(BF16), preferred_element_type=F32)


def _mm_nt(a, b):
    return lax.dot_general(a.astype(BF16), b.astype(BF16), (((1,), (1,)), ((), ())),
                           preferred_element_type=F32)


def _mm_tn(a, b):
    return lax.dot_general(a.astype(BF16), b.astype(BF16), (((0,), (0,)), ((), ())),
                           preferred_element_type=F32)


def _sigmoid(x):
    return 1.0 / (1.0 + jnp.exp(-x))


def _softplus(x):
    return jnp.maximum(x, 0.0) + jnp.log(1.0 + jnp.exp(-jnp.abs(x)))


def _norm_mod(x, g, sh, sc):
    y = x * lax.rsqrt(jnp.mean(x * x, axis=-1, keepdims=True) + NORM_EPS)
    return (y * g) * (1.0 + sc) + sh


def _mod_kernel(c_ref, w_ref, b_ref, o_ref):
    c = c_ref[...]
    c1, c2, c3 = _split3(c * _sigmoid(c))
    w = w_ref[0]
    w1 = w.astype(BF16)
    w2 = (w - w1.astype(F32)).astype(BF16)
    dot = lambda a, b: jnp.dot(a, b, preferred_element_type=F32)
    o_ref[0] = (dot(c1, w1) + (dot(c2, w1) + dot(c1, w2))
                + (dot(c3, w1) + dot(c2, w2))) + b_ref[0]


def _modulation(c, mod_w, mod_b):
    depth, _, n = mod_w.shape
    b = c.shape[0]
    rows = 16
    c_pad = jnp.zeros((rows, D), F32).at[:b].set(c)
    tn = 1536
    out = pl.pallas_call(
        _mod_kernel,
        grid=(depth, n // tn),
        in_specs=[pl.BlockSpec((rows, D), lambda l, j: (0, 0)),
                  pl.BlockSpec((1, D, tn), lambda l, j: (l, 0, j)),
                  pl.BlockSpec((1, 1, tn), lambda l, j: (l, 0, j))],
        out_specs=pl.BlockSpec((1, rows, tn), lambda l, j: (l, 0, j)),
        out_shape=jax.ShapeDtypeStruct((depth, rows, n), F32),
        compiler_params=pltpu.CompilerParams(
            dimension_semantics=("arbitrary", "arbitrary"), vmem_limit_bytes=VMEM_LIMIT),
        name="mod",
    )(c_pad, mod_w, mod_b.reshape(depth, 1, n))
    return out[:, :b]


def _split3(z):
    z1 = z.astype(BF16)
    r1 = z - z1.astype(F32)
    z2 = r1.astype(BF16)
    return z1, z2, (r1 - z2.astype(F32)).astype(BF16)


def _mm_split3_lhs_exact(a_exact, b):
    aa = a_exact.astype(BF16)
    b1, b2, b3 = _split3(b)
    return (jnp.dot(aa, b1, preferred_element_type=F32)
            + jnp.dot(aa, b2, preferred_element_type=F32)
            + jnp.dot(aa, b3, preferred_element_type=F32))


def _moba_penalty_steps(gate, own, out):
    lane = lax.broadcasted_iota(jnp.int32, gate.shape, 1)
    n = lane & (N_MOBA_BLOCKS - 1)
    past = n < own
    g = jnp.where(past, gate, -jnp.inf)
    rank = jnp.zeros(gate.shape, jnp.int32)
    for k in range(1, N_MOBA_BLOCKS):
        gk = jnp.where(n >= k, pltpu.roll(g, k, 1), pltpu.roll(g, AUG - N_MOBA_BLOCKS + k, 1))
        beats = (gk > g) | ((gk == g) & (n >= k))
        rank = rank + beats.astype(jnp.int32)
        yield
    keep = (past & (rank < MOBA_TOPK)) | (n == own)
    out[0] = jnp.where(keep, 0.0, PEN)


def _inproj_kernel(x_ref, g_ref, sh_ref, sc_ref, w_ref, wf_ref, bf_ref, sel_c_ref, const_c_ref,
                   sel_p_ref, qa_ref, ka_ref, vt_ref, carry_ref, km_ref):
    tm = x_ref.shape[1]
    si = pl.program_id(1)

    @pl.when(si == 0)
    def _():
        carry_ref[...] = jnp.zeros_like(carry_ref)
        km_ref[...] = jnp.zeros_like(km_ref)

    h = _norm_mod(x_ref[0], g_ref[...], sh_ref[0], sc_ref[0]).astype(BF16)
    width = PAIR + AUG

    def proj(lo, hi):
        return jnp.dot(h, w_ref[:, lo:hi], preferred_element_type=F32)

    def put(ref, pair, value):
        ref[0, :, pair * width:pair * width + PAIR] = value.astype(BF16)

    def put_bias(ref, pair, value):
        ref[0, :, pair * width + PAIR:(pair + 1) * width] = value.astype(BF16)

    q_mb = proj(3 * FOX_W, 4 * FOX_W)
    qb = q_mb.astype(BF16)
    km1, km2, km3 = _split3(km_ref[...])
    gate = (jnp.dot(qb, km1, preferred_element_type=F32)
            + jnp.dot(qb, km2, preferred_element_type=F32)
            + jnp.dot(qb, km3, preferred_element_type=F32))
    pen = [None]
    ranking = _moba_penalty_steps(gate, si, pen)

    def fox_bias():
        f = jnp.dot(h, wf_ref[...], preferred_element_type=F32) + bf_ref[...]
        lf = -_softplus(-f)
        row = lax.broadcasted_iota(jnp.int32, (tm, tm), 0)
        col = lax.broadcasted_iota(jnp.int32, (tm, tm), 1)
        tri = jnp.where(row >= col, 1.0, 0.0).astype(F32)
        cum = _mm_split3_lhs_exact(tri, lf) + carry_ref[...]
        carry_ref[...] = cum[tm - 1:tm, :]
        c3 = jnp.concatenate(_split3(cum * LOG2E), axis=1)
        aug = jnp.dot(c3, sel_c_ref[...], preferred_element_type=F32) + const_c_ref[...]
        for p in range(4):
            put_bias(qa_ref, p, aug[:, p * PAIR:(p + 1) * PAIR])
            put_bias(ka_ref, p, aug[:, FOX_W + p * PAIR:FOX_W + (p + 1) * PAIR])

    def fox_qk(half):
        q = proj(half * 2 * PAIR, (half + 1) * 2 * PAIR)
        k = proj(FOX_W + half * 2 * PAIR, FOX_W + (half + 1) * 2 * PAIR)
        for i in range(2):
            put(qa_ref, 2 * half + i, q[:, i * PAIR:(i + 1) * PAIR])
            put(ka_ref, 2 * half + i, k[:, i * PAIR:(i + 1) * PAIR])

    def values(group, half):
        lo = (2 + 3 * group) * FOX_W + half * 2 * PAIR
        v = proj(lo, lo + 2 * PAIR)
        first = 4 * group + 2 * half
        vt_ref[0, first:first + 2, 0] = v.T.reshape(2, PAIR, tm).astype(BF16)

    def moba_k():
        k_mb = proj(4 * FOX_W, 5 * FOX_W)
        lane_p = lax.broadcasted_iota(jnp.int32, (tm, AUG), 1)
        onehot = jnp.where((lane_p < 2 * AUG_PER_HEAD) & ((lane_p & (N_MOBA_BLOCKS - 1)) == si),
                           1.0, 0.0)
        for p in range(4):
            put(ka_ref, 4 + p, k_mb[:, p * PAIR:(p + 1) * PAIR])
            put_bias(ka_ref, 4 + p, onehot)
        kb_t = k_mb.astype(BF16).astype(F32).T
        place = jnp.where((lane_p & (N_MOBA_BLOCKS - 1)) == si, 1.0 / tm, 0.0)
        upd = jnp.dot(kb_t.astype(BF16), place.astype(BF16), preferred_element_type=F32)
        d_head = lax.broadcasted_iota(jnp.int32, (FOX_W, AUG), 0) >> 6
        l_head = lax.broadcasted_iota(jnp.int32, (FOX_W, AUG), 1) >> 4
        km_ref[...] += jnp.where(d_head == l_head, upd, 0.0)

    others = [fox_bias, lambda: fox_qk(0), lambda: fox_qk(1), moba_k,
              lambda: values(0, 0), lambda: values(0, 1), lambda: values(1, 0),
              lambda: values(1, 1)]
    for i, _ in enumerate(ranking):
        if i % 2 == 0 and others:
            others.pop(0)()
    for task in others:
        task()
    aug_mb_q = jnp.dot(pen[0].astype(BF16), sel_p_ref[...], preferred_element_type=F32)
    for p in range(4):
        put(qa_ref, 4 + p, q_mb[:, p * PAIR:(p + 1) * PAIR])
        put_bias(qa_ref, 4 + p, aug_mb_q[:, p * PAIR:(p + 1) * PAIR])


def _bias_lane_selectors():
    sel_c = np.zeros((3 * AUG, 2 * FOX_W), np.float32)
    const_c = np.zeros((1, 2 * FOX_W), np.float32)
    sel_p = np.zeros((AUG, FOX_W), np.float32)
    for head in range(8):
        base = (head // 2) * AUG + (head % 2) * AUG_PER_HEAD
        for i in range(3):
            sel_c[i * AUG + head, base + i] = 1.0
            sel_c[i * AUG + head, FOX_W + base + 3 + i] = -1.0
            const_c[0, base + 3 + i] = 1.0
            const_c[0, FOX_W + base + i] = 1.0
        for n in range(N_MOBA_BLOCKS):
            sel_p[head * N_MOBA_BLOCKS + n, base + n] = 1.0
    return jnp.asarray(sel_c, BF16), jnp.asarray(const_c, F32), jnp.asarray(sel_p, BF16)


def _inproj(x, g, sh, sc, w_main, w_f, b_f):
    b, s, _ = x.shape
    tm = MOBA_BLOCK
    n = w_main.shape[1]
    width = N_PAIR_ATTN * (PAIR + AUG)
    sel_c, const_c, sel_p = _bias_lane_selectors()
    row3 = lambda bi, si: (bi, 0, 0)
    const2 = lambda bi, si: (0, 0)
    tile3 = lambda bi, si: (bi, si, 0)
    full = lambda arr: pl.BlockSpec(arr.shape, const2)
    return pl.pallas_call(
        _inproj_kernel,
        grid=(b, s // tm),
        in_specs=[pl.BlockSpec((1, tm, D), tile3),
                  pl.BlockSpec((1, D), const2),
                  pl.BlockSpec((1, 1, D), row3),
                  pl.BlockSpec((1, 1, D), row3),
                  pl.BlockSpec((D, n), const2),
                  pl.BlockSpec((D, AUG), const2),
                  pl.BlockSpec((1, AUG), const2),
                  full(sel_c), full(const_c), full(sel_p)],
        out_specs=[pl.BlockSpec((1, tm, width), tile3),
                   pl.BlockSpec((1, tm, width), tile3),
                   pl.BlockSpec((1, N_PAIR_ATTN, 1, PAIR, tm), lambda bi, si: (bi, 0, si, 0, 0))],
        out_shape=[jax.ShapeDtypeStruct((b, s, width), BF16),
                   jax.ShapeDtypeStruct((b, s, width), BF16),
                   jax.ShapeDtypeStruct((b, N_PAIR_ATTN, s // tm, PAIR, tm), BF16)],
        scratch_shapes=[pltpu.VMEM((1, AUG), F32), pltpu.VMEM((FOX_W, AUG), F32)],
        compiler_params=pltpu.CompilerParams(
            dimension_semantics=("arbitrary", "arbitrary"), vmem_limit_bytes=VMEM_LIMIT),
        name="inproj",
    )(x, g, sh, sc, w_main, w_f, b_f, sel_c, const_c, sel_p)


def _attn_kernel(qa_ref, ka_ref, vt_ref, o_ref, s_even, s_odd, macc_ref, mprev_ref, acc_ref):
    tq = qa_ref.shape[1]
    tk = vt_ref.shape[4]
    unit = tq // tk
    nq = s_even.shape[1] // tq
    j = pl.program_id(2)
    qa = qa_ref[0]
    lane = lax.broadcasted_iota(jnp.int32, (1, PAIR + AUG), 1)
    head0 = (lane < HEAD) | ((lane >= PAIR) & (lane < PAIR + AUG_PER_HEAD))
    head1 = ((lane >= HEAD) & (lane < PAIR)) | (
        (lane >= PAIR + AUG_PER_HEAD) & (lane < PAIR + 2 * AUG_PER_HEAD))
    m_prev = mprev_ref[...]
    macc_ref[...] = jnp.full(macc_ref.shape, NEG, F32)
    acc_ref[...] = jnp.zeros_like(acc_ref)
    ones_rows = jnp.ones((SUM_ROWS, tk), BF16)

    def score_tiles(s_ref, first_tile, n, diagonal):
        rows = n * tk
        start = pl.multiple_of(first_tile * tk, tk)
        k = ka_ref[0, pl.ds(start, rows), :]
        kz = jnp.zeros_like(k)
        k2 = jnp.concatenate([jnp.where(head0, k, kz), jnp.where(head1, k, kz)], axis=0)
        st = lax.dot_general(k2, qa, (((1,), (1,)), ((), ())), preferred_element_type=F32)
        full = rows - tq if diagonal else rows
        for h in range(2):
            macc = macc_ref[h]
            if full:
                part = st[h * rows:h * rows + full]
                s_ref[h, pl.ds(start, full), :] = part
                macc = jnp.maximum(macc, jnp.max(part.reshape(full // 8, 8, tq), axis=0))
            if diagonal:
                key_pos = lax.broadcasted_iota(jnp.int32, (tq, tq), 0)
                qry_pos = lax.broadcasted_iota(jnp.int32, (tq, tq), 1)
                part = jnp.where(key_pos <= qry_pos, st[h * rows + full:(h + 1) * rows], NEG)
                s_ref[h, pl.ds(start + full, tq), :] = part
                macc = jnp.maximum(macc, jnp.max(part.reshape(tq // 8, 8, tq), axis=0))
            macc_ref[h] = macc

    def pv_tile(s_ref, t):
        start = pl.multiple_of(t * tk, tk)
        vt = vt_ref[0, 0, t]
        accs = []
        for h in range(2):
            p = jnp.exp2(s_ref[h, pl.ds(start, tk), :] - m_prev[h])
            lhs = jnp.concatenate([vt[h * HEAD:(h + 1) * HEAD, :], ones_rows], axis=0)
            accs.append(jnp.dot(lhs, p.astype(BF16), preferred_element_type=F32))
        return jnp.stack(accs)

    def pv_tiles(s_ref, first_tile, n):
        parts = [pv_tile(s_ref, first_tile + t) for t in range(n)]
        while len(parts) > 1:
            parts = [a + b for a, b in zip(parts[0::2], parts[1::2])]
        acc_ref[...] += parts[0]

    def step(s_cur, s_prev):
        bit = nq // 2
        while bit >= 1:
            in_group = ((j & bit) != 0) & (j < nq)
            lowest = (j & (bit - 1)) == 0
            first = (j & (-2 * bit)) * unit
            n = bit * unit

            if bit > 1:
                @pl.when(in_group & jnp.logical_not(lowest))
                def _(n=n, first=first):
                    score_tiles(s_cur, first, n, False)
                    pv_tiles(s_prev, first, n)

            @pl.when(in_group & lowest)
            def _(n=n, first=first):
                score_tiles(s_cur, first, n + unit, True)
                pv_tiles(s_prev, first, n)
            bit //= 2

        @pl.when(j == 0)
        def _():
            score_tiles(s_cur, 0, unit, True)

        @pl.when(j == nq)
        def _():
            for first in range(0, nq * unit, 8):
                pv_tiles(s_prev, first, 8)

    @pl.when((j & 1) == 0)
    def _():
        step(s_even, s_odd)

    @pl.when((j & 1) == 1)
    def _():
        step(s_odd, s_even)

    mprev_ref[...] = jnp.max(macc_ref[...], axis=1, keepdims=True)

    @pl.when(j > 0)
    def _():
        acc = acc_ref[...]
        ot = jnp.concatenate([acc[h, :HEAD] / acc[h, HEAD:HEAD + 1] for h in range(2)], axis=0)
        o_ref[0] = ot.T.astype(o_ref.dtype)


def _attention(qa, ka, vt):
    b, s, _ = qa.shape
    tq = ATTN_TQ
    nk, tk = vt.shape[2], vt.shape[4]
    nq = s // tq
    assert nq & (nq - 1) == 0 and (nq * tq // tk) % 8 == 0, "query tiles must be a power of two"
    return pl.pallas_call(
        _attn_kernel,
        grid=(b, N_PAIR_ATTN, nq + 1),
        in_specs=[pl.BlockSpec((1, tq, PAIR + AUG), lambda bi, p, j: (bi, jnp.minimum(j, nq - 1), p)),
                  pl.BlockSpec((1, s, PAIR + AUG), lambda bi, p, j: (bi, 0, p)),
                  pl.BlockSpec((1, 1, nk, PAIR, tk), lambda bi, p, j: (bi, p, 0, 0, 0))],
        out_specs=pl.BlockSpec((1, tq, PAIR), lambda bi, p, j: (bi, jnp.maximum(j - 1, 0), p)),
        out_shape=jax.ShapeDtypeStruct((b, s, N_PAIR_ATTN * PAIR), BF16),
        scratch_shapes=[pltpu.VMEM((2, s, tq), F32),
                        pltpu.VMEM((2, s, tq), F32),
                        pltpu.VMEM((2, 8, tq), F32),
                        pltpu.VMEM((2, 1, tq), F32),
                        pltpu.VMEM((2, HEAD + SUM_ROWS, tq), F32)],
        compiler_params=pltpu.CompilerParams(
            dimension_semantics=("arbitrary", "arbitrary", "arbitrary"),
            vmem_limit_bytes=VMEM_LIMIT),
        name="attn",
    )(qa, ka, vt)


def _post_kernel(x_ref, a_ref, wo_ref, gtm_ref, g_ref, sh_ref, sc_ref, gtf_ref,
                 win_ref, wd_ref, gfin_ref, o_ref, *, final_norm):
    x1 = x_ref[0] + gtm_ref[0] * jnp.dot(a_ref[0], wo_ref[...], preferred_element_type=F32)
    h = _norm_mod(x1, g_ref[...], sh_ref[0], sc_ref[0]).astype(BF16)
    ff = jnp.zeros_like(x1)
    for lo, hi in zip(FF_SLAB_EDGES[:-1], FF_SLAB_EDGES[1:]):
        gate = jnp.dot(h, win_ref[0, :, lo:hi], preferred_element_type=F32)
        up = jnp.dot(h, win_ref[0, :, D_FF + lo:D_FF + hi], preferred_element_type=F32)
        act = (gate * _sigmoid(gate) * up).astype(BF16)
        ff = ff + jnp.dot(act, wd_ref[0, lo:hi, :], preferred_element_type=F32)
    x2 = x1 + gtf_ref[0] * ff
    if final_norm:
        x2 = x2 * lax.rsqrt(jnp.mean(x2 * x2, axis=-1, keepdims=True) + NORM_EPS) * gfin_ref[...]
    o_ref[0] = x2


def _post(x, a, w_o, gt_m, g_ffn, sh_f, sc_f, gt_f, w_in, w_down, layer, g_final, final_norm):
    b, s, _ = x.shape
    tm = 512
    row3 = lambda bi, si: (bi, 0, 0)
    const2 = lambda bi, si: (0, 0)
    tile3 = lambda bi, si: (bi, si, 0)
    of_layer = lambda bi, si: (layer, 0, 0)
    once = pl.Buffered(1)
    kern = functools.partial(_post_kernel, final_norm=final_norm)
    return pl.pallas_call(
        kern,
        grid=(b, s // tm),
        in_specs=[pl.BlockSpec((1, tm, D), tile3),
                  pl.BlockSpec((1, tm, D), tile3),
                  pl.BlockSpec((D, D), const2),
                  pl.BlockSpec((1, 1, D), row3),
                  pl.BlockSpec((1, D), const2),
                  pl.BlockSpec((1, 1, D), row3),
                  pl.BlockSpec((1, 1, D), row3),
                  pl.BlockSpec((1, 1, D), row3),
                  pl.BlockSpec((1, D, 2 * D_FF), of_layer, pipeline_mode=once),
                  pl.BlockSpec((1, D_FF, D), of_layer, pipeline_mode=once),
                  pl.BlockSpec((1, D), const2)],
        out_specs=pl.BlockSpec((1, tm, D), tile3),
        out_shape=jax.ShapeDtypeStruct((b, s, D), F32),
        compiler_params=pltpu.CompilerParams(
            dimension_semantics=("arbitrary", "arbitrary"), vmem_limit_bytes=VMEM_LIMIT),
        name="post_final" if final_norm else "post",
    )(x, a, w_o, gt_m, g_ffn, sh_f, sc_f, gt_f, w_in, w_down, g_final)


def _rwkvproj_kernel(x_ref, g_ref, sh_ref, sc_ref, mu_ref, wr_ref, wk_ref, wv_ref,
                     w1_ref, w2_ref, a1_ref, a2_ref, g1_ref, g2_ref,
                     w0_ref, a0_ref, kk_ref, ka_ref,
                     r_ref, k_ref, v_ref, kkv_ref, a_ref, lw_ref, gate_ref, carry_ref):
    tm = x_ref.shape[1]

    @pl.when(pl.program_id(1) == 0)
    def _():
        carry_ref[...] = jnp.zeros_like(carry_ref)

    h = _norm_mod(x_ref[0], g_ref[...], sh_ref[0], sc_ref[0])
    rowi = lax.broadcasted_iota(jnp.int32, (tm, 1), 0)
    h_prev = jnp.where(rowi == 0, carry_ref[...], pltpu.roll(h, 1, 0))
    carry_ref[...] = h[tm - 1:tm, :]
    xx = h_prev - h
    mix = lambda i: (h + xx * mu_ref[i:i + 1, :]).astype(BF16)
    xr, xk, xv = mix(0), mix(2), mix(3)
    hid_w = jnp.tanh(_mm(mix(1), w1_ref[...])).astype(BF16)
    hid_a = _mm(mix(4), a1_ref[...]).astype(BF16)
    hid_g = _sigmoid(_mm(mix(5), g1_ref[...])).astype(BF16)
    cw = RWKV_PROJ_COLS
    for c in range(D // cw):
        cols = slice(c * cw, (c + 1) * cw)
        k = jnp.dot(xk, wk_ref[:, cols], preferred_element_type=F32)
        a = _sigmoid(a0_ref[:, cols] + jnp.dot(hid_a, a2_ref[:, cols], preferred_element_type=F32))
        wl = w0_ref[:, cols] + jnp.dot(hid_w, w2_ref[:, cols], preferred_element_type=F32)
        k_ref[0, :, cols] = (k * (1.0 + (a - 1.0) * ka_ref[:, cols])).astype(k_ref.dtype)
        kkv_ref[0, :, cols] = k * kk_ref[:, cols]
        a_ref[0, :, cols] = a
        lw_ref[0, :, cols] = -math.exp(-0.5) * _sigmoid(wl)
        r_ref[0, :, cols] = jnp.dot(xr, wr_ref[:, cols],
                                    preferred_element_type=F32).astype(r_ref.dtype)
        v_ref[0, :, cols] = jnp.dot(xv, wv_ref[:, cols],
                                    preferred_element_type=F32).astype(v_ref.dtype)
        gate_ref[0, :, cols] = jnp.dot(hid_g, g2_ref[:, cols],
                                       preferred_element_type=F32).astype(gate_ref.dtype)


def _rwkv_proj(x, g, sh, sc, mu, w_r, w_k, w_v, w1, w2, a1, a2, g1, g2, w0, a0, k_k, k_a):
    b, s, _ = x.shape
    tm = 512
    row3 = lambda bi, si: (bi, 0, 0)
    const2 = lambda bi, si: (0, 0)
    tile3 = lambda bi, si: (bi, si, 0)
    full = lambda arr: pl.BlockSpec(arr.shape, const2)
    outs = pl.pallas_call(
        _rwkvproj_kernel,
        grid=(b, s // tm),
        in_specs=[pl.BlockSpec((1, tm, D), tile3), full(g),
                  pl.BlockSpec((1, 1, D), row3), pl.BlockSpec((1, 1, D), row3),
                  full(mu), full(w_r), full(w_k), full(w_v),
                  full(w1), full(w2), full(a1), full(a2), full(g1), full(g2),
                  full(w0), full(a0), full(k_k), full(k_a)],
        out_specs=[pl.BlockSpec((1, tm, D), tile3)] * 7,
        out_shape=[jax.ShapeDtypeStruct((b, s, D), dt) for dt in (BF16, BF16, BF16, F32, F32, F32, BF16)],
        scratch_shapes=[pltpu.VMEM((1, D), F32)],
        compiler_params=pltpu.CompilerParams(
            dimension_semantics=("arbitrary", "arbitrary"), vmem_limit_bytes=VMEM_LIMIT),
        name="rwkv_proj",
    )(x, g, sh, sc, mu, w_r, w_k, w_v, w1, w2, a1, a2, g1, g2, w0, a0, k_k, k_a)
    return outs


def _scan_kernel(r_ref, k_ref, v_ref, kkv_ref, a_ref, lw_ref, gate_ref,
                 rk_ref, lnw_ref, lnb_ref, o_ref, state_ref):
    tc = r_ref.shape[1]
    group_rows = SCAN_GROUP * CHUNK
    n_groups = tc // group_rows

    @pl.when(pl.program_id(2) == 0)
    def _():
        state_ref[...] = jnp.zeros_like(state_ref)

    li = lax.broadcasted_iota(jnp.int32, (PAIR, PAIR), 0)
    lj = lax.broadcasted_iota(jnp.int32, (PAIR, PAIR), 1)
    ones_bd = jnp.where((li >> 6) == (lj >> 6), 1.0, 0.0).astype(BF16)
    strict = (li & (CHUNK - 1)) > (lj & (CHUNK - 1))
    incl = (li & (CHUNK - 1)) >= (lj & (CHUNK - 1))
    eye = jnp.where(li == lj, 1.0, 0.0).astype(F32)
    tri_rows = 4 * CHUNK
    ti = lax.broadcasted_iota(jnp.int32, (tri_rows, tri_rows), 0)
    tj = lax.broadcasted_iota(jnp.int32, (tri_rows, tri_rows), 1)
    tri = jnp.where((ti >= tj) & ((ti >> 6) == (tj >> 6)), 1.0, 0.0).astype(BF16)
    is_h0 = lax.broadcasted_iota(jnp.int32, (1, PAIR), 1) < HEAD
    chunks = range(SCAN_GROUP)

    def stack(z):
        return jnp.concatenate([jnp.where(is_h0, z, 0.0), jnp.where(is_h0, 0.0, z)], axis=0)

    def state_free_part(g, out):
        rows = slice(g * group_rows, (g + 1) * group_rows)
        r = r_ref[0, rows, :].astype(F32)
        k = k_ref[0, rows, :].astype(F32)
        v = v_ref[0, rows, :].astype(F32)
        kkv = kkv_ref[0, rows, :]
        lw = lw_ref[0, rows, :]
        lw1 = lw.astype(BF16)
        lw2 = (lw - lw1.astype(F32)).astype(BF16)
        cl = jnp.concatenate(
            [jnp.dot(tri, lw1[i:i + tri_rows], preferred_element_type=F32)
             + jnp.dot(tri, lw2[i:i + tri_rows], preferred_element_type=F32)
             for i in range(0, group_rows, tri_rows)], axis=0)
        n2 = _mm(kkv * kkv, ones_bd)
        kk = kkv / jnp.maximum(jnp.sqrt(n2), 1e-12)
        bvec = kk * a_ref[0, rows, :]
        egi = jnp.exp(-cl)
        a_t = -kk * jnp.exp(cl - lw)
        b_t = bvec * egi
        k_t = k * egi
        r_t = r * jnp.exp(cl)
        out["bonus"] = _mm(r * k * rk_ref[...], ones_bd) * v
        sls = [slice(c * CHUNK, (c + 1) * CHUNK) for c in chunks]
        cl_end = [cl[(c + 1) * CHUNK - 1:(c + 1) * CHUNK, :] for c in chunks]
        out["decay"] = [jnp.exp(ce) for ce in cl_end]
        tails = [jnp.exp(cl_end[c] - cl[sls[c]]) for c in chunks]
        a_s = [stack(a_t[sl]) for sl in sls]
        r_s = [stack(r_t[sl]) for sl in sls]
        v_s = [stack(v[sl]) for sl in sls]
        bh_s = [stack(bvec[sls[c]] * tails[c]) for c in chunks]
        kh_s = [stack(k[sls[c]] * tails[c]) for c in chunks]
        grams = [_mm_nt(jnp.concatenate([a_s[c], r_s[c]], axis=0),
                        jnp.concatenate([stack(b_t[sls[c]]), stack(k_t[sls[c]])], axis=0))
                 for c in chunks]
        yield
        m_ak = [jnp.where(strict, gm[:PAIR, PAIR:], 0.0) for gm in grams]
        a_r = [jnp.concatenate([jnp.where(incl, gm[PAIR:, :PAIR], 0.0),
                                jnp.where(incl, gm[PAIR:, PAIR:], 0.0)], axis=1) for gm in grams]
        rhs = [jnp.concatenate([a_s[c], _mm(m_ak[c], v_s[c])], axis=1) for c in chunks]
        mps = [jnp.where(strict, gm[:PAIR, :PAIR], 0.0) for gm in grams]
        ts = [eye + mp for mp in mps]
        mps = [_mm(mp, mp) for mp in mps]
        yield
        for level in range(1, 6):
            if level < 5:
                res = [_mm(mps[c], jnp.concatenate([mps[c], ts[c]], axis=1)) for c in chunks]
                mps = [rs[:, :PAIR] for rs in res]
                ts = [ts[c] + res[c][:, PAIR:] for c in chunks]
            else:
                ts = [ts[c] + _mm(mps[c], ts[c]) for c in chunks]
            yield
        xs = [_mm(ts[c], rhs[c]) for c in chunks]
        gys = [_mm(a_r[c], jnp.concatenate(
            [xs[c], jnp.concatenate([jnp.zeros_like(v_s[c]), v_s[c]], axis=1)], axis=0))
               for c in chunks]
        out["g"] = [r_s[c] + gys[c][:, :PAIR] for c in chunks]
        out["y0"] = [gy[:, PAIR:] for gy in gys]
        yield
        out["pb"] = [_mm_tn(xs[c][:, :PAIR], bh_s[c]) for c in chunks]
        out["q"] = [_mm_tn(jnp.concatenate([xs[c][:, PAIR:], v_s[c]], axis=0),
                           jnp.concatenate([bh_s[c], kh_s[c]], axis=0)) for c in chunks]
        yield

    carried_state = [state_ref[...]]

    def state_part(g, res):
        ys = []
        for c in chunks:
            state = carried_state[0]
            y_s = _mm_nt(res["g"][c], state) + res["y0"][c]
            ys.append(y_s[:CHUNK] + y_s[CHUNK:])
            carried_state[0] = state * res["decay"][c] + _mm(state, res["pb"][c]) + res["q"][c]
            yield
        rows = slice(g * group_rows, (g + 1) * group_rows)
        y = jnp.concatenate(ys, axis=0)
        inv_n = 1.0 / HEAD
        mean = _mm(y, ones_bd) * inv_n
        yc = y - mean
        var = _mm(yc * yc, ones_bd) * inv_n
        yn = yc * lax.rsqrt(var + GN_EPS) * lnw_ref[...] + lnb_ref[...]
        o_ref[0, rows, :] = ((yn + res["bonus"]) * gate_ref[0, rows, :].astype(F32)
                             ).astype(o_ref.dtype)
        yield

    results = [dict() for _ in range(n_groups)]
    for _ in state_free_part(0, results[0]):
        pass
    for g in range(n_groups):
        carried = state_part(g, results[g])
        ahead = state_free_part(g + 1, results[g + 1]) if g + 1 < n_groups else iter(())
        n_carried, n_ahead = SCAN_GROUP + 1, 9
        for i, _ in enumerate(ahead):
            if (i * n_carried) // n_ahead != ((i + 1) * n_carried) // n_ahead:
                next(carried, None)
        for _ in carried:
            pass
    state_ref[...] = carried_state[0]


def _rwkv_scan(r, k, v, kkv, a, lw, gate, r_k, ln_w, ln_b):
    b, s, _ = r.shape
    tc = SCAN_BLOCK
    tile = pl.BlockSpec((1, tc, PAIR), lambda bi, p, ti: (bi, ti, p))
    vec = pl.BlockSpec((1, PAIR), lambda bi, p, ti: (0, p))
    return pl.pallas_call(
        _scan_kernel,
        grid=(b, D // PAIR, s // tc),
        in_specs=[tile] * 7 + [vec] * 3,
        out_specs=tile,
        out_shape=jax.ShapeDtypeStruct((b, s, D), BF16),
        scratch_shapes=[pltpu.VMEM((PAIR, PAIR), F32)],
        compiler_params=pltpu.CompilerParams(
            dimension_semantics=("arbitrary", "arbitrary", "arbitrary"),
            vmem_limit_bytes=VMEM_LIMIT),
        name="rwkv_scan",
    )(r, k, v, kkv, a, lw, gate, r_k, ln_w, ln_b)


def kernel(x, c, attn_w_in, attn_b_f, attn_w_out, rwkv_mu, rwkv_w_r, rwkv_w_k, rwkv_w_v, rwkv_w_o, rwkv_w0, rwkv_w1, rwkv_w2, rwkv_a0, rwkv_a1, rwkv_a2, rwkv_g1, rwkv_g2, rwkv_k_k, rwkv_k_a, rwkv_r_k, rwkv_ln_w, rwkv_ln_b, mod_w, mod_b, norm_mix, norm_ffn, ffn_w_in, ffn_w_out, norm_final):
    b, s, _ = x.shape
    mod = _modulation(c, mod_w, mod_b)
    mods = [[mod[l, :, i * D:(i + 1) * D].reshape(b, 1, D) for i in range(6)] for l in range(2)]
    row = lambda t: t.reshape(1, D)
    g_final = row(norm_final)

    sh_m, sc_m, gt_m, sh_f, sc_f, gt_f = mods[0]
    w_in = attn_w_in[0]
    qscale = HEAD ** -0.5 * LOG2E
    col_scale = jnp.ones((3 * D,), F32)
    col_scale = col_scale.at[0:FOX_W].set(qscale).at[3 * FOX_W:4 * FOX_W].set(qscale)
    w_main = (w_in[:, :3 * D] * col_scale[None, :]).astype(BF16)
    w_f = jnp.zeros((D, AUG), F32).at[:, :8].set(w_in[:, 3 * D:]).astype(BF16)
    b_f = jnp.zeros((1, AUG), F32).at[0, :8].set(attn_b_f[0])
    qa, ka, vt = _inproj(x, row(norm_mix[0]), sh_m, sc_m, w_main, w_f, b_f)
    o = _attention(qa, ka, vt)

    w_ffn_in = ffn_w_in.astype(BF16)
    w_ffn_out = ffn_w_out.astype(BF16)
    x = _post(x, o, attn_w_out[0].astype(BF16), gt_m, row(norm_ffn[0]), sh_f, sc_f, gt_f,
              w_ffn_in, w_ffn_out, 0, g_final, final_norm=False)

    sh_m, sc_m, gt_m, sh_f, sc_f, gt_f = mods[1]
    pad_in = lambda w, n: jnp.zeros((D, n), BF16).at[:, :w.shape[1]].set(w.astype(BF16))
    pad_out = lambda w, n: jnp.zeros((n, D), BF16).at[:w.shape[0], :].set(w.astype(BF16))
    r, k, v, kkv, a, lw, gate = _rwkv_proj(
        x, row(norm_mix[1]), sh_m, sc_m, rwkv_mu[0],
        rwkv_w_r[0].astype(BF16), rwkv_w_k[0].astype(BF16), rwkv_w_v[0].astype(BF16),
        pad_in(rwkv_w1[0], 128), pad_out(rwkv_w2[0], 128),
        pad_in(rwkv_a1[0], 128), pad_out(rwkv_a2[0], 128),
        pad_in(rwkv_g1[0], 256), pad_out(rwkv_g2[0], 256),
        row(rwkv_w0[0]), row(rwkv_a0[0]), row(rwkv_k_k[0]), row(rwkv_k_a[0]))
    y = _rwkv_scan(r, k, v, kkv, a, lw, gate, row(rwkv_r_k[0]), row(rwkv_ln_w[0]),
                   row(rwkv_ln_b[0]))
    x = _post(x, y, rwkv_w_o[0].astype(BF16), gt_m, row(norm_ffn[1]), sh_f, sc_f, gt_f,
              w_ffn_in, w_ffn_out, 1, g_final, final_norm=True)
    return x
```

```python
import functools
import math

import jax
import jax.numpy as jnp
import numpy as np
from jax import lax
from jax.experimental import pallas as pl
from jax.experimental.pallas import tpu as pltpu

F32 = jnp.float32
BF16 = jnp.bfloat16

D = 1024
HEAD = 64
PAIR = 2 * HEAD
N_PAIR_ATTN = 8
FOX_W = 512
D_FF = 2816
MXU_WIDTH = 256
FF_SLAB_EDGES = (0, 5 * MXU_WIDTH, D_FF)
MOBA_BLOCK = 256
N_MOBA_BLOCKS = 16
MOBA_TOPK = 3
AUG = 128
AUG_PER_HEAD = 16
NORM_EPS = 1e-6
GN_EPS = HEAD * 1e-5
LOG2E = 1.4426950408889634
NEG = -1e30
PEN = -30000.0
CHUNK = 64
SCAN_GROUP = 8
SCAN_BLOCK = 8 * SCAN_GROUP * CHUNK
SUM_ROWS = 16
ATTN_SUBSTEPS = 4
ATTN_TQ = 256
RWKV_PROJ_COLS = 256
VMEM_LIMIT = 56 * 1024 * 1024


def _mm(a, b):
    return jnp.dot(a.astype(BF16), b.astype(BF16), preferred_element_type=F32)


def _mm_nt(a, b):
    return lax.dot_general(a.astype(BF16), b.astype(BF16), (((1,), (1,)), ((), ())),
                           preferred_element_type=F32)


def _mm_tn(a, b):
    return lax.dot_general(a.astype(BF16), b.astype(BF16), (((0,), (0,)), ((), ())),
                           preferred_element_type=F32)


def _sigmoid(x):
    return 1.0 / (1.0 + jnp.exp(-x))


def _softplus(x):
    return jnp.maximum(x, 0.0) + jnp.log(1.0 + jnp.exp(-jnp.abs(x)))


def _norm_mod(x, g, sh, sc):
    y = x * lax.rsqrt(jnp.mean(x * x, axis=-1, keepdims=True) + NORM_EPS)
    return (y * g) * (1.0 + sc) + sh


def _mod_kernel(c_ref, w_ref, b_ref, o_ref):
    c = c_ref[...]
    c1, c2, c3 = _split3(c * _sigmoid(c))
    w = w_ref[0]
    w1 = w.astype(BF16)
    w2 = (w - w1.astype(F32)).astype(BF16)
    dot = lambda a, b: jnp.dot(a, b, preferred_element_type=F32)
    o_ref[0] = (dot(c1, w1) + (dot(c2, w1) + dot(c1, w2))
                + (dot(c3, w1) + dot(c2, w2))) + b_ref[0]


def _modulation(c, mod_w, mod_b):
    depth, _, n = mod_w.shape
    b = c.shape[0]
    rows = 16
    c_pad = jnp.zeros((rows, D), F32).at[:b].set(c)
    tn = 1536
    out = pl.pallas_call(
        _mod_kernel,
        grid=(depth, n // tn),
        in_specs=[pl.BlockSpec((rows, D), lambda l, j: (0, 0)),
                  pl.BlockSpec((1, D, tn), lambda l, j: (l, 0, j)),
                  pl.BlockSpec((1, 1, tn), lambda l, j: (l, 0, j))],
        out_specs=pl.BlockSpec((1, rows, tn), lambda l, j: (l, 0, j)),
        out_shape=jax.ShapeDtypeStruct((depth, rows, n), F32),
        compiler_params=pltpu.CompilerParams(
            dimension_semantics=("arbitrary", "arbitrary"), vmem_limit_bytes=VMEM_LIMIT),
        name="mod",
    )(c_pad, mod_w, mod_b.reshape(depth, 1, n))
    return out[:, :b]


def _split3(z):
    z1 = z.astype(BF16)
    r1 = z - z1.astype(F32)
    z2 = r1.astype(BF16)
    return z1, z2, (r1 - z2.astype(F32)).astype(BF16)


def _mm_split3_lhs_exact(a_exact, b):
    aa = a_exact.astype(BF16)
    b1, b2, b3 = _split3(b)
    return (jnp.dot(aa, b1, preferred_element_type=F32)
            + jnp.dot(aa, b2, preferred_element_type=F32)
            + jnp.dot(aa, b3, preferred_element_type=F32))


def _moba_penalty_steps(gate, own, out):
    lane = lax.broadcasted_iota(jnp.int32, gate.shape, 1)
    n = lane & (N_MOBA_BLOCKS - 1)
    past = n < own
    g = jnp.where(past, gate, -jnp.inf)
    rank = jnp.zeros(gate.shape, jnp.int32)
    for k in range(1, N_MOBA_BLOCKS):
        gk = jnp.where(n >= k, pltpu.roll(g, k, 1), pltpu.roll(g, AUG - N_MOBA_BLOCKS + k, 1))
        beats = (gk > g) | ((gk == g) & (n >= k))
        rank = rank + beats.astype(jnp.int32)
        yield
    keep = (past & (rank < MOBA_TOPK)) | (n == own)
    out[0] = jnp.where(keep, 0.0, PEN)


def _inproj_kernel(x_ref, g_ref, sh_ref, sc_ref, w_ref, wf_ref, bf_ref, sel_c_ref, const_c_ref,
                   sel_p_ref, qa_ref, ka_ref, vt_ref, carry_ref, km_ref):
    tm = x_ref.shape[1]
    si = pl.program_id(1)

    @pl.when(si == 0)
    def _():
        carry_ref[...] = jnp.zeros_like(carry_ref)
        km_ref[...] = jnp.zeros_like(km_ref)

    h = _norm_mod(x_ref[0], g_ref[...], sh_ref[0], sc_ref[0]).astype(BF16)
    width = PAIR + AUG

    def proj(lo, hi):
        return jnp.dot(h, w_ref[:, lo:hi], preferred_element_type=F32)

    def put(ref, pair, value):
        ref[0, :, pair * width:pair * width + PAIR] = value.astype(BF16)

    def put_bias(ref, pair, value):
        ref[0, :, pair * width + PAIR:(pair + 1) * width] = value.astype(BF16)

    q_mb = proj(3 * FOX_W, 4 * FOX_W)
    qb = q_mb.astype(BF16)
    km1, km2, km3 = _split3(km_ref[...])
    gate = (jnp.dot(qb, km1, preferred_element_type=F32)
            + jnp.dot(qb, km2, preferred_element_type=F32)
            + jnp.dot(qb, km3, preferred_element_type=F32))
    pen = [None]
    ranking = _moba_penalty_steps(gate, si, pen)

    def fox_bias():
        f = jnp.dot(h, wf_ref[...], preferred_element_type=F32) + bf_ref[...]
        lf = -_softplus(-f)
        row = lax.broadcasted_iota(jnp.int32, (tm, tm), 0)
        col = lax.broadcasted_iota(jnp.int32, (tm, tm), 1)
        tri = jnp.where(row >= col, 1.0, 0.0).astype(F32)
        cum = _mm_split3_lhs_exact(tri, lf) + carry_ref[...]
        carry_ref[...] = cum[tm - 1:tm, :]
        c3 = jnp.concatenate(_split3(cum * LOG2E), axis=1)
        aug = jnp.dot(c3, sel_c_ref[...], preferred_element_type=F32) + const_c_ref[...]
        for p in range(4):
            put_bias(qa_ref, p, aug[:, p * PAIR:(p + 1) * PAIR])
            put_bias(ka_ref, p, aug[:, FOX_W + p * PAIR:FOX_W + (p + 1) * PAIR])

    def fox_qk(half):
        q = proj(half * 2 * PAIR, (half + 1) * 2 * PAIR)
        k = proj(FOX_W + half * 2 * PAIR, FOX_W + (half + 1) * 2 * PAIR)
        for i in range(2):
            put(qa_ref, 2 * half + i, q[:, i * PAIR:(i + 1) * PAIR])
            put(ka_ref, 2 * half + i, k[:, i * PAIR:(i + 1) * PAIR])

    def values(group, half):
        lo = (2 + 3 * group) * FOX_W + half * 2 * PAIR
        v = proj(lo, lo + 2 * PAIR)
        first = 4 * group + 2 * half
        vt_ref[0, first:first + 2, 0] = v.T.reshape(2, PAIR, tm).astype(BF16)

    def moba_k():
        k_mb = proj(4 * FOX_W, 5 * FOX_W)
        lane_p = lax.broadcasted_iota(jnp.int32, (tm, AUG), 1)
        onehot = jnp.where((lane_p < 2 * AUG_PER_HEAD) & ((lane_p & (N_MOBA_BLOCKS - 1)) == si),
                           1.0, 0.0)
        for p in range(4):
            put(ka_ref, 4 + p, k_mb[:, p * PAIR:(p + 1) * PAIR])
            put_bias(ka_ref, 4 + p, onehot)
        kb_t = k_mb.astype(BF16).astype(F32).T
        place = jnp.where((lane_p & (N_MOBA_BLOCKS - 1)) == si, 1.0 / tm, 0.0)
        upd = jnp.dot(kb_t.astype(BF16), place.astype(BF16), preferred_element_type=F32)
        d_head = lax.broadcasted_iota(jnp.int32, (FOX_W, AUG), 0) >> 6
        l_head = lax.broadcasted_iota(jnp.int32, (FOX_W, AUG), 1) >> 4
        km_ref[...] += jnp.where(d_head == l_head, upd, 0.0)

    others = [fox_bias, lambda: fox_qk(0), lambda: fox_qk(1), moba_k,
              lambda: values(0, 0), lambda: values(0, 1), lambda: values(1, 0),
              lambda: values(1, 1)]
    for i, _ in enumerate(ranking):
        if i % 2 == 0 and others:
            others.pop(0)()
    for task in others:
        task()
    aug_mb_q = jnp.dot(pen[0].astype(BF16), sel_p_ref[...], preferred_element_type=F32)
    for p in range(4):
        put(qa_ref, 4 + p, q_mb[:, p * PAIR:(p + 1) * PAIR])
        put_bias(qa_ref, 4 + p, aug_mb_q[:, p * PAIR:(p + 1) * PAIR])


def _bias_lane_selectors():
    sel_c = np.zeros((3 * AUG, 2 * FOX_W), np.float32)
    const_c = np.zeros((1, 2 * FOX_W), np.float32)
    sel_p = np.zeros((AUG, FOX_W), np.float32)
    for head in range(8):
        base = (head // 2) * AUG + (head % 2) * AUG_PER_HEAD
        for i in range(3):
            sel_c[i * AUG + head, base + i] = 1.0
            sel_c[i * AUG + head, FOX_W + base + 3 + i] = -1.0
            const_c[0, base + 3 + i] = 1.0
            const_c[0, FOX_W + base + i] = 1.0
        for n in range(N_MOBA_BLOCKS):
            sel_p[head * N_MOBA_BLOCKS + n, base + n] = 1.0
    return jnp.asarray(sel_c, BF16), jnp.asarray(const_c, F32), jnp.asarray(sel_p, BF16)


def _inproj(x, g, sh, sc, w_main, w_f, b_f):
    b, s, _ = x.shape
    tm = MOBA_BLOCK
    n = w_main.shape[1]
    width = N_PAIR_ATTN * (PAIR + AUG)
    sel_c, const_c, sel_p = _bias_lane_selectors()
    row3 = lambda bi, si: (bi, 0, 0)
    const2 = lambda bi, si: (0, 0)
    tile3 = lambda bi, si: (bi, si, 0)
    full = lambda arr: pl.BlockSpec(arr.shape, const2)
    return pl.pallas_call(
        _inproj_kernel,
        grid=(b, s // tm),
        in_specs=[pl.BlockSpec((1, tm, D), tile3),
                  pl.BlockSpec((1, D), const2),
                  pl.BlockSpec((1, 1, D), row3),
                  pl.BlockSpec((1, 1, D), row3),
                  pl.BlockSpec((D, n), const2),
                  pl.BlockSpec((D, AUG), const2),
                  pl.BlockSpec((1, AUG), const2),
                  full(sel_c), full(const_c), full(sel_p)],
        out_specs=[pl.BlockSpec((1, tm, width), tile3),
                   pl.BlockSpec((1, tm, width), tile3),
                   pl.BlockSpec((1, N_PAIR_ATTN, 1, PAIR, tm), lambda bi, si: (bi, 0, si, 0, 0))],
        out_shape=[jax.ShapeDtypeStruct((b, s, width), BF16),
                   jax.ShapeDtypeStruct((b, s, width), BF16),
                   jax.ShapeDtypeStruct((b, N_PAIR_ATTN, s // tm, PAIR, tm), BF16)],
        scratch_shapes=[pltpu.VMEM((1, AUG), F32), pltpu.VMEM((FOX_W, AUG), F32)],
        compiler_params=pltpu.CompilerParams(
            dimension_semantics=("arbitrary", "arbitrary"), vmem_limit_bytes=VMEM_LIMIT),
        name="inproj",
    )(x, g, sh, sc, w_main, w_f, b_f, sel_c, const_c, sel_p)


def _attn_kernel(*refs):
    qa_refs = refs[:ATTN_SUBSTEPS]
    ka_ref, vt_ref, o_ref, s_even, s_odd, macc_ref, mprev_ref, acc_ref = refs[ATTN_SUBSTEPS:]
    tq = qa_refs[0].shape[1]
    tk = vt_ref.shape[4]
    unit = tq // tk
    nq = s_even.shape[1] // tq
    lane = lax.broadcasted_iota(jnp.int32, (1, PAIR + AUG), 1)
    head0 = (lane < HEAD) | ((lane >= PAIR) & (lane < PAIR + AUG_PER_HEAD))
    head1 = ((lane >= HEAD) & (lane < PAIR)) | (
        (lane >= PAIR + AUG_PER_HEAD) & (lane < PAIR + 2 * AUG_PER_HEAD))
    ones_rows = jnp.ones((SUM_ROWS, tk), BF16)
    for sub in range(ATTN_SUBSTEPS):
        j = ATTN_SUBSTEPS * (pl.program_id(2) - 1) + 1 + sub
        s_cur, s_prev = (s_odd, s_even) if sub % 2 == 0 else (s_even, s_odd)
        _attn_substep(j, qa_refs[sub][0], s_cur, s_prev, o_ref.at[0, sub * tq:(sub + 1) * tq, :],
                      ka_ref, vt_ref, macc_ref, mprev_ref, acc_ref, head0, head1, ones_rows,
                      tq, tk, unit, nq)


def _attn_substep(j, qa, s_cur, s_prev, o_ref, ka_ref, vt_ref, macc_ref, mprev_ref, acc_ref,
                  head0, head1, ones_rows, tq, tk, unit, nq):
    m_prev = mprev_ref[...]
    macc_ref[...] = jnp.full(macc_ref.shape, NEG, F32)
    acc_ref[...] = jnp.zeros_like(acc_ref)

    def score_tiles(s_ref, first_tile, n, diagonal):
        rows = n * tk
        start = pl.multiple_of(first_tile * tk, tk)
        k = ka_ref[0, pl.ds(start, rows), :]
        kz = jnp.zeros_like(k)
        k2 = jnp.concatenate([jnp.where(head0, k, kz), jnp.where(head1, k, kz)], axis=0)
        st = lax.dot_general(k2, qa, (((1,), (1,)), ((), ())), preferred_element_type=F32)
        full = rows - tq if diagonal else rows
        for h in range(2):
            macc = macc_ref[h]
            if full:
                part = st[h * rows:h * rows + full]
                s_ref[h, pl.ds(start, full), :] = part
                macc = jnp.maximum(macc, jnp.max(part.reshape(full // 8, 8, tq), axis=0))
            if diagonal:
                key_pos = lax.broadcasted_iota(jnp.int32, (tq, tq), 0)
                qry_pos = lax.broadcasted_iota(jnp.int32, (tq, tq), 1)
                part = jnp.where(key_pos <= qry_pos, st[h * rows + full:(h + 1) * rows], NEG)
                s_ref[h, pl.ds(start + full, tq), :] = part
                macc = jnp.maximum(macc, jnp.max(part.reshape(tq // 8, 8, tq), axis=0))
            macc_ref[h] = macc

    def pv_tile(s_ref, t):
        start = pl.multiple_of(t * tk, tk)
        vt = vt_ref[0, 0, t]
        accs = []
        for h in range(2):
            p = jnp.exp2(s_ref[h, pl.ds(start, tk), :] - m_prev[h])
            lhs = jnp.concatenate([vt[h * HEAD:(h + 1) * HEAD, :], ones_rows], axis=0)
            accs.append(jnp.dot(lhs, p.astype(BF16), preferred_element_type=F32))
        return jnp.stack(accs)

    def pv_tiles(s_ref, first_tile, n):
        parts = [pv_tile(s_ref, first_tile + t) for t in range(n)]
        while len(parts) > 1:
            parts = [a + b for a, b in zip(parts[0::2], parts[1::2])]
        acc_ref[...] += parts[0]

    bit = nq // 2
    while bit >= 1:
        in_group = (j > 0) & (j < nq) & ((j & bit) != 0)
        lowest = (j & (bit - 1)) == 0
        first = (j & (-2 * bit)) * unit
        n = bit * unit

        if bit > 1:
            @pl.when(in_group & jnp.logical_not(lowest))
            def _(n=n, first=first):
                score_tiles(s_cur, first, n, False)
                pv_tiles(s_prev, first, n)

        @pl.when(in_group & lowest)
        def _(n=n, first=first):
            score_tiles(s_cur, first, n + unit, True)
            pv_tiles(s_prev, first, n)
        bit //= 2

    @pl.when(j == 0)
    def _():
        score_tiles(s_cur, 0, unit, True)

    @pl.when(j == nq)
    def _():
        for first in range(0, nq * unit, 8):
            pv_tiles(s_prev, first, 8)

    mprev_ref[...] = jnp.max(macc_ref[...], axis=1, keepdims=True)

    @pl.when(j > 0)
    def _():
        acc = acc_ref[...]
        ot = jnp.concatenate([acc[h, :HEAD] / acc[h, HEAD:HEAD + 1] for h in range(2)], axis=0)
        o_ref[...] = ot.T.astype(o_ref.dtype)


def _attention(qa, ka, vt):
    b, s, _ = qa.shape
    tq = ATTN_TQ
    nk, tk = vt.shape[2], vt.shape[4]
    nq = s // tq
    assert nq & (nq - 1) == 0 and (nq * tq // tk) % 8 == 0, "query tiles must be a power of two"
    n = ATTN_SUBSTEPS
    assert n % 2 == 0 and nq % n == 0

    def qa_spec(sub):
        tile = lambda t: jnp.clip(n * (t - 1) + 1 + sub, 0, nq - 1)
        return pl.BlockSpec((1, tq, PAIR + AUG), lambda bi, p, t: (bi, tile(t), p))

    return pl.pallas_call(
        _attn_kernel,
        grid=(b, N_PAIR_ATTN, nq // n + 1),
        in_specs=[qa_spec(sub) for sub in range(n)] + [
            pl.BlockSpec((1, s, PAIR + AUG), lambda bi, p, t: (bi, 0, p)),
            pl.BlockSpec((1, 1, nk, PAIR, tk), lambda bi, p, t: (bi, p, 0, 0, 0))],
        out_specs=pl.BlockSpec((1, n * tq, PAIR), lambda bi, p, t: (bi, jnp.maximum(t - 1, 0), p)),
        out_shape=jax.ShapeDtypeStruct((b, s, N_PAIR_ATTN * PAIR), BF16),
        scratch_shapes=[pltpu.VMEM((2, s, tq), F32),
                        pltpu.VMEM((2, s, tq), F32),
                        pltpu.VMEM((2, 8, tq), F32),
                        pltpu.VMEM((2, 1, tq), F32),
                        pltpu.VMEM((2, HEAD + SUM_ROWS, tq), F32)],
        compiler_params=pltpu.CompilerParams(
            dimension_semantics=("arbitrary", "arbitrary", "arbitrary"),
            vmem_limit_bytes=VMEM_LIMIT),
        name="attn",
    )(*([qa] * n), ka, vt)


def _post_kernel(x_ref, a_ref, wo_ref, gtm_ref, g_ref, sh_ref, sc_ref, gtf_ref,
                 win_ref, wd_ref, gfin_ref, o_ref, *, final_norm):
    x1 = x_ref[0] + gtm_ref[0] * jnp.dot(a_ref[0], wo_ref[...], preferred_element_type=F32)
    h = _norm_mod(x1, g_ref[...], sh_ref[0], sc_ref[0]).astype(BF16)
    ff = jnp.zeros_like(x1)
    for lo, hi in zip(FF_SLAB_EDGES[:-1], FF_SLAB_EDGES[1:]):
        gate = jnp.dot(h, win_ref[0, :, lo:hi], preferred_element_type=F32)
        up = jnp.dot(h, win_ref[0, :, D_FF + lo:D_FF + hi], preferred_element_type=F32)
        act = (gate * _sigmoid(gate) * up).astype(BF16)
        ff = ff + jnp.dot(act, wd_ref[0, lo:hi, :], preferred_element_type=F32)
    x2 = x1 + gtf_ref[0] * ff
    if final_norm:
        x2 = x2 * lax.rsqrt(jnp.mean(x2 * x2, axis=-1, keepdims=True) + NORM_EPS) * gfin_ref[...]
    o_ref[0] = x2


def _post(x, a, w_o, gt_m, g_ffn, sh_f, sc_f, gt_f, w_in, w_down, layer, g_final, final_norm):
    b, s, _ = x.shape
    tm = 512
    row3 = lambda bi, si: (bi, 0, 0)
    const2 = lambda bi, si: (0, 0)
    tile3 = lambda bi, si: (bi, si, 0)
    of_layer = lambda bi, si: (layer, 0, 0)
    once = pl.Buffered(1)
    kern = functools.partial(_post_kernel, final_norm=final_norm)
    return pl.pallas_call(
        kern,
        grid=(b, s // tm),
        in_specs=[pl.BlockSpec((1, tm, D), tile3),
                  pl.BlockSpec((1, tm, D), tile3),
                  pl.BlockSpec((D, D), const2),
                  pl.BlockSpec((1, 1, D), row3),
                  pl.BlockSpec((1, D), const2),
                  pl.BlockSpec((1, 1, D), row3),
                  pl.BlockSpec((1, 1, D), row3),
                  pl.BlockSpec((1, 1, D), row3),
                  pl.BlockSpec((1, D, 2 * D_FF), of_layer, pipeline_mode=once),
                  pl.BlockSpec((1, D_FF, D), of_layer, pipeline_mode=once),
                  pl.BlockSpec((1, D), const2)],
        out_specs=pl.BlockSpec((1, tm, D), tile3),
        out_shape=jax.ShapeDtypeStruct((b, s, D), F32),
        compiler_params=pltpu.CompilerParams(
            dimension_semantics=("arbitrary", "arbitrary"), vmem_limit_bytes=VMEM_LIMIT),
        name="post_final" if final_norm else "post",
    )(x, a, w_o, gt_m, g_ffn, sh_f, sc_f, gt_f, w_in, w_down, g_final)


def _rwkvproj_kernel(x_ref, g_ref, sh_ref, sc_ref, mu_ref, wr_ref, wk_ref, wv_ref,
                     w1_ref, w2_ref, a1_ref, a2_ref, g1_ref, g2_ref,
                     w0_ref, a0_ref, kk_ref, ka_ref,
                     r_ref, k_ref, v_ref, kkv_ref, a_ref, lw_ref, gate_ref, carry_ref):
    tm = x_ref.shape[1]

    @pl.when(pl.program_id(1) == 0)
    def _():
        carry_ref[...] = jnp.zeros_like(carry_ref)

    h = _norm_mod(x_ref[0], g_ref[...], sh_ref[0], sc_ref[0])
    rowi = lax.broadcasted_iota(jnp.int32, (tm, 1), 0)
    h_prev = jnp.where(rowi == 0, carry_ref[...], pltpu.roll(h, 1, 0))
    carry_ref[...] = h[tm - 1:tm, :]
    xx = h_prev - h
    mix = lambda i: (h + xx * mu_ref[i:i + 1, :]).astype(BF16)
    cw = RWKV_PROJ_COLS
    slabs = [slice(c * cw, (c + 1) * cw) for c in range(D // cw)]

    def project(lhs, w_ref, out_ref, which):
        for cols in which:
            out_ref[0, :, cols] = jnp.dot(lhs, w_ref[:, cols],
                                          preferred_element_type=F32).astype(out_ref.dtype)

    half0, half1 = slabs[:len(slabs) // 2], slabs[len(slabs) // 2:]
    xr = mix(0)
    xv = mix(3)
    project(xr, wr_ref, r_ref, half0)
    hid_g = _sigmoid(_mm(mix(5), g1_ref[...])).astype(BF16)
    project(xr, wr_ref, r_ref, half1)
    hid_w = jnp.tanh(_mm(mix(1), w1_ref[...])).astype(BF16)
    project(xv, wv_ref, v_ref, half0)
    hid_a = _mm(mix(4), a1_ref[...]).astype(BF16)
    project(xv, wv_ref, v_ref, half1)
    xk = mix(2)
    project(hid_g, g2_ref, gate_ref, slabs)
    for cols in slabs:
        k = jnp.dot(xk, wk_ref[:, cols], preferred_element_type=F32)
        a = _sigmoid(a0_ref[:, cols] + jnp.dot(hid_a, a2_ref[:, cols], preferred_element_type=F32))
        wl = w0_ref[:, cols] + jnp.dot(hid_w, w2_ref[:, cols], preferred_element_type=F32)
        k_ref[0, :, cols] = (k * (1.0 + (a - 1.0) * ka_ref[:, cols])).astype(k_ref.dtype)
        kkv_ref[0, :, cols] = k * kk_ref[:, cols]
        a_ref[0, :, cols] = a
        lw_ref[0, :, cols] = -math.exp(-0.5) * _sigmoid(wl)


def _rwkv_proj(x, g, sh, sc, mu, w_r, w_k, w_v, w1, w2, a1, a2, g1, g2, w0, a0, k_k, k_a):
    b, s, _ = x.shape
    tm = 512
    row3 = lambda bi, si: (bi, 0, 0)
    const2 = lambda bi, si: (0, 0)
    tile3 = lambda bi, si: (bi, si, 0)
    full = lambda arr: pl.BlockSpec(arr.shape, const2)
    outs = pl.pallas_call(
        _rwkvproj_kernel,
        grid=(b, s // tm),
        in_specs=[pl.BlockSpec((1, tm, D), tile3), full(g),
                  pl.BlockSpec((1, 1, D), row3), pl.BlockSpec((1, 1, D), row3),
                  full(mu), full(w_r), full(w_k), full(w_v),
                  full(w1), full(w2), full(a1), full(a2), full(g1), full(g2),
                  full(w0), full(a0), full(k_k), full(k_a)],
        out_specs=[pl.BlockSpec((1, tm, D), tile3)] * 7,
        out_shape=[jax.ShapeDtypeStruct((b, s, D), dt) for dt in (BF16, BF16, BF16, F32, F32, F32, BF16)],
        scratch_shapes=[pltpu.VMEM((1, D), F32)],
        compiler_params=pltpu.CompilerParams(
            dimension_semantics=("arbitrary", "arbitrary"), vmem_limit_bytes=VMEM_LIMIT),
        name="rwkv_proj",
    )(x, g, sh, sc, mu, w_r, w_k, w_v, w1, w2, a1, a2, g1, g2, w0, a0, k_k, k_a)
    return outs


def _scan_kernel(r_ref, k_ref, v_ref, kkv_ref, a_ref, lw_ref, gate_ref,
                 rk_ref, lnw_ref, lnb_ref, o_ref, state_ref):
    tc = r_ref.shape[1]
    group_rows = SCAN_GROUP * CHUNK
    n_groups = tc // group_rows

    @pl.when(pl.program_id(2) == 0)
    def _():
        state_ref[...] = jnp.zeros_like(state_ref)

    li = lax.broadcasted_iota(jnp.int32, (PAIR, PAIR), 0)
    lj = lax.broadcasted_iota(jnp.int32, (PAIR, PAIR), 1)
    ones_bd = jnp.where((li >> 6) == (lj >> 6), 1.0, 0.0).astype(BF16)
    strict = (li & (CHUNK - 1)) > (lj & (CHUNK - 1))
    incl = (li & (CHUNK - 1)) >= (lj & (CHUNK - 1))
    eye = jnp.where(li == lj, 1.0, 0.0).astype(F32)
    tri_rows = 4 * CHUNK
    ti = lax.broadcasted_iota(jnp.int32, (tri_rows, tri_rows), 0)
    tj = lax.broadcasted_iota(jnp.int32, (tri_rows, tri_rows), 1)
    tri = jnp.where((ti >= tj) & ((ti >> 6) == (tj >> 6)), 1.0, 0.0).astype(BF16)
    is_h0 = lax.broadcasted_iota(jnp.int32, (1, PAIR), 1) < HEAD
    chunks = range(SCAN_GROUP)

    def stack(z):
        return jnp.concatenate([jnp.where(is_h0, z, 0.0), jnp.where(is_h0, 0.0, z)], axis=0)

    def state_free_part(g, out):
        rows = slice(g * group_rows, (g + 1) * group_rows)
        r = r_ref[0, rows, :].astype(F32)
        k = k_ref[0, rows, :].astype(F32)
        v = v_ref[0, rows, :].astype(F32)
        kkv = kkv_ref[0, rows, :]
        lw = lw_ref[0, rows, :]
        lw1 = lw.astype(BF16)
        lw2 = (lw - lw1.astype(F32)).astype(BF16)
        cl = jnp.concatenate(
            [jnp.dot(tri, lw1[i:i + tri_rows], preferred_element_type=F32)
             + jnp.dot(tri, lw2[i:i + tri_rows], preferred_element_type=F32)
             for i in range(0, group_rows, tri_rows)], axis=0)
        n2 = _mm(kkv * kkv, ones_bd)
        kk = kkv / jnp.maximum(jnp.sqrt(n2), 1e-12)
        bvec = kk * a_ref[0, rows, :]
        egi = jnp.exp(-cl)
        a_t = -kk * jnp.exp(cl - lw)
        b_t = bvec * egi
        k_t = k * egi
        r_t = r * jnp.exp(cl)
        out["bonus"] = _mm(r * k * rk_ref[...], ones_bd) * v
        sls = [slice(c * CHUNK, (c + 1) * CHUNK) for c in chunks]
        cl_end = [cl[(c + 1) * CHUNK - 1:(c + 1) * CHUNK, :] for c in chunks]
        out["decay"] = [jnp.exp(ce) for ce in cl_end]
        tails = [jnp.exp(cl_end[c] - cl[sls[c]]) for c in chunks]
        a_s = [stack(a_t[sl]) for sl in sls]
        r_s = [stack(r_t[sl]) for sl in sls]
        v_s = [stack(v[sl]) for sl in sls]
        bh_s = [stack(bvec[sls[c]] * tails[c]) for c in chunks]
        kh_s = [stack(k[sls[c]] * tails[c]) for c in chunks]
        grams = [_mm_nt(jnp.concatenate([a_s[c], r_s[c]], axis=0),
                        jnp.concatenate([stack(b_t[sls[c]]), stack(k_t[sls[c]])], axis=0))
                 for c in chunks]
        yield
        m_ak = [jnp.where(strict, gm[:PAIR, PAIR:], 0.0) for gm in grams]
        a_r = [jnp.concatenate([jnp.where(incl, gm[PAIR:, :PAIR], 0.0),
                                jnp.where(incl, gm[PAIR:, PAIR:], 0.0)], axis=1) for gm in grams]
        rhs = [jnp.concatenate([a_s[c], _mm(m_ak[c], v_s[c])], axis=1) for c in chunks]
        mps = [jnp.where(strict, gm[:PAIR, :PAIR], 0.0) for gm in grams]
        ts = [eye + mp for mp in mps]
        mps = [_mm(mp, mp) for mp in mps]
        yield
        for level in range(1, 6):
            if level < 5:
                res = [_mm(mps[c], jnp.concatenate([mps[c], ts[c]], axis=1)) for c in chunks]
                mps = [rs[:, :PAIR] for rs in res]
                ts = [ts[c] + res[c][:, PAIR:] for c in chunks]
            else:
                ts = [ts[c] + _mm(mps[c], ts[c]) for c in chunks]
            yield
        xs = [_mm(ts[c], rhs[c]) for c in chunks]
        gys = [_mm(a_r[c], jnp.concatenate(
            [xs[c], jnp.concatenate([jnp.zeros_like(v_s[c]), v_s[c]], axis=1)], axis=0))
               for c in chunks]
        out["g"] = [r_s[c] + gys[c][:, :PAIR] for c in chunks]
        out["y0"] = [gy[:, PAIR:] for gy in gys]
        yield
        out["pb"] = [_mm_tn(xs[c][:, :PAIR], bh_s[c]) for c in chunks]
        out["q"] = [_mm_tn(jnp.concatenate([xs[c][:, PAIR:], v_s[c]], axis=0),
                           jnp.concatenate([bh_s[c], kh_s[c]], axis=0)) for c in chunks]
        yield

    carried_state = [state_ref[...]]

    def state_part(g, res):
        ys = []
        for c in chunks:
            state = carried_state[0]
            y_s = _mm_nt(res["g"][c], state) + res["y0"][c]
            ys.append(y_s[:CHUNK] + y_s[CHUNK:])
            carried_state[0] = state * res["decay"][c] + _mm(state, res["pb"][c]) + res["q"][c]
            yield
        rows = slice(g * group_rows, (g + 1) * group_rows)
        y = jnp.concatenate(ys, axis=0)
        inv_n = 1.0 / HEAD
        mean = _mm(y, ones_bd) * inv_n
        yc = y - mean
        var = _mm(yc * yc, ones_bd) * inv_n
        yn = yc * lax.rsqrt(var + GN_EPS) * lnw_ref[...] + lnb_ref[...]
        o_ref[0, rows, :] = ((yn + res["bonus"]) * gate_ref[0, rows, :].astype(F32)
                             ).astype(o_ref.dtype)
        yield

    results = [dict() for _ in range(n_groups)]
    for _ in state_free_part(0, results[0]):
        pass
    for g in range(n_groups):
        carried = state_part(g, results[g])
        ahead = state_free_part(g + 1, results[g + 1]) if g + 1 < n_groups else iter(())
        n_carried, n_ahead = SCAN_GROUP + 1, 9
        for i, _ in enumerate(ahead):
            if (i * n_carried) // n_ahead != ((i + 1) * n_carried) // n_ahead:
                next(carried, None)
        for _ in carried:
            pass
    state_ref[...] = carried_state[0]


def _rwkv_scan(r, k, v, kkv, a, lw, gate, r_k, ln_w, ln_b):
    b, s, _ = r.shape
    tc = SCAN_BLOCK
    tile = pl.BlockSpec((1, tc, PAIR), lambda bi, p, ti: (bi, ti, p))
    vec = pl.BlockSpec((1, PAIR), lambda bi, p, ti: (0, p))
    return pl.pallas_call(
        _scan_kernel,
        grid=(b, D // PAIR, s // tc),
        in_specs=[tile] * 7 + [vec] * 3,
        out_specs=tile,
        out_shape=jax.ShapeDtypeStruct((b, s, D), BF16),
        scratch_shapes=[pltpu.VMEM((PAIR, PAIR), F32)],
        compiler_params=pltpu.CompilerParams(
            dimension_semantics=("arbitrary", "arbitrary", "arbitrary"),
            vmem_limit_bytes=VMEM_LIMIT),
        name="rwkv_scan",
    )(r, k, v, kkv, a, lw, gate, r_k, ln_w, ln_b)


def kernel(x, c, attn_w_in, attn_b_f, attn_w_out, rwkv_mu, rwkv_w_r, rwkv_w_k, rwkv_w_v, rwkv_w_o, rwkv_w0, rwkv_w1, rwkv_w2, rwkv_a0, rwkv_a1, rwkv_a2, rwkv_g1, rwkv_g2, rwkv_k_k, rwkv_k_a, rwkv_r_k, rwkv_ln_w, rwkv_ln_b, mod_w, mod_b, norm_mix, norm_ffn, ffn_w_in, ffn_w_out, norm_final):
    b, s, _ = x.shape
    mod = _modulation(c, mod_w, mod_b)
    mods = [[mod[l, :, i * D:(i + 1) * D].reshape(b, 1, D) for i in range(6)] for l in range(2)]
    row = lambda t: t.reshape(1, D)
    g_final = row(norm_final)

    sh_m, sc_m, gt_m, sh_f, sc_f, gt_f = mods[0]
    w_in = attn_w_in[0]
    qscale = HEAD ** -0.5 * LOG2E
    col_scale = jnp.ones((3 * D,), F32)
    col_scale = col_scale.at[0:FOX_W].set(qscale).at[3 * FOX_W:4 * FOX_W].set(qscale)
    w_main = (w_in[:, :3 * D] * col_scale[None, :]).astype(BF16)
    w_f = jnp.zeros((D, AUG), F32).at[:, :8].set(w_in[:, 3 * D:]).astype(BF16)
    b_f = jnp.zeros((1, AUG), F32).at[0, :8].set(attn_b_f[0])
    qa, ka, vt = _inproj(x, row(norm_mix[0]), sh_m, sc_m, w_main, w_f, b_f)
    o = _attention(qa, ka, vt)

    w_ffn_in = ffn_w_in.astype(BF16)
    w_ffn_out = ffn_w_out.astype(BF16)
    x = _post(x, o, attn_w_out[0].astype(BF16), gt_m, row(norm_ffn[0]), sh_f, sc_f, gt_f,
              w_ffn_in, w_ffn_out, 0, g_final, final_norm=False)

    sh_m, sc_m, gt_m, sh_f, sc_f, gt_f = mods[1]
    pad_in = lambda w, n: jnp.zeros((D, n), BF16).at[:, :w.shape[1]].set(w.astype(BF16))
    pad_out = lambda w, n: jnp.zeros((n, D), BF16).at[:w.shape[0], :].set(w.astype(BF16))
    r, k, v, kkv, a, lw, gate = _rwkv_proj(
        x, row(norm_mix[1]), sh_m, sc_m, rwkv_mu[0],
        rwkv_w_r[0].astype(BF16), rwkv_w_k[0].astype(BF16), rwkv_w_v[0].astype(BF16),
        pad_in(rwkv_w1[0], 128), pad_out(rwkv_w2[0], 128),
        pad_in(rwkv_a1[0], 128), pad_out(rwkv_a2[0], 128),
        pad_in(rwkv_g1[0], 256), pad_out(rwkv_g2[0], 256),
        row(rwkv_w0[0]), row(rwkv_a0[0]), row(rwkv_k_k[0]), row(rwkv_k_a[0]))
    y = _rwkv_scan(r, k, v, kkv, a, lw, gate, row(rwkv_r_k[0]), row(rwkv_ln_w[0]),
                   row(rwkv_ln_b[0]))
    x = _post(x, y, rwkv_w_o[0].astype(BF16), gt_m, row(norm_ffn[1]), sh_f, sc_f, gt_f,
              w_ffn_in, w_ffn_out, 1, g_final, final_norm=True)
    return x
```

```python
import functools
import math

import jax
import jax.numpy as jnp
import numpy as np
from jax import lax
from jax.experimental import pallas as pl
from jax.experimental.pallas import tpu as pltpu

F32 = jnp.float32
BF16 = jnp.bfloat16

D = 1024
HEAD = 64
PAIR = 2 * HEAD
N_PAIR_ATTN = 8
FOX_W = 512
D_FF = 2816
MXU_WIDTH = 256
FF_SLAB_EDGES = (0, 5 * MXU_WIDTH, D_FF)
MOBA_BLOCK = 256
N_MOBA_BLOCKS = 16
MOBA_TOPK = 3
AUG = 128
AUG_PER_HEAD = 16
NORM_EPS = 1e-6
GN_EPS = HEAD * 1e-5
LOG2E = 1.4426950408889634
NEG = -1e30
PEN = -30000.0
CHUNK = 64
SCAN_GROUP = 8
SCAN_BLOCK = 8 * SCAN_GROUP * CHUNK
SUM_ROWS = 16
ATTN_SUBSTEPS = 4
ATTN_TQ = 256
RWKV_PROJ_COLS = 256
VMEM_LIMIT = 56 * 1024 * 1024


def _mm(a, b):
    return jnp.dot(a.astype(BF16), b.astype(BF16), preferred_element_type=F32)


def _mm_nt(a, b):
    return lax.dot_general(a.astype(BF16), b.astype(BF16), (((1,), (1,)), ((), ())),
                           preferred_element_type=F32)


def _mm_tn(a, b):
    return lax.dot_general(a.astype(BF16), b.astype(BF16), (((0,), (0,)), ((), ())),
                           preferred_element_type=F32)


def _sigmoid(x):
    return 1.0 / (1.0 + jnp.exp(-x))


def _softplus(x):
    return jnp.maximum(x, 0.0) + jnp.log(1.0 + jnp.exp(-jnp.abs(x)))


def _norm_mod(x, g, sh, sc):
    y = x * lax.rsqrt(jnp.mean(x * x, axis=-1, keepdims=True) + NORM_EPS)
    return (y * g) * (1.0 + sc) + sh


def _mod_kernel(c_ref, w_ref, b_ref, o_ref):
    c = c_ref[...]
    c1, c2, c3 = _split3(c * _sigmoid(c))
    w = w_ref[0]
    w1 = w.astype(BF16)
    w2 = (w - w1.astype(F32)).astype(BF16)
    dot = lambda a, b: jnp.dot(a, b, preferred_element_type=F32)
    o_ref[0] = (dot(c1, w1) + (dot(c2, w1) + dot(c1, w2))
                + (dot(c3, w1) + dot(c2, w2))) + b_ref[0]


def _modulation(c, mod_w, mod_b):
    depth, _, n = mod_w.shape
    b = c.shape[0]
    rows = 16
    c_pad = jnp.zeros((rows, D), F32).at[:b].set(c)
    tn = 1536
    out = pl.pallas_call(
        _mod_kernel,
        grid=(depth, n // tn),
        in_specs=[pl.BlockSpec((rows, D), lambda l, j: (0, 0)),
                  pl.BlockSpec((1, D, tn), lambda l, j: (l, 0, j)),
                  pl.BlockSpec((1, 1, tn), lambda l, j: (l, 0, j))],
        out_specs=pl.BlockSpec((1, rows, tn), lambda l, j: (l, 0, j)),
        out_shape=jax.ShapeDtypeStruct((depth, rows, n), F32),
        compiler_params=pltpu.CompilerParams(
            dimension_semantics=("arbitrary", "arbitrary"), vmem_limit_bytes=VMEM_LIMIT),
        name="mod",
    )(c_pad, mod_w, mod_b.reshape(depth, 1, n))
    return out[:, :b]


def _split3(z):
    z1 = z.astype(BF16)
    r1 = z - z1.astype(F32)
    z2 = r1.astype(BF16)
    return z1, z2, (r1 - z2.astype(F32)).astype(BF16)


def _mm_split3_lhs_exact(a_exact, b):
    aa = a_exact.astype(BF16)
    b1, b2, b3 = _split3(b)
    return (jnp.dot(aa, b1, preferred_element_type=F32)
            + jnp.dot(aa, b2, preferred_element_type=F32)
            + jnp.dot(aa, b3, preferred_element_type=F32))


def _moba_penalty_steps(gate, own, out):
    lane = lax.broadcasted_iota(jnp.int32, gate.shape, 1)
    n = lane & (N_MOBA_BLOCKS - 1)
    past = n < own
    g = jnp.where(past, gate, -jnp.inf)
    rank = jnp.zeros(gate.shape, jnp.int32)
    for k in range(1, N_MOBA_BLOCKS):
        gk = jnp.where(n >= k, pltpu.roll(g, k, 1), pltpu.roll(g, AUG - N_MOBA_BLOCKS + k, 1))
        beats = (gk > g) | ((gk == g) & (n >= k))
        rank = rank + beats.astype(jnp.int32)
        yield
    keep = (past & (rank < MOBA_TOPK)) | (n == own)
    out[0] = jnp.where(keep, 0.0, PEN)


def _inproj_kernel(x_ref, g_ref, sh_ref, sc_ref, w_ref, wf_ref, bf_ref, sel_c_ref, const_c_ref,
                   sel_p_ref, qa_ref, ka_ref, vt_ref, carry_ref, km_ref):
    tm = x_ref.shape[1]
    si = pl.program_id(1)

    @pl.when(si == 0)
    def _():
        carry_ref[...] = jnp.zeros_like(carry_ref)
        km_ref[...] = jnp.zeros_like(km_ref)

    h = _norm_mod(x_ref[0], g_ref[...], sh_ref[0], sc_ref[0]).astype(BF16)
    width = PAIR + AUG

    def proj(lo, hi):
        return jnp.dot(h, w_ref[:, lo:hi], preferred_element_type=F32)

    def put(ref, pair, value):
        if ref is qa_ref:
            ref[0, pair, 0:PAIR, :] = value.T.astype(BF16)
        else:
            ref[0, :, pair * width:pair * width + PAIR] = value.astype(BF16)

    def put_bias(ref, pair, value):
        if ref is qa_ref:
            ref[0, pair, PAIR:width, :] = value.T.astype(BF16)
        else:
            ref[0, :, pair * width + PAIR:(pair + 1) * width] = value.astype(BF16)

    q_mb = proj(3 * FOX_W, 4 * FOX_W)
    qb = q_mb.astype(BF16)
    km1, km2, km3 = _split3(km_ref[...])
    gate = (jnp.dot(qb, km1, preferred_element_type=F32)
            + jnp.dot(qb, km2, preferred_element_type=F32)
            + jnp.dot(qb, km3, preferred_element_type=F32))
    pen = [None]
    ranking = _moba_penalty_steps(gate, si, pen)

    def fox_bias():
        f = jnp.dot(h, wf_ref[...], preferred_element_type=F32) + bf_ref[...]
        lf = -_softplus(-f)
        row = lax.broadcasted_iota(jnp.int32, (tm, tm), 0)
        col = lax.broadcasted_iota(jnp.int32, (tm, tm), 1)
        tri = jnp.where(row >= col, 1.0, 0.0).astype(F32)
        cum = _mm_split3_lhs_exact(tri, lf) + carry_ref[...]
        carry_ref[...] = cum[tm - 1:tm, :]
        c3 = jnp.concatenate(_split3(cum * LOG2E), axis=1)
        aug = jnp.dot(c3, sel_c_ref[...], preferred_element_type=F32) + const_c_ref[...]
        for p in range(4):
            put_bias(qa_ref, p, aug[:, p * PAIR:(p + 1) * PAIR])
            put_bias(ka_ref, p, aug[:, FOX_W + p * PAIR:FOX_W + (p + 1) * PAIR])

    def fox_qk(half):
        q = proj(half * 2 * PAIR, (half + 1) * 2 * PAIR)
        k = proj(FOX_W + half * 2 * PAIR, FOX_W + (half + 1) * 2 * PAIR)
        for i in range(2):
            put(qa_ref, 2 * half + i, q[:, i * PAIR:(i + 1) * PAIR])
            put(ka_ref, 2 * half + i, k[:, i * PAIR:(i + 1) * PAIR])

    def values(group, half):
        lo = (2 + 3 * group) * FOX_W + half * 2 * PAIR
        v = proj(lo, lo + 2 * PAIR)
        first = 4 * group + 2 * half
        vt_ref[0, first:first + 2, 0] = v.T.reshape(2, PAIR, tm).astype(BF16)

    def moba_k():
        k_mb = proj(4 * FOX_W, 5 * FOX_W)
        lane_p = lax.broadcasted_iota(jnp.int32, (tm, AUG), 1)
        onehot = jnp.where((lane_p < 2 * AUG_PER_HEAD) & ((lane_p & (N_MOBA_BLOCKS - 1)) == si),
                           1.0, 0.0)
        for p in range(4):
            put(ka_ref, 4 + p, k_mb[:, p * PAIR:(p + 1) * PAIR])
            put_bias(ka_ref, 4 + p, onehot)
        kb_t = k_mb.astype(BF16).astype(F32).T
        place = jnp.where((lane_p & (N_MOBA_BLOCKS - 1)) == si, 1.0 / tm, 0.0)
        upd = jnp.dot(kb_t.astype(BF16), place.astype(BF16), preferred_element_type=F32)
        d_head = lax.broadcasted_iota(jnp.int32, (FOX_W, AUG), 0) >> 6
        l_head = lax.broadcasted_iota(jnp.int32, (FOX_W, AUG), 1) >> 4
        km_ref[...] += jnp.where(d_head == l_head, upd, 0.0)

    others = [fox_bias, lambda: fox_qk(0), lambda: fox_qk(1), moba_k,
              lambda: values(0, 0), lambda: values(0, 1), lambda: values(1, 0),
              lambda: values(1, 1)]
    for i, _ in enumerate(ranking):
        if i % 2 == 0 and others:
            others.pop(0)()
    for task in others:
        task()
    aug_mb_q = jnp.dot(pen[0].astype(BF16), sel_p_ref[...], preferred_element_type=F32)
    for p in range(4):
        put(qa_ref, 4 + p, q_mb[:, p * PAIR:(p + 1) * PAIR])
        put_bias(qa_ref, 4 + p, aug_mb_q[:, p * PAIR:(p + 1) * PAIR])


def _bias_lane_selectors():
    sel_c = np.zeros((3 * AUG, 2 * FOX_W), np.float32)
    const_c = np.zeros((1, 2 * FOX_W), np.float32)
    sel_p = np.zeros((AUG, FOX_W), np.float32)
    for head in range(8):
        base = (head // 2) * AUG + (head % 2) * AUG_PER_HEAD
        for i in range(3):
            sel_c[i * AUG + head, base + i] = 1.0
            sel_c[i * AUG + head, FOX_W + base + 3 + i] = -1.0
            const_c[0, base + 3 + i] = 1.0
            const_c[0, FOX_W + base + i] = 1.0
        for n in range(N_MOBA_BLOCKS):
            sel_p[head * N_MOBA_BLOCKS + n, base + n] = 1.0
    return jnp.asarray(sel_c, BF16), jnp.asarray(const_c, F32), jnp.asarray(sel_p, BF16)


def _inproj(x, g, sh, sc, w_main, w_f, b_f):
    b, s, _ = x.shape
    tm = MOBA_BLOCK
    n = w_main.shape[1]
    width = N_PAIR_ATTN * (PAIR + AUG)
    sel_c, const_c, sel_p = _bias_lane_selectors()
    row3 = lambda bi, si: (bi, 0, 0)
    const2 = lambda bi, si: (0, 0)
    tile3 = lambda bi, si: (bi, si, 0)
    full = lambda arr: pl.BlockSpec(arr.shape, const2)
    return pl.pallas_call(
        _inproj_kernel,
        grid=(b, s // tm),
        in_specs=[pl.BlockSpec((1, tm, D), tile3),
                  pl.BlockSpec((1, D), const2),
                  pl.BlockSpec((1, 1, D), row3),
                  pl.BlockSpec((1, 1, D), row3),
                  pl.BlockSpec((D, n), const2),
                  pl.BlockSpec((D, AUG), const2),
                  pl.BlockSpec((1, AUG), const2),
                  full(sel_c), full(const_c), full(sel_p)],
        out_specs=[pl.BlockSpec((1, N_PAIR_ATTN, PAIR + AUG, tm), lambda bi, si: (bi, 0, 0, si)),
                   pl.BlockSpec((1, tm, width), tile3),
                   pl.BlockSpec((1, N_PAIR_ATTN, 1, PAIR, tm), lambda bi, si: (bi, 0, si, 0, 0))],
        out_shape=[jax.ShapeDtypeStruct((b, N_PAIR_ATTN, PAIR + AUG, s), BF16),
                   jax.ShapeDtypeStruct((b, s, width), BF16),
                   jax.ShapeDtypeStruct((b, N_PAIR_ATTN, s // tm, PAIR, tm), BF16)],
        scratch_shapes=[pltpu.VMEM((1, AUG), F32), pltpu.VMEM((FOX_W, AUG), F32)],
        compiler_params=pltpu.CompilerParams(
            dimension_semantics=("arbitrary", "arbitrary"), vmem_limit_bytes=VMEM_LIMIT),
        name="inproj",
    )(x, g, sh, sc, w_main, w_f, b_f, sel_c, const_c, sel_p)


def _attn_kernel(*refs):
    qa_refs = refs[:ATTN_SUBSTEPS]
    ka_ref, vt_ref, o_ref, s_even, s_odd, macc_ref, mprev_ref, acc_ref = refs[ATTN_SUBSTEPS:]
    tq = qa_refs[0].shape[3]
    tk = vt_ref.shape[4]
    unit = tq // tk
    nq = s_even.shape[1] // tq
    lane = lax.broadcasted_iota(jnp.int32, (1, PAIR + AUG), 1)
    head0 = (lane < HEAD) | ((lane >= PAIR) & (lane < PAIR + AUG_PER_HEAD))
    head1 = ((lane >= HEAD) & (lane < PAIR)) | (
        (lane >= PAIR + AUG_PER_HEAD) & (lane < PAIR + 2 * AUG_PER_HEAD))
    ones_rows = jnp.ones((SUM_ROWS, tk), BF16)
    for sub in range(ATTN_SUBSTEPS):
        j = ATTN_SUBSTEPS * (pl.program_id(2) - 1) + 1 + sub
        s_cur, s_prev = (s_odd, s_even) if sub % 2 == 0 else (s_even, s_odd)
        _attn_substep(j, qa_refs[sub][0, 0], s_cur, s_prev, o_ref.at[0, sub * tq:(sub + 1) * tq, :],
                      ka_ref, vt_ref, macc_ref, mprev_ref, acc_ref, head0, head1, ones_rows,
                      tq, tk, unit, nq)


def _attn_substep(j, qa, s_cur, s_prev, o_ref, ka_ref, vt_ref, macc_ref, mprev_ref, acc_ref,
                  head0, head1, ones_rows, tq, tk, unit, nq):
    m_prev = mprev_ref[...]
    macc_ref[...] = jnp.full(macc_ref.shape, NEG, F32)
    acc_ref[...] = jnp.zeros_like(acc_ref)

    def score_tiles(s_ref, first_tile, n, diagonal):
        rows = n * tk
        start = pl.multiple_of(first_tile * tk, tk)
        k = ka_ref[0, pl.ds(start, rows), :]
        kz = jnp.zeros_like(k)
        k2 = jnp.concatenate([jnp.where(head0, k, kz), jnp.where(head1, k, kz)], axis=0)
        st = jnp.dot(k2, qa, preferred_element_type=F32)
        full = rows - tq if diagonal else rows
        for h in range(2):
            macc = macc_ref[h]
            if full:
                part = st[h * rows:h * rows + full]
                s_ref[h, pl.ds(start, full), :] = part
                macc = jnp.maximum(macc, jnp.max(part.reshape(full // 8, 8, tq), axis=0))
            if diagonal:
                key_pos = lax.broadcasted_iota(jnp.int32, (tq, tq), 0)
                qry_pos = lax.broadcasted_iota(jnp.int32, (tq, tq), 1)
                part = jnp.where(key_pos <= qry_pos, st[h * rows + full:(h + 1) * rows], NEG)
                s_ref[h, pl.ds(start + full, tq), :] = part
                macc = jnp.maximum(macc, jnp.max(part.reshape(tq // 8, 8, tq), axis=0))
            macc_ref[h] = macc

    def pv_tile(s_ref, t):
        start = pl.multiple_of(t * tk, tk)
        vt = vt_ref[0, 0, t]
        accs = []
        for h in range(2):
            p = jnp.exp2(s_ref[h, pl.ds(start, tk), :] - m_prev[h])
            lhs = jnp.concatenate([vt[h * HEAD:(h + 1) * HEAD, :], ones_rows], axis=0)
            accs.append(jnp.dot(lhs, p.astype(BF16), preferred_element_type=F32))
        return jnp.stack(accs)

    def pv_tiles(s_ref, first_tile, n):
        parts = [pv_tile(s_ref, first_tile + t) for t in range(n)]
        while len(parts) > 1:
            parts = [a + b for a, b in zip(parts[0::2], parts[1::2])]
        acc_ref[...] += parts[0]

    bit = nq // 2
    while bit >= 1:
        in_group = (j > 0) & (j < nq) & ((j & bit) != 0)
        lowest = (j & (bit - 1)) == 0
        first = (j & (-2 * bit)) * unit
        n = bit * unit

        if bit > 1:
            @pl.when(in_group & jnp.logical_not(lowest))
            def _(n=n, first=first):
                score_tiles(s_cur, first, n, False)
                pv_tiles(s_prev, first, n)

        @pl.when(in_group & lowest)
        def _(n=n, first=first):
            score_tiles(s_cur, first, n + unit, True)
            pv_tiles(s_prev, first, n)
        bit //= 2

    @pl.when(j == 0)
    def _():
        score_tiles(s_cur, 0, unit, True)

    @pl.when(j == nq)
    def _():
        for first in range(0, nq * unit, 8):
            pv_tiles(s_prev, first, 8)

    mprev_ref[...] = jnp.max(macc_ref[...], axis=1, keepdims=True)

    @pl.when(j > 0)
    def _():
        acc = acc_ref[...]
        ot = jnp.concatenate([acc[h, :HEAD] / acc[h, HEAD:HEAD + 1] for h in range(2)], axis=0)
        o_ref[...] = ot.T.astype(o_ref.dtype)


def _attention(qa, ka, vt):
    b, _, _, s = qa.shape
    tq = ATTN_TQ
    nk, tk = vt.shape[2], vt.shape[4]
    nq = s // tq
    assert nq & (nq - 1) == 0 and (nq * tq // tk) % 8 == 0, "query tiles must be a power of two"
    n = ATTN_SUBSTEPS
    assert n % 2 == 0 and nq % n == 0

    def qa_spec(sub):
        tile = lambda t: jnp.clip(n * (t - 1) + 1 + sub, 0, nq - 1)
        return pl.BlockSpec((1, 1, PAIR + AUG, tq), lambda bi, p, t: (bi, p, 0, tile(t)))

    return pl.pallas_call(
        _attn_kernel,
        grid=(b, N_PAIR_ATTN, nq // n + 1),
        in_specs=[qa_spec(sub) for sub in range(n)] + [
            pl.BlockSpec((1, s, PAIR + AUG), lambda bi, p, t: (bi, 0, p)),
            pl.BlockSpec((1, 1, nk, PAIR, tk), lambda bi, p, t: (bi, p, 0, 0, 0))],
        out_specs=pl.BlockSpec((1, n * tq, PAIR), lambda bi, p, t: (bi, jnp.maximum(t - 1, 0), p)),
        out_shape=jax.ShapeDtypeStruct((b, s, N_PAIR_ATTN * PAIR), BF16),
        scratch_shapes=[pltpu.VMEM((2, s, tq), F32),
                        pltpu.VMEM((2, s, tq), F32),
                        pltpu.VMEM((2, 8, tq), F32),
                        pltpu.VMEM((2, 1, tq), F32),
                        pltpu.VMEM((2, HEAD + SUM_ROWS, tq), F32)],
        compiler_params=pltpu.CompilerParams(
            dimension_semantics=("arbitrary", "arbitrary", "arbitrary"),
            vmem_limit_bytes=VMEM_LIMIT),
        name="attn",
    )(*([qa] * n), ka, vt)


def _post_kernel(x_ref, a_ref, wo_ref, gtm_ref, g_ref, sh_ref, sc_ref, gtf_ref,
                 win_ref, wd_ref, gfin_ref, o_ref, *, final_norm):
    x1 = x_ref[0] + gtm_ref[0] * jnp.dot(a_ref[0], wo_ref[...], preferred_element_type=F32)
    h = _norm_mod(x1, g_ref[...], sh_ref[0], sc_ref[0]).astype(BF16)
    ff = jnp.zeros_like(x1)
    for lo, hi in zip(FF_SLAB_EDGES[:-1], FF_SLAB_EDGES[1:]):
        gate = jnp.dot(h, win_ref[0, :, lo:hi], preferred_element_type=F32)
        up = jnp.dot(h, win_ref[0, :, D_FF + lo:D_FF + hi], preferred_element_type=F32)
        act = (gate * _sigmoid(gate) * up).astype(BF16)
        ff = ff + jnp.dot(act, wd_ref[0, lo:hi, :], preferred_element_type=F32)
    x2 = x1 + gtf_ref[0] * ff
    if final_norm:
        x2 = x2 * lax.rsqrt(jnp.mean(x2 * x2, axis=-1, keepdims=True) + NORM_EPS) * gfin_ref[...]
    o_ref[0] = x2


def _post(x, a, w_o, gt_m, g_ffn, sh_f, sc_f, gt_f, w_in, w_down, layer, g_final, final_norm):
    b, s, _ = x.shape
    tm = 512
    row3 = lambda bi, si: (bi, 0, 0)
    const2 = lambda bi, si: (0, 0)
    tile3 = lambda bi, si: (bi, si, 0)
    of_layer = lambda bi, si: (layer, 0, 0)
    once = pl.Buffered(1)
    kern = functools.partial(_post_kernel, final_norm=final_norm)
    return pl.pallas_call(
        kern,
        grid=(b, s // tm),
        in_specs=[pl.BlockSpec((1, tm, D), tile3),
                  pl.BlockSpec((1, tm, D), tile3),
                  pl.BlockSpec((D, D), const2),
                  pl.BlockSpec((1, 1, D), row3),
                  pl.BlockSpec((1, D), const2),
                  pl.BlockSpec((1, 1, D), row3),
                  pl.BlockSpec((1, 1, D), row3),
                  pl.BlockSpec((1, 1, D), row3),
                  pl.BlockSpec((1, D, 2 * D_FF), of_layer, pipeline_mode=once),
                  pl.BlockSpec((1, D_FF, D), of_layer, pipeline_mode=once),
                  pl.BlockSpec((1, D), const2)],
        out_specs=pl.BlockSpec((1, tm, D), tile3),
        out_shape=jax.ShapeDtypeStruct((b, s, D), F32),
        compiler_params=pltpu.CompilerParams(
            dimension_semantics=("arbitrary", "arbitrary"), vmem_limit_bytes=VMEM_LIMIT),
        name="post_final" if final_norm else "post",
    )(x, a, w_o, gt_m, g_ffn, sh_f, sc_f, gt_f, w_in, w_down, g_final)


def _rwkvproj_kernel(x_ref, g_ref, sh_ref, sc_ref, mu_ref, wr_ref, wk_ref, wv_ref,
                     w1_ref, w2_ref, a1_ref, a2_ref, g1_ref, g2_ref,
                     w0_ref, a0_ref, kk_ref, ka_ref,
                     r_ref, k_ref, v_ref, kkv_ref, a_ref, lw_ref, gate_ref, carry_ref):
    tm = x_ref.shape[1]

    @pl.when(pl.program_id(1) == 0)
    def _():
        carry_ref[...] = jnp.zeros_like(carry_ref)

    h = _norm_mod(x_ref[0], g_ref[...], sh_ref[0], sc_ref[0])
    rowi = lax.broadcasted_iota(jnp.int32, (tm, 1), 0)
    h_prev = jnp.where(rowi == 0, carry_ref[...], pltpu.roll(h, 1, 0))
    carry_ref[...] = h[tm - 1:tm, :]
    xx = h_prev - h
    mix = lambda i: (h + xx * mu_ref[i:i + 1, :]).astype(BF16)
    cw = RWKV_PROJ_COLS
    slabs = [slice(c * cw, (c + 1) * cw) for c in range(D // cw)]

    def project(lhs, w_ref, out_ref, which):
        for cols in which:
            out_ref[0, :, cols] = jnp.dot(lhs, w_ref[:, cols],
                                          preferred_element_type=F32).astype(out_ref.dtype)

    half0, half1 = slabs[:len(slabs) // 2], slabs[len(slabs) // 2:]
    xr = mix(0)
    xv = mix(3)
    project(xr, wr_ref, r_ref, half0)
    hid_g = _sigmoid(_mm(mix(5), g1_ref[...])).astype(BF16)
    project(xr, wr_ref, r_ref, half1)
    hid_w = jnp.tanh(_mm(mix(1), w1_ref[...])).astype(BF16)
    project(xv, wv_ref, v_ref, half0)
    hid_a = _mm(mix(4), a1_ref[...]).astype(BF16)
    project(xv, wv_ref, v_ref, half1)
    xk = mix(2)
    project(hid_g, g2_ref, gate_ref, slabs)
    for cols in slabs:
        k = jnp.dot(xk, wk_ref[:, cols], preferred_element_type=F32)
        a = _sigmoid(a0_ref[:, cols] + jnp.dot(hid_a, a2_ref[:, cols], preferred_element_type=F32))
        wl = w0_ref[:, cols] + jnp.dot(hid_w, w2_ref[:, cols], preferred_element_type=F32)
        k_ref[0, :, cols] = (k * (1.0 + (a - 1.0) * ka_ref[:, cols])).astype(k_ref.dtype)
        kkv_ref[0, :, cols] = k * kk_ref[:, cols]
        a_ref[0, :, cols] = a
        lw_ref[0, :, cols] = -math.exp(-0.5) * _sigmoid(wl)


def _rwkv_proj(x, g, sh, sc, mu, w_r, w_k, w_v, w1, w2, a1, a2, g1, g2, w0, a0, k_k, k_a):
    b, s, _ = x.shape
    tm = 512
    row3 = lambda bi, si: (bi, 0, 0)
    const2 = lambda bi, si: (0, 0)
    tile3 = lambda bi, si: (bi, si, 0)
    full = lambda arr: pl.BlockSpec(arr.shape, const2)
    outs = pl.pallas_call(
        _rwkvproj_kernel,
        grid=(b, s // tm),
        in_specs=[pl.BlockSpec((1, tm, D), tile3), full(g),
                  pl.BlockSpec((1, 1, D), row3), pl.BlockSpec((1, 1, D), row3),
                  full(mu), full(w_r), full(w_k), full(w_v),
                  full(w1), full(w2), full(a1), full(a2), full(g1), full(g2),
                  full(w0), full(a0), full(k_k), full(k_a)],
        out_specs=[pl.BlockSpec((1, tm, D), tile3)] * 7,
        out_shape=[jax.ShapeDtypeStruct((b, s, D), dt) for dt in (BF16, BF16, BF16, F32, F32, F32, BF16)],
        scratch_shapes=[pltpu.VMEM((1, D), F32)],
        compiler_params=pltpu.CompilerParams(
            dimension_semantics=("arbitrary", "arbitrary"), vmem_limit_bytes=VMEM_LIMIT),
        name="rwkv_proj",
    )(x, g, sh, sc, mu, w_r, w_k, w_v, w1, w2, a1, a2, g1, g2, w0, a0, k_k, k_a)
    return outs


def _scan_kernel(r_ref, k_ref, v_ref, kkv_ref, a_ref, lw_ref, gate_ref,
                 rk_ref, lnw_ref, lnb_ref, o_ref, state_ref):
    tc = r_ref.shape[1]
    group_rows = SCAN_GROUP * CHUNK
    n_groups = tc // group_rows

    @pl.when(pl.program_id(2) == 0)
    def _():
        state_ref[...] = jnp.zeros_like(state_ref)

    li = lax.broadcasted_iota(jnp.int32, (PAIR, PAIR), 0)
    lj = lax.broadcasted_iota(jnp.int32, (PAIR, PAIR), 1)
    ones_bd = jnp.where((li >> 6) == (lj >> 6), 1.0, 0.0).astype(BF16)
    strict = (li & (CHUNK - 1)) > (lj & (CHUNK - 1))
    incl = (li & (CHUNK - 1)) >= (lj & (CHUNK - 1))
    eye = jnp.where(li == lj, 1.0, 0.0).astype(F32)
    tri_rows = 4 * CHUNK
    ti = lax.broadcasted_iota(jnp.int32, (tri_rows, tri_rows), 0)
    tj = lax.broadcasted_iota(jnp.int32, (tri_rows, tri_rows), 1)
    tri = jnp.where((ti >= tj) & ((ti >> 6) == (tj >> 6)), 1.0, 0.0).astype(BF16)
    is_h0 = lax.broadcasted_iota(jnp.int32, (1, PAIR), 1) < HEAD
    chunks = range(SCAN_GROUP)

    def stack(z):
        return jnp.concatenate([jnp.where(is_h0, z, 0.0), jnp.where(is_h0, 0.0, z)], axis=0)

    def state_free_part(g, out):
        rows = slice(g * group_rows, (g + 1) * group_rows)
        r = r_ref[0, rows, :].astype(F32)
        k = k_ref[0, rows, :].astype(F32)
        v = v_ref[0, rows, :].astype(F32)
        kkv = kkv_ref[0, rows, :]
        lw = lw_ref[0, rows, :]
        lw1 = lw.astype(BF16)
        lw2 = (lw - lw1.astype(F32)).astype(BF16)
        cl = jnp.concatenate(
            [jnp.dot(tri, lw1[i:i + tri_rows], preferred_element_type=F32)
             + jnp.dot(tri, lw2[i:i + tri_rows], preferred_element_type=F32)
             for i in range(0, group_rows, tri_rows)], axis=0)
        n2 = _mm(kkv * kkv, ones_bd)
        kk = kkv / jnp.maximum(jnp.sqrt(n2), 1e-12)
        bvec = kk * a_ref[0, rows, :]
        egi = jnp.exp(-cl)
        a_t = -kk * jnp.exp(cl - lw)
        b_t = bvec * egi
        k_t = k * egi
        r_t = r * jnp.exp(cl)
        out["bonus"] = _mm(r * k * rk_ref[...], ones_bd) * v
        sls = [slice(c * CHUNK, (c + 1) * CHUNK) for c in chunks]
        cl_end = [cl[(c + 1) * CHUNK - 1:(c + 1) * CHUNK, :] for c in chunks]
        out["decay"] = [jnp.exp(ce) for ce in cl_end]
        tails = [jnp.exp(cl_end[c] - cl[sls[c]]) for c in chunks]
        a_s = [stack(a_t[sl]) for sl in sls]
        r_s = [stack(r_t[sl]) for sl in sls]
        v_s = [stack(v[sl]) for sl in sls]
        bh_s = [stack(bvec[sls[c]] * tails[c]) for c in chunks]
        kh_s = [stack(k[sls[c]] * tails[c]) for c in chunks]
        grams = [_mm_nt(jnp.concatenate([a_s[c], r_s[c]], axis=0),
                        jnp.concatenate([stack(b_t[sls[c]]), stack(k_t[sls[c]])], axis=0))
                 for c in chunks]
        yield
        m_ak = [jnp.where(strict, gm[:PAIR, PAIR:], 0.0) for gm in grams]
        a_r = [jnp.concatenate([jnp.where(incl, gm[PAIR:, :PAIR], 0.0),
                                jnp.where(incl, gm[PAIR:, PAIR:], 0.0)], axis=1) for gm in grams]
        rhs = [jnp.concatenate([a_s[c], _mm(m_ak[c], v_s[c])], axis=1) for c in chunks]
        mps = [jnp.where(strict, gm[:PAIR, :PAIR], 0.0) for gm in grams]
        ts = [eye + mp for mp in mps]
        mps = [_mm(mp, mp) for mp in mps]
        yield
        for level in range(1, 6):
            if level < 5:
                res = [_mm(mps[c], jnp.concatenate([mps[c], ts[c]], axis=1)) for c in chunks]
                mps = [rs[:, :PAIR] for rs in res]
                ts = [ts[c] + res[c][:, PAIR:] for c in chunks]
            else:
                ts = [ts[c] + _mm(mps[c], ts[c]) for c in chunks]
            yield
        xs = [_mm(ts[c], rhs[c]) for c in chunks]
        gys = [_mm(a_r[c], jnp.concatenate(
            [xs[c], jnp.concatenate([jnp.zeros_like(v_s[c]), v_s[c]], axis=1)], axis=0))
               for c in chunks]
        out["g"] = [r_s[c] + gys[c][:, :PAIR] for c in chunks]
        out["y0"] = [gy[:, PAIR:] for gy in gys]
        yield
        out["pb"] = [_mm_tn(xs[c][:, :PAIR], bh_s[c]) for c in chunks]
        out["q"] = [_mm_tn(jnp.concatenate([xs[c][:, PAIR:], v_s[c]], axis=0),
                           jnp.concatenate([bh_s[c], kh_s[c]], axis=0)) for c in chunks]
        yield

    carried_state = [state_ref[...]]

    def state_part(g, res):
        ys = []
        for c in chunks:
            state = carried_state[0]
            y_s = _mm_nt(res["g"][c], state) + res["y0"][c]
            ys.append(y_s[:CHUNK] + y_s[CHUNK:])
            carried_state[0] = state * res["decay"][c] + _mm(state, res["pb"][c]) + res["q"][c]
            yield
        rows = slice(g * group_rows, (g + 1) * group_rows)
        y = jnp.concatenate(ys, axis=0)
        inv_n = 1.0 / HEAD
        mean = _mm(y, ones_bd) * inv_n
        yc = y - mean
        var = _mm(yc * yc, ones_bd) * inv_n
        yn = yc * lax.rsqrt(var + GN_EPS) * lnw_ref[...] + lnb_ref[...]
        o_ref[0, rows, :] = ((yn + res["bonus"]) * gate_ref[0, rows, :].astype(F32)
                             ).astype(o_ref.dtype)
        yield

    results = [dict() for _ in range(n_groups)]
    for _ in state_free_part(0, results[0]):
        pass
    for g in range(n_groups):
        carried = state_part(g, results[g])
        ahead = state_free_part(g + 1, results[g + 1]) if g + 1 < n_groups else iter(())
        n_carried, n_ahead = SCAN_GROUP + 1, 9
        for i, _ in enumerate(ahead):
            if (i * n_carried) // n_ahead != ((i + 1) * n_carried) // n_ahead:
                next(carried, None)
        for _ in carried:
            pass
    state_ref[...] = carried_state[0]


def _rwkv_scan(r, k, v, kkv, a, lw, gate, r_k, ln_w, ln_b):
    b, s, _ = r.shape
    tc = SCAN_BLOCK
    tile = pl.BlockSpec((1, tc, PAIR), lambda bi, p, ti: (bi, ti, p))
    vec = pl.BlockSpec((1, PAIR), lambda bi, p, ti: (0, p))
    return pl.pallas_call(
        _scan_kernel,
        grid=(b, D // PAIR, s // tc),
        in_specs=[tile] * 7 + [vec] * 3,
        out_specs=tile,
        out_shape=jax.ShapeDtypeStruct((b, s, D), BF16),
        scratch_shapes=[pltpu.VMEM((PAIR, PAIR), F32)],
        compiler_params=pltpu.CompilerParams(
            dimension_semantics=("arbitrary", "arbitrary", "arbitrary"),
            vmem_limit_bytes=VMEM_LIMIT),
        name="rwkv_scan",
    )(r, k, v, kkv, a, lw, gate, r_k, ln_w, ln_b)


def kernel(x, c, attn_w_in, attn_b_f, attn_w_out, rwkv_mu, rwkv_w_r, rwkv_w_k, rwkv_w_v, rwkv_w_o, rwkv_w0, rwkv_w1, rwkv_w2, rwkv_a0, rwkv_a1, rwkv_a2, rwkv_g1, rwkv_g2, rwkv_k_k, rwkv_k_a, rwkv_r_k, rwkv_ln_w, rwkv_ln_b, mod_w, mod_b, norm_mix, norm_ffn, ffn_w_in, ffn_w_out, norm_final):
    b, s, _ = x.shape
    mod = _modulation(c, mod_w, mod_b)
    mods = [[mod[l, :, i * D:(i + 1) * D].reshape(b, 1, D) for i in range(6)] for l in range(2)]
    row = lambda t: t.reshape(1, D)
    g_final = row(norm_final)

    sh_m, sc_m, gt_m, sh_f, sc_f, gt_f = mods[0]
    w_in = attn_w_in[0]
    qscale = HEAD ** -0.5 * LOG2E
    col_scale = jnp.ones((3 * D,), F32)
    col_scale = col_scale.at[0:FOX_W].set(qscale).at[3 * FOX_W:4 * FOX_W].set(qscale)
    w_main = (w_in[:, :3 * D] * col_scale[None, :]).astype(BF16)
    w_f = jnp.zeros((D, AUG), F32).at[:, :8].set(w_in[:, 3 * D:]).astype(BF16)
    b_f = jnp.zeros((1, AUG), F32).at[0, :8].set(attn_b_f[0])
    qa, ka, vt = _inproj(x, row(norm_mix[0]), sh_m, sc_m, w_main, w_f, b_f)
    o = _attention(qa, ka, vt)

    w_ffn_in = ffn_w_in.astype(BF16)
    w_ffn_out = ffn_w_out.astype(BF16)
    x = _post(x, o, attn_w_out[0].astype(BF16), gt_m, row(norm_ffn[0]), sh_f, sc_f, gt_f,
              w_ffn_in, w_ffn_out, 0, g_final, final_norm=False)

    sh_m, sc_m, gt_m, sh_f, sc_f, gt_f = mods[1]
    pad_in = lambda w, n: jnp.zeros((D, n), BF16).at[:, :w.shape[1]].set(w.astype(BF16))
    pad_out = lambda w, n: jnp.zeros((n, D), BF16).at[:w.shape[0], :].set(w.astype(BF16))
    r, k, v, kkv, a, lw, gate = _rwkv_proj(
        x, row(norm_mix[1]), sh_m, sc_m, rwkv_mu[0],
        rwkv_w_r[0].astype(BF16), rwkv_w_k[0].astype(BF16), rwkv_w_v[0].astype(BF16),
        pad_in(rwkv_w1[0], 128), pad_out(rwkv_w2[0], 128),
        pad_in(rwkv_a1[0], 128), pad_out(rwkv_a2[0], 128),
        pad_in(rwkv_g1[0], 256), pad_out(rwkv_g2[0], 256),
        row(rwkv_w0[0]), row(rwkv_a0[0]), row(rwkv_k_k[0]), row(rwkv_k_a[0]))
    y = _rwkv_scan(r, k, v, kkv, a, lw, gate, row(rwkv_r_k[0]), row(rwkv_ln_w[0]),
                   row(rwkv_ln_b[0]))
    x = _post(x, y, rwkv_w_o[0].astype(BF16), gt_m, row(norm_ffn[1]), sh_f, sc_f, gt_f,
              w_ffn_in, w_ffn_out, 1, g_final, final_norm=True)
    return x
```

```python
import functools
import math

import jax
import jax.numpy as jnp
import numpy as np
from jax import lax
from jax.experimental import pallas as pl
from jax.experimental.pallas import tpu as pltpu

F32 = jnp.float32
BF16 = jnp.bfloat16

D = 1024
HEAD = 64
PAIR = 2 * HEAD
N_PAIR_ATTN = 8
FOX_W = 512
D_FF = 2816
MXU_WIDTH = 256
FF_SLAB_EDGES = (0, 5 * MXU_WIDTH, D_FF)
MOBA_BLOCK = 256
N_MOBA_BLOCKS = 16
MOBA_TOPK = 3
AUG = 128
AUG_PER_HEAD = 16
NORM_EPS = 1e-6
GN_EPS = HEAD * 1e-5
LOG2E = 1.4426950408889634
NEG = -1e30
PEN = -30000.0
CHUNK = 64
SCAN_GROUP = 8
SCAN_BLOCK = 8 * SCAN_GROUP * CHUNK
BF16_SUBLANES = 16
SUM_ROWS = BF16_SUBLANES
ATTN_SUBSTEPS = 4
ATTN_TQ = 256
ATTN_FLUSH_TILES = 8
RWKV_PROJ_COLS = MXU_WIDTH
ROW_TILE = 512
MOD_COLS = 1536
LOG2_HEAD = 6
LOG2_MOBA_BLOCKS = 4
assert (1 << LOG2_HEAD, 1 << LOG2_MOBA_BLOCKS, CHUNK) == (HEAD, N_MOBA_BLOCKS, HEAD)
VMEM_LIMIT = 56 * 1024 * 1024


def _mm(a, b):
    return jnp.dot(a.astype(BF16), b.astype(BF16), preferred_element_type=F32)


def _mm_nt(a, b):
    return lax.dot_general(a.astype(BF16), b.astype(BF16), (((1,), (1,)), ((), ())),
                           preferred_element_type=F32)


def _mm_tn(a, b):
    return lax.dot_general(a.astype(BF16), b.astype(BF16), (((0,), (0,)), ((), ())),
                           preferred_element_type=F32)


def _sigmoid(x):
    return 1.0 / (1.0 + jnp.exp(-x))


def _softplus(x):
    return jnp.maximum(x, 0.0) + jnp.log(1.0 + jnp.exp(-jnp.abs(x)))


def _norm_mod(x, g, sh, sc):
    y = x * lax.rsqrt(jnp.mean(x * x, axis=-1, keepdims=True) + NORM_EPS)
    return (y * g) * (1.0 + sc) + sh


def _mod_kernel(c_ref, w_ref, b_ref, o_ref):
    c = c_ref[...]
    c1, c2, c3 = _split3(c * _sigmoid(c))
    w = w_ref[0]
    w1 = w.astype(BF16)
    w2 = (w - w1.astype(F32)).astype(BF16)
    dot = lambda a, b: jnp.dot(a, b, preferred_element_type=F32)
    o_ref[0] = (dot(c1, w1) + (dot(c2, w1) + dot(c1, w2))
                + (dot(c3, w1) + dot(c2, w2))) + b_ref[0]


def _modulation(c, mod_w, mod_b):
    depth, _, n = mod_w.shape
    b = c.shape[0]
    rows = BF16_SUBLANES
    c_pad = jnp.zeros((rows, D), F32).at[:b].set(c)
    tn = MOD_COLS
    out = pl.pallas_call(
        _mod_kernel,
        grid=(depth, n // tn),
        in_specs=[pl.BlockSpec((rows, D), lambda l, j: (0, 0)),
                  pl.BlockSpec((1, D, tn), lambda l, j: (l, 0, j)),
                  pl.BlockSpec((1, 1, tn), lambda l, j: (l, 0, j))],
        out_specs=pl.BlockSpec((1, rows, tn), lambda l, j: (l, 0, j)),
        out_shape=jax.ShapeDtypeStruct((depth, rows, n), F32),
        compiler_params=pltpu.CompilerParams(
            dimension_semantics=("arbitrary", "arbitrary"), vmem_limit_bytes=VMEM_LIMIT),
        name="mod",
    )(c_pad, mod_w, mod_b.reshape(depth, 1, n))
    return out[:, :b]


def _split3(z):
    z1 = z.astype(BF16)
    r1 = z - z1.astype(F32)
    z2 = r1.astype(BF16)
    return z1, z2, (r1 - z2.astype(F32)).astype(BF16)


def _mm_split3_lhs_exact(a_exact, b):
    aa = a_exact.astype(BF16)
    b1, b2, b3 = _split3(b)
    return (jnp.dot(aa, b1, preferred_element_type=F32)
            + jnp.dot(aa, b2, preferred_element_type=F32)
            + jnp.dot(aa, b3, preferred_element_type=F32))


def _moba_penalty_steps(gate, own, out):
    lane = lax.broadcasted_iota(jnp.int32, gate.shape, 1)
    n = lane & (N_MOBA_BLOCKS - 1)
    past = n < own
    g = jnp.where(past, gate, -jnp.inf)
    rank = jnp.zeros(gate.shape, jnp.int32)
    for k in range(1, N_MOBA_BLOCKS):
        gk = jnp.where(n >= k, pltpu.roll(g, k, 1), pltpu.roll(g, AUG - N_MOBA_BLOCKS + k, 1))
        beats = (gk > g) | ((gk == g) & (n >= k))
        rank = rank + beats.astype(jnp.int32)
        yield
    keep = (past & (rank < MOBA_TOPK)) | (n == own)
    out[0] = jnp.where(keep, 0.0, PEN)


def _inproj_kernel(x_ref, g_ref, sh_ref, sc_ref, w_ref, wf_ref, bf_ref, sel_c_ref, const_c_ref,
                   sel_p_ref, qa_ref, ka_ref, vt_ref, carry_ref, km_ref):
    tm = x_ref.shape[1]
    si = pl.program_id(1)

    @pl.when(si == 0)
    def _():
        carry_ref[...] = jnp.zeros_like(carry_ref)
        km_ref[...] = jnp.zeros_like(km_ref)

    h = _norm_mod(x_ref[0], g_ref[...], sh_ref[0], sc_ref[0]).astype(BF16)
    width = PAIR + AUG

    def proj(lo, hi):
        return jnp.dot(h, w_ref[:, lo:hi], preferred_element_type=F32)

    def put(ref, pair, value):
        if ref is qa_ref:
            ref[0, pair, 0:PAIR, :] = value.T.astype(BF16)
        else:
            ref[0, :, pair * width:pair * width + PAIR] = value.astype(BF16)

    def put_bias(ref, pair, value):
        if ref is qa_ref:
            ref[0, pair, PAIR:width, :] = value.T.astype(BF16)
        else:
            ref[0, :, pair * width + PAIR:(pair + 1) * width] = value.astype(BF16)

    q_mb = proj(3 * FOX_W, 4 * FOX_W)
    qb = q_mb.astype(BF16)
    km1, km2, km3 = _split3(km_ref[...])
    gate = (jnp.dot(qb, km1, preferred_element_type=F32)
            + jnp.dot(qb, km2, preferred_element_type=F32)
            + jnp.dot(qb, km3, preferred_element_type=F32))
    pen = [None]
    ranking = _moba_penalty_steps(gate, si, pen)

    def fox_bias():
        f = jnp.dot(h, wf_ref[...], preferred_element_type=F32) + bf_ref[...]
        lf = -_softplus(-f)
        row = lax.broadcasted_iota(jnp.int32, (tm, tm), 0)
        col = lax.broadcasted_iota(jnp.int32, (tm, tm), 1)
        tri = jnp.where(row >= col, 1.0, 0.0).astype(F32)
        cum = _mm_split3_lhs_exact(tri, lf) + carry_ref[...]
        carry_ref[...] = cum[tm - 1:tm, :]
        c3 = jnp.concatenate(_split3(cum * LOG2E), axis=1)
        aug = jnp.dot(c3, sel_c_ref[...], preferred_element_type=F32) + const_c_ref[...]
        for p in range(4):
            put_bias(qa_ref, p, aug[:, p * PAIR:(p + 1) * PAIR])
            put_bias(ka_ref, p, aug[:, FOX_W + p * PAIR:FOX_W + (p + 1) * PAIR])

    def fox_qk(half):
        q = proj(half * 2 * PAIR, (half + 1) * 2 * PAIR)
        k = proj(FOX_W + half * 2 * PAIR, FOX_W + (half + 1) * 2 * PAIR)
        for i in range(2):
            put(qa_ref, 2 * half + i, q[:, i * PAIR:(i + 1) * PAIR])
            put(ka_ref, 2 * half + i, k[:, i * PAIR:(i + 1) * PAIR])

    def values(group, half):
        lo = (2 + 3 * group) * FOX_W + half * 2 * PAIR
        v = proj(lo, lo + 2 * PAIR)
        first = 4 * group + 2 * half
        vt_ref[0, first:first + 2, 0] = v.T.reshape(2, PAIR, tm).astype(BF16)

    def moba_k():
        k_mb = proj(4 * FOX_W, 5 * FOX_W)
        lane_p = lax.broadcasted_iota(jnp.int32, (tm, AUG), 1)
        onehot = jnp.where((lane_p < 2 * AUG_PER_HEAD) & ((lane_p & (N_MOBA_BLOCKS - 1)) == si),
                           1.0, 0.0)
        for p in range(4):
            put(ka_ref, 4 + p, k_mb[:, p * PAIR:(p + 1) * PAIR])
            put_bias(ka_ref, 4 + p, onehot)
        kb_t = k_mb.astype(BF16).astype(F32).T
        place = jnp.where((lane_p & (N_MOBA_BLOCKS - 1)) == si, 1.0 / tm, 0.0)
        upd = jnp.dot(kb_t.astype(BF16), place.astype(BF16), preferred_element_type=F32)
        d_head = lax.broadcasted_iota(jnp.int32, (FOX_W, AUG), 0) >> LOG2_HEAD
        l_head = lax.broadcasted_iota(jnp.int32, (FOX_W, AUG), 1) >> LOG2_MOBA_BLOCKS
        km_ref[...] += jnp.where(d_head == l_head, upd, 0.0)

    others = [fox_bias, lambda: fox_qk(0), lambda: fox_qk(1), moba_k,
              lambda: values(0, 0), lambda: values(0, 1), lambda: values(1, 0),
              lambda: values(1, 1)]
    for i, _ in enumerate(ranking):
        if i % 2 == 0 and others:
            others.pop(0)()
    for task in others:
        task()
    aug_mb_q = jnp.dot(pen[0].astype(BF16), sel_p_ref[...], preferred_element_type=F32)
    for p in range(4):
        put(qa_ref, 4 + p, q_mb[:, p * PAIR:(p + 1) * PAIR])
        put_bias(qa_ref, 4 + p, aug_mb_q[:, p * PAIR:(p + 1) * PAIR])


def _bias_lane_selectors():
    sel_c = np.zeros((3 * AUG, 2 * FOX_W), np.float32)
    const_c = np.zeros((1, 2 * FOX_W), np.float32)
    sel_p = np.zeros((AUG, FOX_W), np.float32)
    for head in range(8):
        base = (head // 2) * AUG + (head % 2) * AUG_PER_HEAD
        for i in range(3):
            sel_c[i * AUG + head, base + i] = 1.0
            sel_c[i * AUG + head, FOX_W + base + 3 + i] = -1.0
            const_c[0, base + 3 + i] = 1.0
            const_c[0, FOX_W + base + i] = 1.0
        for n in range(N_MOBA_BLOCKS):
            sel_p[head * N_MOBA_BLOCKS + n, base + n] = 1.0
    return jnp.asarray(sel_c, BF16), jnp.asarray(const_c, F32), jnp.asarray(sel_p, BF16)


def _inproj(x, g, sh, sc, w_main, w_f, b_f):
    b, s, _ = x.shape
    tm = MOBA_BLOCK
    n = w_main.shape[1]
    width = N_PAIR_ATTN * (PAIR + AUG)
    sel_c, const_c, sel_p = _bias_lane_selectors()
    row3 = lambda bi, si: (bi, 0, 0)
    const2 = lambda bi, si: (0, 0)
    tile3 = lambda bi, si: (bi, si, 0)
    full = lambda arr: pl.BlockSpec(arr.shape, const2)
    return pl.pallas_call(
        _inproj_kernel,
        grid=(b, s // tm),
        in_specs=[pl.BlockSpec((1, tm, D), tile3),
                  pl.BlockSpec((1, D), const2),
                  pl.BlockSpec((1, 1, D), row3),
                  pl.BlockSpec((1, 1, D), row3),
                  pl.BlockSpec((D, n), const2),
                  pl.BlockSpec((D, AUG), const2),
                  pl.BlockSpec((1, AUG), const2),
                  full(sel_c), full(const_c), full(sel_p)],
        out_specs=[pl.BlockSpec((1, N_PAIR_ATTN, PAIR + AUG, tm), lambda bi, si: (bi, 0, 0, si)),
                   pl.BlockSpec((1, tm, width), tile3),
                   pl.BlockSpec((1, N_PAIR_ATTN, 1, PAIR, tm), lambda bi, si: (bi, 0, si, 0, 0))],
        out_shape=[jax.ShapeDtypeStruct((b, N_PAIR_ATTN, PAIR + AUG, s), BF16),
                   jax.ShapeDtypeStruct((b, s, width), BF16),
                   jax.ShapeDtypeStruct((b, N_PAIR_ATTN, s // tm, PAIR, tm), BF16)],
        scratch_shapes=[pltpu.VMEM((1, AUG), F32), pltpu.VMEM((FOX_W, AUG), F32)],
        compiler_params=pltpu.CompilerParams(
            dimension_semantics=("arbitrary", "arbitrary"), vmem_limit_bytes=VMEM_LIMIT),
        name="inproj",
    )(x, g, sh, sc, w_main, w_f, b_f, sel_c, const_c, sel_p)


def _attn_kernel(*refs):
    qa_refs = refs[:ATTN_SUBSTEPS]
    ka_ref, vt_ref, o_ref, s_even, s_odd, macc_ref, mprev_ref, acc_ref = refs[ATTN_SUBSTEPS:]
    tq = qa_refs[0].shape[3]
    tk = vt_ref.shape[4]
    unit = tq // tk
    nq = s_even.shape[1] // tq
    lane = lax.broadcasted_iota(jnp.int32, (1, PAIR + AUG), 1)
    head0 = (lane < HEAD) | ((lane >= PAIR) & (lane < PAIR + AUG_PER_HEAD))
    head1 = ((lane >= HEAD) & (lane < PAIR)) | (
        (lane >= PAIR + AUG_PER_HEAD) & (lane < PAIR + 2 * AUG_PER_HEAD))
    ones_rows = jnp.ones((SUM_ROWS, tk), BF16)
    for sub in range(ATTN_SUBSTEPS):
        j = ATTN_SUBSTEPS * (pl.program_id(2) - 1) + 1 + sub
        s_cur, s_prev = (s_odd, s_even) if sub % 2 == 0 else (s_even, s_odd)
        _attn_substep(j, qa_refs[sub][0, 0], s_cur, s_prev, o_ref.at[0, sub * tq:(sub + 1) * tq, :],
                      ka_ref, vt_ref, macc_ref, mprev_ref, acc_ref, head0, head1, ones_rows,
                      tq, tk, unit, nq)


def _attn_substep(j, qa, s_cur, s_prev, o_ref, ka_ref, vt_ref, macc_ref, mprev_ref, acc_ref,
                  head0, head1, ones_rows, tq, tk, unit, nq):
    m_prev = mprev_ref[...]
    macc_ref[...] = jnp.full(macc_ref.shape, NEG, F32)
    acc_ref[...] = jnp.zeros_like(acc_ref)

    def score_tiles(s_ref, first_tile, n, diagonal):
        rows = n * tk
        start = pl.multiple_of(first_tile * tk, tk)
        k = ka_ref[0, pl.ds(start, rows), :]
        kz = jnp.zeros_like(k)
        k2 = jnp.concatenate([jnp.where(head0, k, kz), jnp.where(head1, k, kz)], axis=0)
        st = jnp.dot(k2, qa, preferred_element_type=F32)
        full = rows - tq if diagonal else rows
        for h in range(2):
            macc = macc_ref[h]
            if full:
                part = st[h * rows:h * rows + full]
                s_ref[h, pl.ds(start, full), :] = part
                macc = jnp.maximum(macc, jnp.max(part.reshape(full // 8, 8, tq), axis=0))
            if diagonal:
                key_pos = lax.broadcasted_iota(jnp.int32, (tq, tq), 0)
                qry_pos = lax.broadcasted_iota(jnp.int32, (tq, tq), 1)
                part = jnp.where(key_pos <= qry_pos, st[h * rows + full:(h + 1) * rows], NEG)
                s_ref[h, pl.ds(start + full, tq), :] = part
                macc = jnp.maximum(macc, jnp.max(part.reshape(tq // 8, 8, tq), axis=0))
            macc_ref[h] = macc

    def pv_tile(s_ref, t):
        start = pl.multiple_of(t * tk, tk)
        vt = vt_ref[0, 0, t]
        accs = []
        for h in range(2):
            p = jnp.exp2(s_ref[h, pl.ds(start, tk), :] - m_prev[h])
            lhs = jnp.concatenate([vt[h * HEAD:(h + 1) * HEAD, :], ones_rows], axis=0)
            accs.append(jnp.dot(lhs, p.astype(BF16), preferred_element_type=F32))
        return jnp.stack(accs)

    def pv_tiles(s_ref, first_tile, n):
        parts = [pv_tile(s_ref, first_tile + t) for t in range(n)]
        while len(parts) > 1:
            parts = [a + b for a, b in zip(parts[0::2], parts[1::2])]
        acc_ref[...] += parts[0]

    bit = nq // 2
    while bit >= 1:
        in_group = (j > 0) & (j < nq) & ((j & bit) != 0)
        lowest = (j & (bit - 1)) == 0
        first = (j & (-2 * bit)) * unit
        n = bit * unit

        if bit > 1:
            @pl.when(in_group & jnp.logical_not(lowest))
            def _(n=n, first=first):
                score_tiles(s_cur, first, n, False)
                pv_tiles(s_prev, first, n)

        @pl.when(in_group & lowest)
        def _(n=n, first=first):
            score_tiles(s_cur, first, n + unit, True)
            pv_tiles(s_prev, first, n)
        bit //= 2

    @pl.when(j == 0)
    def _():
        score_tiles(s_cur, 0, unit, True)

    @pl.when(j == nq)
    def _():
        for first in range(0, nq * unit, ATTN_FLUSH_TILES):
            pv_tiles(s_prev, first, ATTN_FLUSH_TILES)

    mprev_ref[...] = jnp.max(macc_ref[...], axis=1, keepdims=True)

    @pl.when(j > 0)
    def _():
        acc = acc_ref[...]
        ot = jnp.concatenate([acc[h, :HEAD] / acc[h, HEAD:HEAD + 1] for h in range(2)], axis=0)
        o_ref[...] = ot.T.astype(o_ref.dtype)


def _attention(qa, ka, vt):
    b, _, _, s = qa.shape
    tq = ATTN_TQ
    nk, tk = vt.shape[2], vt.shape[4]
    nq = s // tq
    assert nq & (nq - 1) == 0 and (nq * tq // tk) % ATTN_FLUSH_TILES == 0, "query tiles: a power of two"
    n = ATTN_SUBSTEPS
    assert n % 2 == 0 and nq % n == 0

    def qa_spec(sub):
        tile = lambda t: jnp.clip(n * (t - 1) + 1 + sub, 0, nq - 1)
        return pl.BlockSpec((1, 1, PAIR + AUG, tq), lambda bi, p, t: (bi, p, 0, tile(t)))

    return pl.pallas_call(
        _attn_kernel,
        grid=(b, N_PAIR_ATTN, nq // n + 1),
        in_specs=[qa_spec(sub) for sub in range(n)] + [
            pl.BlockSpec((1, s, PAIR + AUG), lambda bi, p, t: (bi, 0, p)),
            pl.BlockSpec((1, 1, nk, PAIR, tk), lambda bi, p, t: (bi, p, 0, 0, 0))],
        out_specs=pl.BlockSpec((1, n * tq, PAIR), lambda bi, p, t: (bi, jnp.maximum(t - 1, 0), p)),
        out_shape=jax.ShapeDtypeStruct((b, s, N_PAIR_ATTN * PAIR), BF16),
        scratch_shapes=[pltpu.VMEM((2, s, tq), F32),
                        pltpu.VMEM((2, s, tq), F32),
                        pltpu.VMEM((2, 8, tq), F32),
                        pltpu.VMEM((2, 1, tq), F32),
                        pltpu.VMEM((2, HEAD + SUM_ROWS, tq), F32)],
        compiler_params=pltpu.CompilerParams(
            dimension_semantics=("arbitrary", "arbitrary", "arbitrary"),
            vmem_limit_bytes=VMEM_LIMIT),
        name="attn",
    )(*([qa] * n), ka, vt)


def _post_kernel(x_ref, a_ref, wo_ref, gtm_ref, g_ref, sh_ref, sc_ref, gtf_ref,
                 win_ref, wd_ref, gfin_ref, o_ref, *, final_norm):
    x1 = x_ref[0] + gtm_ref[0] * jnp.dot(a_ref[0], wo_ref[...], preferred_element_type=F32)
    h = _norm_mod(x1, g_ref[...], sh_ref[0], sc_ref[0]).astype(BF16)
    ff = jnp.zeros_like(x1)
    for lo, hi in zip(FF_SLAB_EDGES[:-1], FF_SLAB_EDGES[1:]):
        gate = jnp.dot(h, win_ref[0, :, lo:hi], preferred_element_type=F32)
        up = jnp.dot(h, win_ref[0, :, D_FF + lo:D_FF + hi], preferred_element_type=F32)
        act = (gate * _sigmoid(gate) * up).astype(BF16)
        ff = ff + jnp.dot(act, wd_ref[0, lo:hi, :], preferred_element_type=F32)
    x2 = x1 + gtf_ref[0] * ff
    if final_norm:
        x2 = x2 * lax.rsqrt(jnp.mean(x2 * x2, axis=-1, keepdims=True) + NORM_EPS) * gfin_ref[...]
    o_ref[0] = x2


def _post(x, a, w_o, gt_m, g_ffn, sh_f, sc_f, gt_f, w_in, w_down, layer, g_final, final_norm):
    b, s, _ = x.shape
    tm = ROW_TILE
    row3 = lambda bi, si: (bi, 0, 0)
    const2 = lambda bi, si: (0, 0)
    tile3 = lambda bi, si: (bi, si, 0)
    of_layer = lambda bi, si: (layer, 0, 0)
    once = pl.Buffered(1)
    kern = functools.partial(_post_kernel, final_norm=final_norm)
    return pl.pallas_call(
        kern,
        grid=(b, s // tm),
        in_specs=[pl.BlockSpec((1, tm, D), tile3),
                  pl.BlockSpec((1, tm, D), tile3),
                  pl.BlockSpec((D, D), const2),
                  pl.BlockSpec((1, 1, D), row3),
                  pl.BlockSpec((1, D), const2),
                  pl.BlockSpec((1, 1, D), row3),
                  pl.BlockSpec((1, 1, D), row3),
                  pl.BlockSpec((1, 1, D), row3),
                  pl.BlockSpec((1, D, 2 * D_FF), of_layer, pipeline_mode=once),
                  pl.BlockSpec((1, D_FF, D), of_layer, pipeline_mode=once),
                  pl.BlockSpec((1, D), const2)],
        out_specs=pl.BlockSpec((1, tm, D), tile3),
        out_shape=jax.ShapeDtypeStruct((b, s, D), F32),
        compiler_params=pltpu.CompilerParams(
            dimension_semantics=("arbitrary", "arbitrary"), vmem_limit_bytes=VMEM_LIMIT),
        name="post_final" if final_norm else "post",
    )(x, a, w_o, gt_m, g_ffn, sh_f, sc_f, gt_f, w_in, w_down, g_final)


def _rwkvproj_kernel(x_ref, g_ref, sh_ref, sc_ref, mu_ref, wr_ref, wk_ref, wv_ref,
                     w1_ref, w2_ref, a1_ref, a2_ref, g1_ref, g2_ref,
                     w0_ref, a0_ref, kk_ref, ka_ref,
                     r_ref, k_ref, v_ref, kkv_ref, a_ref, lw_ref, gate_ref, carry_ref):
    tm = x_ref.shape[1]

    @pl.when(pl.program_id(1) == 0)
    def _():
        carry_ref[...] = jnp.zeros_like(carry_ref)

    h = _norm_mod(x_ref[0], g_ref[...], sh_ref[0], sc_ref[0])
    rowi = lax.broadcasted_iota(jnp.int32, (tm, 1), 0)
    h_prev = jnp.where(rowi == 0, carry_ref[...], pltpu.roll(h, 1, 0))
    carry_ref[...] = h[tm - 1:tm, :]
    xx = h_prev - h
    mix = lambda i: (h + xx * mu_ref[i:i + 1, :]).astype(BF16)
    cw = RWKV_PROJ_COLS
    slabs = [slice(c * cw, (c + 1) * cw) for c in range(D // cw)]

    def project(lhs, w_ref, out_ref, which):
        for cols in which:
            out_ref[0, :, cols] = jnp.dot(lhs, w_ref[:, cols],
                                          preferred_element_type=F32).astype(out_ref.dtype)

    half0, half1 = slabs[:len(slabs) // 2], slabs[len(slabs) // 2:]
    xr = mix(0)
    xv = mix(3)
    project(xr, wr_ref, r_ref, half0)
    hid_g = _sigmoid(_mm(mix(5), g1_ref[...])).astype(BF16)
    project(xr, wr_ref, r_ref, half1)
    hid_w = jnp.tanh(_mm(mix(1), w1_ref[...])).astype(BF16)
    project(xv, wv_ref, v_ref, half0)
    hid_a = _mm(mix(4), a1_ref[...]).astype(BF16)
    project(xv, wv_ref, v_ref, half1)
    xk = mix(2)
    project(hid_g, g2_ref, gate_ref, slabs)
    for cols in slabs:
        k = jnp.dot(xk, wk_ref[:, cols], preferred_element_type=F32)
        a = _sigmoid(a0_ref[:, cols] + jnp.dot(hid_a, a2_ref[:, cols], preferred_element_type=F32))
        wl = w0_ref[:, cols] + jnp.dot(hid_w, w2_ref[:, cols], preferred_element_type=F32)
        k_ref[0, :, cols] = (k * (1.0 + (a - 1.0) * ka_ref[:, cols])).astype(k_ref.dtype)
        kkv_ref[0, :, cols] = k * kk_ref[:, cols]
        a_ref[0, :, cols] = a
        lw_ref[0, :, cols] = -math.exp(-0.5) * _sigmoid(wl)


def _rwkv_proj(x, g, sh, sc, mu, w_r, w_k, w_v, w1, w2, a1, a2, g1, g2, w0, a0, k_k, k_a):
    b, s, _ = x.shape
    tm = ROW_TILE
    row3 = lambda bi, si: (bi, 0, 0)
    const2 = lambda bi, si: (0, 0)
    tile3 = lambda bi, si: (bi, si, 0)
    full = lambda arr: pl.BlockSpec(arr.shape, const2)
    outs = pl.pallas_call(
        _rwkvproj_kernel,
        grid=(b, s // tm),
        in_specs=[pl.BlockSpec((1, tm, D), tile3), full(g),
                  pl.BlockSpec((1, 1, D), row3), pl.BlockSpec((1, 1, D), row3),
                  full(mu), full(w_r), full(w_k), full(w_v),
                  full(w1), full(w2), full(a1), full(a2), full(g1), full(g2),
                  full(w0), full(a0), full(k_k), full(k_a)],
        out_specs=[pl.BlockSpec((1, tm, D), tile3)] * 7,
        out_shape=[jax.ShapeDtypeStruct((b, s, D), dt) for dt in (BF16, BF16, BF16, F32, F32, F32, BF16)],
        scratch_shapes=[pltpu.VMEM((1, D), F32)],
        compiler_params=pltpu.CompilerParams(
            dimension_semantics=("arbitrary", "arbitrary"), vmem_limit_bytes=VMEM_LIMIT),
        name="rwkv_proj",
    )(x, g, sh, sc, mu, w_r, w_k, w_v, w1, w2, a1, a2, g1, g2, w0, a0, k_k, k_a)
    return outs


def _scan_kernel(r_ref, k_ref, v_ref, kkv_ref, a_ref, lw_ref, gate_ref,
                 rk_ref, lnw_ref, lnb_ref, o_ref, state_ref):
    tc = r_ref.shape[1]
    group_rows = SCAN_GROUP * CHUNK
    n_groups = tc // group_rows

    @pl.when(pl.program_id(2) == 0)
    def _():
        state_ref[...] = jnp.zeros_like(state_ref)

    li = lax.broadcasted_iota(jnp.int32, (PAIR, PAIR), 0)
    lj = lax.broadcasted_iota(jnp.int32, (PAIR, PAIR), 1)
    ones_bd = jnp.where((li >> LOG2_HEAD) == (lj >> LOG2_HEAD), 1.0, 0.0).astype(BF16)
    strict = (li & (CHUNK - 1)) > (lj & (CHUNK - 1))
    incl = (li & (CHUNK - 1)) >= (lj & (CHUNK - 1))
    eye = jnp.where(li == lj, 1.0, 0.0).astype(F32)
    tri_rows = 4 * CHUNK
    ti = lax.broadcasted_iota(jnp.int32, (tri_rows, tri_rows), 0)
    tj = lax.broadcasted_iota(jnp.int32, (tri_rows, tri_rows), 1)
    same_chunk = (ti >> LOG2_HEAD) == (tj >> LOG2_HEAD)
    tri = jnp.where((ti >= tj) & same_chunk, 1.0, 0.0).astype(BF16)
    is_h0 = lax.broadcasted_iota(jnp.int32, (1, PAIR), 1) < HEAD
    chunks = range(SCAN_GROUP)

    def stack(z):
        return jnp.concatenate([jnp.where(is_h0, z, 0.0), jnp.where(is_h0, 0.0, z)], axis=0)

    def state_free_part(g, out):
        rows = slice(g * group_rows, (g + 1) * group_rows)
        r = r_ref[0, rows, :].astype(F32)
        k = k_ref[0, rows, :].astype(F32)
        v = v_ref[0, rows, :].astype(F32)
        kkv = kkv_ref[0, rows, :]
        lw = lw_ref[0, rows, :]
        lw1 = lw.astype(BF16)
        lw2 = (lw - lw1.astype(F32)).astype(BF16)
        cl = jnp.concatenate(
            [jnp.dot(tri, lw1[i:i + tri_rows], preferred_element_type=F32)
             + jnp.dot(tri, lw2[i:i + tri_rows], preferred_element_type=F32)
             for i in range(0, group_rows, tri_rows)], axis=0)
        n2 = _mm(kkv * kkv, ones_bd)
        kk = kkv / jnp.maximum(jnp.sqrt(n2), 1e-12)
        bvec = kk * a_ref[0, rows, :]
        egi = jnp.exp(-cl)
        a_t = -kk * jnp.exp(cl - lw)
        b_t = bvec * egi
        k_t = k * egi
        r_t = r * jnp.exp(cl)
        out["bonus"] = _mm(r * k * rk_ref[...], ones_bd) * v
        sls = [slice(c * CHUNK, (c + 1) * CHUNK) for c in chunks]
        cl_end = [cl[(c + 1) * CHUNK - 1:(c + 1) * CHUNK, :] for c in chunks]
        out["decay"] = [jnp.exp(ce) for ce in cl_end]
        tails = [jnp.exp(cl_end[c] - cl[sls[c]]) for c in chunks]
        a_s = [stack(a_t[sl]) for sl in sls]
        r_s = [stack(r_t[sl]) for sl in sls]
        v_s = [stack(v[sl]) for sl in sls]
        bh_s = [stack(bvec[sls[c]] * tails[c]) for c in chunks]
        kh_s = [stack(k[sls[c]] * tails[c]) for c in chunks]
        grams = [_mm_nt(jnp.concatenate([a_s[c], r_s[c]], axis=0),
                        jnp.concatenate([stack(b_t[sls[c]]), stack(k_t[sls[c]])], axis=0))
                 for c in chunks]
        yield
        m_ak = [jnp.where(strict, gm[:PAIR, PAIR:], 0.0) for gm in grams]
        a_r = [jnp.concatenate([jnp.where(incl, gm[PAIR:, :PAIR], 0.0),
                                jnp.where(incl, gm[PAIR:, PAIR:], 0.0)], axis=1) for gm in grams]
        rhs = [jnp.concatenate([a_s[c], _mm(m_ak[c], v_s[c])], axis=1) for c in chunks]
        mps = [jnp.where(strict, gm[:PAIR, :PAIR], 0.0) for gm in grams]
        ts = [eye + mp for mp in mps]
        mps = [_mm(mp, mp) for mp in mps]
        yield
        for level in range(1, 6):
            if level < 5:
                res = [_mm(mps[c], jnp.concatenate([mps[c], ts[c]], axis=1)) for c in chunks]
                mps = [rs[:, :PAIR] for rs in res]
                ts = [ts[c] + res[c][:, PAIR:] for c in chunks]
            else:
                ts = [ts[c] + _mm(mps[c], ts[c]) for c in chunks]
            yield
        xs = [_mm(ts[c], rhs[c]) for c in chunks]
        gys = [_mm(a_r[c], jnp.concatenate(
            [xs[c], jnp.concatenate([jnp.zeros_like(v_s[c]), v_s[c]], axis=1)], axis=0))
               for c in chunks]
        out["g"] = [r_s[c] + gys[c][:, :PAIR] for c in chunks]
        out["y0"] = [gy[:, PAIR:] for gy in gys]
        yield
        out["pb"] = [_mm_tn(xs[c][:, :PAIR], bh_s[c]) for c in chunks]
        out["q"] = [_mm_tn(jnp.concatenate([xs[c][:, PAIR:], v_s[c]], axis=0),
                           jnp.concatenate([bh_s[c], kh_s[c]], axis=0)) for c in chunks]
        yield

    carried_state = [state_ref[...]]

    def state_part(g, res):
        ys = []
        for c in chunks:
            state = carried_state[0]
            y_s = _mm_nt(res["g"][c], state) + res["y0"][c]
            ys.append(y_s[:CHUNK] + y_s[CHUNK:])
            carried_state[0] = state * res["decay"][c] + _mm(state, res["pb"][c]) + res["q"][c]
            yield
        rows = slice(g * group_rows, (g + 1) * group_rows)
        y = jnp.concatenate(ys, axis=0)
        inv_n = 1.0 / HEAD
        mean = _mm(y, ones_bd) * inv_n
        yc = y - mean
        var = _mm(yc * yc, ones_bd) * inv_n
        yn = yc * lax.rsqrt(var + GN_EPS) * lnw_ref[...] + lnb_ref[...]
        o_ref[0, rows, :] = ((yn + res["bonus"]) * gate_ref[0, rows, :].astype(F32)
                             ).astype(o_ref.dtype)
        yield

    results = [dict() for _ in range(n_groups)]
    for _ in state_free_part(0, results[0]):
        pass
    for g in range(n_groups):
        carried = state_part(g, results[g])
        ahead = state_free_part(g + 1, results[g + 1]) if g + 1 < n_groups else iter(())
        n_carried, n_ahead = SCAN_GROUP + 1, 9
        for i, _ in enumerate(ahead):
            if (i * n_carried) // n_ahead != ((i + 1) * n_carried) // n_ahead:
                next(carried, None)
        for _ in carried:
            pass
    state_ref[...] = carried_state[0]


def _rwkv_scan(r, k, v, kkv, a, lw, gate, r_k, ln_w, ln_b):
    b, s, _ = r.shape
    tc = SCAN_BLOCK
    tile = pl.BlockSpec((1, tc, PAIR), lambda bi, p, ti: (bi, ti, p))
    vec = pl.BlockSpec((1, PAIR), lambda bi, p, ti: (0, p))
    return pl.pallas_call(
        _scan_kernel,
        grid=(b, D // PAIR, s // tc),
        in_specs=[tile] * 7 + [vec] * 3,
        out_specs=tile,
        out_shape=jax.ShapeDtypeStruct((b, s, D), BF16),
        scratch_shapes=[pltpu.VMEM((PAIR, PAIR), F32)],
        compiler_params=pltpu.CompilerParams(
            dimension_semantics=("arbitrary", "arbitrary", "arbitrary"),
            vmem_limit_bytes=VMEM_LIMIT),
        name="rwkv_scan",
    )(r, k, v, kkv, a, lw, gate, r_k, ln_w, ln_b)


def kernel(x, c, attn_w_in, attn_b_f, attn_w_out, rwkv_mu, rwkv_w_r, rwkv_w_k, rwkv_w_v, rwkv_w_o, rwkv_w0, rwkv_w1, rwkv_w2, rwkv_a0, rwkv_a1, rwkv_a2, rwkv_g1, rwkv_g2, rwkv_k_k, rwkv_k_a, rwkv_r_k, rwkv_ln_w, rwkv_ln_b, mod_w, mod_b, norm_mix, norm_ffn, ffn_w_in, ffn_w_out, norm_final):
    b, s, _ = x.shape
    mod = _modulation(c, mod_w, mod_b)
    mods = [[mod[l, :, i * D:(i + 1) * D].reshape(b, 1, D) for i in range(6)] for l in range(2)]
    row = lambda t: t.reshape(1, D)
    g_final = row(norm_final)

    sh_m, sc_m, gt_m, sh_f, sc_f, gt_f = mods[0]
    w_in = attn_w_in[0]
    qscale = HEAD ** -0.5 * LOG2E
    col_scale = jnp.ones((3 * D,), F32)
    col_scale = col_scale.at[0:FOX_W].set(qscale).at[3 * FOX_W:4 * FOX_W].set(qscale)
    w_main = (w_in[:, :3 * D] * col_scale[None, :]).astype(BF16)
    w_f = jnp.zeros((D, AUG), F32).at[:, :8].set(w_in[:, 3 * D:]).astype(BF16)
    b_f = jnp.zeros((1, AUG), F32).at[0, :8].set(attn_b_f[0])
    qa, ka, vt = _inproj(x, row(norm_mix[0]), sh_m, sc_m, w_main, w_f, b_f)
    o = _attention(qa, ka, vt)

    w_ffn_in = ffn_w_in.astype(BF16)
    w_ffn_out = ffn_w_out.astype(BF16)
    x = _post(x, o, attn_w_out[0].astype(BF16), gt_m, row(norm_ffn[0]), sh_f, sc_f, gt_f,
              w_ffn_in, w_ffn_out, 0, g_final, final_norm=False)

    sh_m, sc_m, gt_m, sh_f, sc_f, gt_f = mods[1]
    pad_in = lambda w, n: jnp.zeros((D, n), BF16).at[:, :w.shape[1]].set(w.astype(BF16))
    pad_out = lambda w, n: jnp.zeros((n, D), BF16).at[:w.shape[0], :].set(w.astype(BF16))
    r, k, v, kkv, a, lw, gate = _rwkv_proj(
        x, row(norm_mix[1]), sh_m, sc_m, rwkv_mu[0],
        rwkv_w_r[0].astype(BF16), rwkv_w_k[0].astype(BF16), rwkv_w_v[0].astype(BF16),
        pad_in(rwkv_w1[0], 128), pad_out(rwkv_w2[0], 128),
        pad_in(rwkv_a1[0], 128), pad_out(rwkv_a2[0], 128),
        pad_in(rwkv_g1[0], 256), pad_out(rwkv_g2[0], 256),
        row(rwkv_w0[0]), row(rwkv_a0[0]), row(rwkv_k_k[0]), row(rwkv_k_a[0]))
    y = _rwkv_scan(r, k, v, kkv, a, lw, gate, row(rwkv_r_k[0]), row(rwkv_ln_w[0]),
                   row(rwkv_ln_b[0]))
    x = _post(x, y, rwkv_w_o[0].astype(BF16), gt_m, row(norm_ffn[1]), sh_f, sc_f, gt_f,
              w_ffn_in, w_ffn_out, 1, g_final, final_norm=True)
    return x
```

```python
import functools
import math

import jax
import jax.numpy as jnp
import numpy as np
from jax import lax
from jax.experimental import pallas as pl
from jax.experimental.pallas import tpu as pltpu

F32 = jnp.float32
BF16 = jnp.bfloat16

D = 1024
HEAD = 64
PAIR = 2 * HEAD
N_PAIR_ATTN = 8
FOX_W = 512
D_FF = 2816
MXU_WIDTH = 256
FF_SLAB_EDGES = (0, 5 * MXU_WIDTH, D_FF)
MOBA_BLOCK = 256
N_MOBA_BLOCKS = 16
MOBA_TOPK = 3
AUG = 128
AUG_PER_HEAD = 16
NORM_EPS = 1e-6
GN_EPS = HEAD * 1e-5
LOG2E = 1.4426950408889634
NEG = -1e30
PEN = -30000.0
CHUNK = 64
SCAN_GROUP = 8
SCAN_BLOCK = 8 * SCAN_GROUP * CHUNK
BF16_SUBLANES = 16
SUM_ROWS = BF16_SUBLANES
ATTN_SUBSTEPS = 4
ATTN_TQ = 256
ATTN_FLUSH_TILES = 8
RWKV_PROJ_COLS = MXU_WIDTH
ROW_TILE = 512
MOD_COLS = 1536
LOG2_HEAD = 6
LOG2_MOBA_BLOCKS = 4
assert (1 << LOG2_HEAD, 1 << LOG2_MOBA_BLOCKS, CHUNK) == (HEAD, N_MOBA_BLOCKS, HEAD)
VMEM_LIMIT = 56 * 1024 * 1024


def _mm(a, b):
    return jnp.dot(a.astype(BF16), b.astype(BF16), preferred_element_type=F32)


def _mm_nt(a, b):
    return lax.dot_general(a.astype(BF16), b.astype(BF16), (((1,), (1,)), ((), ())),
                           preferred_element_type=F32)


def _mm_tn(a, b):
    return lax.dot_general(a.astype(BF16), b.astype(BF16), (((0,), (0,)), ((), ())),
                           preferred_element_type=F32)


def _sigmoid(x):
    return 1.0 / (1.0 + jnp.exp(-x))


def _softplus(x):
    return jnp.maximum(x, 0.0) + jnp.log(1.0 + jnp.exp(-jnp.abs(x)))


def _norm_mod(x, g, sh, sc):
    y = x * lax.rsqrt(jnp.mean(x * x, axis=-1, keepdims=True) + NORM_EPS)
    return (y * g) * (1.0 + sc) + sh


def _mod_kernel(c_ref, w_ref, b_ref, o_ref):
    c = c_ref[...]
    c1, c2, c3 = _split3(c * _sigmoid(c))
    w = w_ref[0]
    w1 = w.astype(BF16)
    w2 = (w - w1.astype(F32)).astype(BF16)
    dot = lambda a, b: jnp.dot(a, b, preferred_element_type=F32)
    o_ref[0] = (dot(c1, w1) + (dot(c2, w1) + dot(c1, w2))
                + (dot(c3, w1) + dot(c2, w2))) + b_ref[0]


def _modulation(c, mod_w, mod_b):
    depth, _, n = mod_w.shape
    b = c.shape[0]
    rows = BF16_SUBLANES
    c_pad = jnp.zeros((rows, D), F32).at[:b].set(c)
    tn = MOD_COLS
    out = pl.pallas_call(
        _mod_kernel,
        grid=(depth, n // tn),
        in_specs=[pl.BlockSpec((rows, D), lambda l, j: (0, 0)),
                  pl.BlockSpec((1, D, tn), lambda l, j: (l, 0, j)),
                  pl.BlockSpec((1, 1, tn), lambda l, j: (l, 0, j))],
        out_specs=pl.BlockSpec((1, rows, tn), lambda l, j: (l, 0, j)),
        out_shape=jax.ShapeDtypeStruct((depth, rows, n), F32),
        compiler_params=pltpu.CompilerParams(
            dimension_semantics=("arbitrary", "arbitrary"), vmem_limit_bytes=VMEM_LIMIT),
        name="mod",
    )(c_pad, mod_w, mod_b.reshape(depth, 1, n))
    return out[:, :b]


def _split3(z):
    z1 = z.astype(BF16)
    r1 = z - z1.astype(F32)
    z2 = r1.astype(BF16)
    return z1, z2, (r1 - z2.astype(F32)).astype(BF16)


def _mm_split3_lhs_exact(a_exact, b):
    aa = a_exact.astype(BF16)
    b1, b2, b3 = _split3(b)
    return (jnp.dot(aa, b1, preferred_element_type=F32)
            + jnp.dot(aa, b2, preferred_element_type=F32)
            + jnp.dot(aa, b3, preferred_element_type=F32))


def _moba_penalty_steps(gate, own, out):
    lane = lax.broadcasted_iota(jnp.int32, gate.shape, 1)
    n = lane & (N_MOBA_BLOCKS - 1)
    past = n < own
    g = jnp.where(past, gate, -jnp.inf)
    rank = jnp.zeros(gate.shape, jnp.int32)
    for k in range(1, N_MOBA_BLOCKS):
        gk = jnp.where(n >= k, pltpu.roll(g, k, 1), pltpu.roll(g, AUG - N_MOBA_BLOCKS + k, 1))
        beats = (gk > g) | ((gk == g) & (n >= k))
        rank = rank + beats.astype(jnp.int32)
        yield
    keep = (past & (rank < MOBA_TOPK)) | (n == own)
    out[0] = jnp.where(keep, 0.0, PEN)


def _inproj_kernel(x_ref, g_ref, sh_ref, sc_ref, w_ref, wf_ref, bf_ref, sel_c_ref, const_c_ref,
                   sel_p_ref, qa_ref, ka_ref, vt_ref, carry_ref, km_ref):
    tm = x_ref.shape[1]
    si = pl.program_id(1)

    @pl.when(si == 0)
    def _():
        carry_ref[...] = jnp.zeros_like(carry_ref)
        km_ref[...] = jnp.zeros_like(km_ref)

    h = _norm_mod(x_ref[0], g_ref[...], sh_ref[0], sc_ref[0]).astype(BF16)
    width = PAIR + AUG

    def proj(lo, hi):
        return jnp.dot(h, w_ref[:, lo:hi], preferred_element_type=F32)

    def put(ref, pair, value):
        if ref is qa_ref:
            ref[0, pair, 0:PAIR, :] = value.T.astype(BF16)
        else:
            ref[0, :, pair * width:pair * width + PAIR] = value.astype(BF16)

    def put_bias(ref, pair, value):
        if ref is qa_ref:
            ref[0, pair, PAIR:width, :] = value.T.astype(BF16)
        else:
            ref[0, :, pair * width + PAIR:(pair + 1) * width] = value.astype(BF16)

    q_mb = proj(3 * FOX_W, 4 * FOX_W)
    qb = q_mb.astype(BF16)
    km1, km2, km3 = _split3(km_ref[...])
    gate = (jnp.dot(qb, km1, preferred_element_type=F32)
            + jnp.dot(qb, km2, preferred_element_type=F32)
            + jnp.dot(qb, km3, preferred_element_type=F32))
    pen = [None]
    ranking = _moba_penalty_steps(gate, si, pen)

    def fox_bias():
        f = jnp.dot(h, wf_ref[...], preferred_element_type=F32) + bf_ref[...]
        lf = -_softplus(-f)
        row = lax.broadcasted_iota(jnp.int32, (tm, tm), 0)
        col = lax.broadcasted_iota(jnp.int32, (tm, tm), 1)
        tri = jnp.where(row >= col, 1.0, 0.0).astype(F32)
        cum = _mm_split3_lhs_exact(tri, lf) + carry_ref[...]
        carry_ref[...] = cum[tm - 1:tm, :]
        c3 = jnp.concatenate(_split3(cum * LOG2E), axis=1)
        aug = jnp.dot(c3, sel_c_ref[...], preferred_element_type=F32) + const_c_ref[...]
        for p in range(4):
            put_bias(qa_ref, p, aug[:, p * PAIR:(p + 1) * PAIR])
            put_bias(ka_ref, p, aug[:, FOX_W + p * PAIR:FOX_W + (p + 1) * PAIR])

    def fox_qk(half):
        q = proj(half * 2 * PAIR, (half + 1) * 2 * PAIR)
        k = proj(FOX_W + half * 2 * PAIR, FOX_W + (half + 1) * 2 * PAIR)
        for i in range(2):
            put(qa_ref, 2 * half + i, q[:, i * PAIR:(i + 1) * PAIR])
            put(ka_ref, 2 * half + i, k[:, i * PAIR:(i + 1) * PAIR])

    def values(group, half):
        lo = (2 + 3 * group) * FOX_W + half * 2 * PAIR
        v = proj(lo, lo + 2 * PAIR)
        first = 4 * group + 2 * half
        vt_ref[0, first:first + 2, 0] = v.T.reshape(2, PAIR, tm).astype(BF16)

    def moba_k():
        k_mb = proj(4 * FOX_W, 5 * FOX_W)
        lane_p = lax.broadcasted_iota(jnp.int32, (tm, AUG), 1)
        onehot = jnp.where((lane_p < 2 * AUG_PER_HEAD) & ((lane_p & (N_MOBA_BLOCKS - 1)) == si),
                           1.0, 0.0)
        for p in range(4):
            put(ka_ref, 4 + p, k_mb[:, p * PAIR:(p + 1) * PAIR])
            put_bias(ka_ref, 4 + p, onehot)
        kb_t = k_mb.astype(BF16).astype(F32).T
        place = jnp.where((lane_p & (N_MOBA_BLOCKS - 1)) == si, 1.0 / tm, 0.0)
        upd = jnp.dot(kb_t.astype(BF16), place.astype(BF16), preferred_element_type=F32)
        d_head = lax.broadcasted_iota(jnp.int32, (FOX_W, AUG), 0) >> LOG2_HEAD
        l_head = lax.broadcasted_iota(jnp.int32, (FOX_W, AUG), 1) >> LOG2_MOBA_BLOCKS
        km_ref[...] += jnp.where(d_head == l_head, upd, 0.0)

    others = [fox_bias, lambda: fox_qk(0), lambda: fox_qk(1), moba_k,
              lambda: values(0, 0), lambda: values(0, 1), lambda: values(1, 0),
              lambda: values(1, 1)]
    for i, _ in enumerate(ranking):
        if i % 2 == 0 and others:
            others.pop(0)()
    for task in others:
        task()
    aug_mb_q = jnp.dot(pen[0].astype(BF16), sel_p_ref[...], preferred_element_type=F32)
    for p in range(4):
        put(qa_ref, 4 + p, q_mb[:, p * PAIR:(p + 1) * PAIR])
        put_bias(qa_ref, 4 + p, aug_mb_q[:, p * PAIR:(p + 1) * PAIR])


def _bias_lane_selectors():
    sel_c = np.zeros((3 * AUG, 2 * FOX_W), np.float32)
    const_c = np.zeros((1, 2 * FOX_W), np.float32)
    sel_p = np.zeros((AUG, FOX_W), np.float32)
    for head in range(8):
        base = (head // 2) * AUG + (head % 2) * AUG_PER_HEAD
        for i in range(3):
            sel_c[i * AUG + head, base + i] = 1.0
            sel_c[i * AUG + head, FOX_W + base + 3 + i] = -1.0
            const_c[0, base + 3 + i] = 1.0
            const_c[0, FOX_W + base + i] = 1.0
        for n in range(N_MOBA_BLOCKS):
            sel_p[head * N_MOBA_BLOCKS + n, base + n] = 1.0
    return jnp.asarray(sel_c, BF16), jnp.asarray(const_c, F32), jnp.asarray(sel_p, BF16)


def _inproj(x, g, sh, sc, w_main, w_f, b_f):
    b, s, _ = x.shape
    tm = MOBA_BLOCK
    n = w_main.shape[1]
    width = N_PAIR_ATTN * (PAIR + AUG)
    sel_c, const_c, sel_p = _bias_lane_selectors()
    row3 = lambda bi, si: (bi, 0, 0)
    const2 = lambda bi, si: (0, 0)
    tile3 = lambda bi, si: (bi, si, 0)
    full = lambda arr: pl.BlockSpec(arr.shape, const2)
    return pl.pallas_call(
        _inproj_kernel,
        grid=(b, s // tm),
        in_specs=[pl.BlockSpec((1, tm, D), tile3),
                  pl.BlockSpec((1, D), const2),
                  pl.BlockSpec((1, 1, D), row3),
                  pl.BlockSpec((1, 1, D), row3),
                  pl.BlockSpec((D, n), const2),
                  pl.BlockSpec((D, AUG), const2),
                  pl.BlockSpec((1, AUG), const2),
                  full(sel_c), full(const_c), full(sel_p)],
        out_specs=[pl.BlockSpec((1, N_PAIR_ATTN, PAIR + AUG, tm), lambda bi, si: (bi, 0, 0, si)),
                   pl.BlockSpec((1, tm, width), tile3),
                   pl.BlockSpec((1, N_PAIR_ATTN, 1, PAIR, tm), lambda bi, si: (bi, 0, si, 0, 0))],
        out_shape=[jax.ShapeDtypeStruct((b, N_PAIR_ATTN, PAIR + AUG, s), BF16),
                   jax.ShapeDtypeStruct((b, s, width), BF16),
                   jax.ShapeDtypeStruct((b, N_PAIR_ATTN, s // tm, PAIR, tm), BF16)],
        scratch_shapes=[pltpu.VMEM((1, AUG), F32), pltpu.VMEM((FOX_W, AUG), F32)],
        compiler_params=pltpu.CompilerParams(
            dimension_semantics=("arbitrary", "arbitrary"), vmem_limit_bytes=VMEM_LIMIT),
        name="inproj",
    )(x, g, sh, sc, w_main, w_f, b_f, sel_c, const_c, sel_p)


def _attn_kernel(*refs):
    qa_refs = refs[:ATTN_SUBSTEPS]
    ka_ref, vt_ref, o_ref, s_even, s_odd, macc_ref, mprev_ref, acc_ref = refs[ATTN_SUBSTEPS:]
    tq = qa_refs[0].shape[3]
    tk = vt_ref.shape[4]
    unit = tq // tk
    nq = s_even.shape[1] // tq
    lane = lax.broadcasted_iota(jnp.int32, (1, PAIR + AUG), 1)
    head0 = (lane < HEAD) | ((lane >= PAIR) & (lane < PAIR + AUG_PER_HEAD))
    head1 = ((lane >= HEAD) & (lane < PAIR)) | (
        (lane >= PAIR + AUG_PER_HEAD) & (lane < PAIR + 2 * AUG_PER_HEAD))
    ones_rows = jnp.ones((SUM_ROWS, tk), BF16)
    for sub in range(ATTN_SUBSTEPS):
        j = ATTN_SUBSTEPS * (pl.program_id(2) - 1) + 1 + sub
        s_cur, s_prev = (s_odd, s_even) if sub % 2 == 0 else (s_even, s_odd)
        _attn_substep(j, qa_refs[sub][0, 0], s_cur, s_prev, o_ref.at[0, sub * tq:(sub + 1) * tq, :],
                      ka_ref, vt_ref, macc_ref, mprev_ref, acc_ref, head0, head1, ones_rows,
                      tq, tk, unit, nq)


def _attn_substep(j, qa, s_cur, s_prev, o_ref, ka_ref, vt_ref, macc_ref, mprev_ref, acc_ref,
                  head0, head1, ones_rows, tq, tk, unit, nq):
    m_prev = mprev_ref[...]
    macc_ref[...] = jnp.full(macc_ref.shape, NEG, F32)
    acc_ref[...] = jnp.zeros_like(acc_ref)

    def score_tiles(s_ref, first_tile, n, diagonal):
        rows = n * tk
        start = pl.multiple_of(first_tile * tk, tk)
        k = ka_ref[0, pl.ds(start, rows), :]
        kz = jnp.zeros_like(k)
        k2 = jnp.concatenate([jnp.where(head0, k, kz), jnp.where(head1, k, kz)], axis=0)
        st = jnp.dot(k2, qa, preferred_element_type=F32)
        full = rows - tq if diagonal else rows
        for h in range(2):
            macc = macc_ref[h]
            if full:
                part = st[h * rows:h * rows + full]
                s_ref[h, pl.ds(start, full), :] = part
                macc = jnp.maximum(macc, jnp.max(part.reshape(full // 8, 8, tq), axis=0))
            if diagonal:
                key_pos = lax.broadcasted_iota(jnp.int32, (tq, tq), 0)
                qry_pos = lax.broadcasted_iota(jnp.int32, (tq, tq), 1)
                part = jnp.where(key_pos <= qry_pos, st[h * rows + full:(h + 1) * rows], NEG)
                s_ref[h, pl.ds(start + full, tq), :] = part
                macc = jnp.maximum(macc, jnp.max(part.reshape(tq // 8, 8, tq), axis=0))
            macc_ref[h] = macc

    def pv_tile(s_ref, t):
        start = pl.multiple_of(t * tk, tk)
        vt = vt_ref[0, 0, t]
        accs = []
        for h in range(2):
            p = jnp.exp2(s_ref[h, pl.ds(start, tk), :] - m_prev[h])
            lhs = jnp.concatenate([vt[h * HEAD:(h + 1) * HEAD, :], ones_rows], axis=0)
            accs.append(jnp.dot(lhs, p.astype(BF16), preferred_element_type=F32))
        return jnp.stack(accs)

    def pv_tiles(s_ref, first_tile, n):
        parts = [pv_tile(s_ref, first_tile + t) for t in range(n)]
        while len(parts) > 1:
            parts = [a + b for a, b in zip(parts[0::2], parts[1::2])]
        acc_ref[...] += parts[0]

    bit = nq // 2
    while bit >= 1:
        in_group = (j > 0) & (j < nq) & ((j & bit) != 0)
        lowest = (j & (bit - 1)) == 0
        first = (j & (-2 * bit)) * unit
        n = bit * unit

        if bit > 1:
            @pl.when(in_group & jnp.logical_not(lowest))
            def _(n=n, first=first):
                score_tiles(s_cur, first, n, False)
                pv_tiles(s_prev, first, n)

        @pl.when(in_group & lowest)
        def _(n=n, first=first):
            score_tiles(s_cur, first, n + unit, True)
            pv_tiles(s_prev, first, n)
        bit //= 2

    @pl.when(j == 0)
    def _():
        score_tiles(s_cur, 0, unit, True)

    @pl.when(j == nq)
    def _():
        for first in range(0, nq * unit, ATTN_FLUSH_TILES):
            pv_tiles(s_prev, first, ATTN_FLUSH_TILES)

    mprev_ref[...] = jnp.max(macc_ref[...], axis=1, keepdims=True)

    @pl.when(j > 0)
    def _():
        acc = acc_ref[...]
        ot = jnp.concatenate([acc[h, :HEAD] / acc[h, HEAD:HEAD + 1] for h in range(2)], axis=0)
        o_ref[...] = ot.T.astype(o_ref.dtype)


def _attention(qa, ka, vt):
    b, _, _, s = qa.shape
    tq = ATTN_TQ
    nk, tk = vt.shape[2], vt.shape[4]
    nq = s // tq
    assert nq & (nq - 1) == 0 and (nq * tq // tk) % ATTN_FLUSH_TILES == 0, "query tiles: a power of two"
    n = ATTN_SUBSTEPS
    assert n % 2 == 0 and nq % n == 0

    def qa_spec(sub):
        tile = lambda t: jnp.clip(n * (t - 1) + 1 + sub, 0, nq - 1)
        return pl.BlockSpec((1, 1, PAIR + AUG, tq), lambda bi, p, t: (bi, p, 0, tile(t)))

    return pl.pallas_call(
        _attn_kernel,
        grid=(b, N_PAIR_ATTN, nq // n + 1),
        in_specs=[qa_spec(sub) for sub in range(n)] + [
            pl.BlockSpec((1, s, PAIR + AUG), lambda bi, p, t: (bi, 0, p)),
            pl.BlockSpec((1, 1, nk, PAIR, tk), lambda bi, p, t: (bi, p, 0, 0, 0))],
        out_specs=pl.BlockSpec((1, n * tq, PAIR), lambda bi, p, t: (bi, jnp.maximum(t - 1, 0), p)),
        out_shape=jax.ShapeDtypeStruct((b, s, N_PAIR_ATTN * PAIR), BF16),
        scratch_shapes=[pltpu.VMEM((2, s, tq), F32),
                        pltpu.VMEM((2, s, tq), F32),
                        pltpu.VMEM((2, 8, tq), F32),
                        pltpu.VMEM((2, 1, tq), F32),
                        pltpu.VMEM((2, HEAD + SUM_ROWS, tq), F32)],
        compiler_params=pltpu.CompilerParams(
            dimension_semantics=("arbitrary", "arbitrary", "arbitrary"),
            vmem_limit_bytes=VMEM_LIMIT),
        name="attn",
    )(*([qa] * n), ka, vt)


def _post_kernel(x_ref, a_ref, wo_ref, gtm_ref, g_ref, sh_ref, sc_ref, gtf_ref,
                 win_ref, wd_ref, gend_ref, shn_ref, scn_ref, o_ref, *hn_ref, final_norm):
    x1 = x_ref[0] + gtm_ref[0] * jnp.dot(a_ref[0], wo_ref[...], preferred_element_type=F32)
    h = _norm_mod(x1, g_ref[...], sh_ref[0], sc_ref[0]).astype(BF16)
    ff = jnp.zeros_like(x1)
    for lo, hi in zip(FF_SLAB_EDGES[:-1], FF_SLAB_EDGES[1:]):
        gate = jnp.dot(h, win_ref[0, :, lo:hi], preferred_element_type=F32)
        up = jnp.dot(h, win_ref[0, :, D_FF + lo:D_FF + hi], preferred_element_type=F32)
        act = (gate * _sigmoid(gate) * up).astype(BF16)
        ff = ff + jnp.dot(act, wd_ref[0, lo:hi, :], preferred_element_type=F32)
    x2 = x1 + gtf_ref[0] * ff
    if final_norm:
        x2 = x2 * lax.rsqrt(jnp.mean(x2 * x2, axis=-1, keepdims=True) + NORM_EPS) * gend_ref[...]
    else:
        hn_ref[0][0] = _norm_mod(x2, gend_ref[...], shn_ref[0], scn_ref[0])
    o_ref[0] = x2


def _post(x, a, w_o, gt_m, g_ffn, sh_f, sc_f, gt_f, w_in, w_down, layer, g_end, sh_next, sc_next,
          final_norm):
    b, s, _ = x.shape
    tm = ROW_TILE
    row3 = lambda bi, si: (bi, 0, 0)
    const2 = lambda bi, si: (0, 0)
    tile3 = lambda bi, si: (bi, si, 0)
    of_layer = lambda bi, si: (layer, 0, 0)
    once = pl.Buffered(1)
    kern = functools.partial(_post_kernel, final_norm=final_norm)
    n_out = 1 if final_norm else 2
    return pl.pallas_call(
        kern,
        grid=(b, s // tm),
        in_specs=[pl.BlockSpec((1, tm, D), tile3),
                  pl.BlockSpec((1, tm, D), tile3),
                  pl.BlockSpec((D, D), const2),
                  pl.BlockSpec((1, 1, D), row3),
                  pl.BlockSpec((1, D), const2),
                  pl.BlockSpec((1, 1, D), row3),
                  pl.BlockSpec((1, 1, D), row3),
                  pl.BlockSpec((1, 1, D), row3),
                  pl.BlockSpec((1, D, 2 * D_FF), of_layer, pipeline_mode=once),
                  pl.BlockSpec((1, D_FF, D), of_layer, pipeline_mode=once),
                  pl.BlockSpec((1, D), const2),
                  pl.BlockSpec((1, 1, D), row3),
                  pl.BlockSpec((1, 1, D), row3)],
        out_specs=[pl.BlockSpec((1, tm, D), tile3)] * n_out,
        out_shape=[jax.ShapeDtypeStruct((b, s, D), F32)] * n_out,
        compiler_params=pltpu.CompilerParams(
            dimension_semantics=("arbitrary", "arbitrary"), vmem_limit_bytes=VMEM_LIMIT),
        name="post_final" if final_norm else "post",
    )(x, a, w_o, gt_m, g_ffn, sh_f, sc_f, gt_f, w_in, w_down, g_end, sh_next, sc_next)


def _rwkvproj_kernel(h_ref, mu_ref, wr_ref, wk_ref, wv_ref,
                     w1_ref, w2_ref, a1_ref, a2_ref, g1_ref, g2_ref,
                     w0_ref, a0_ref, kk_ref, ka_ref,
                     r_ref, k_ref, v_ref, kkv_ref, a_ref, lw_ref, gate_ref, carry_ref):
    tm = h_ref.shape[1]

    @pl.when(pl.program_id(1) == 0)
    def _():
        carry_ref[...] = jnp.zeros_like(carry_ref)

    h = h_ref[0]
    rowi = lax.broadcasted_iota(jnp.int32, (tm, 1), 0)
    h_prev = jnp.where(rowi == 0, carry_ref[...], pltpu.roll(h, 1, 0))
    carry_ref[...] = h[tm - 1:tm, :]
    xx = h_prev - h
    mix = lambda i: (h + xx * mu_ref[i:i + 1, :]).astype(BF16)
    cw = RWKV_PROJ_COLS
    slabs = [slice(c * cw, (c + 1) * cw) for c in range(D // cw)]

    def project(lhs, w_ref, out_ref, which):
        for cols in which:
            out_ref[0, :, cols] = jnp.dot(lhs, w_ref[:, cols],
                                          preferred_element_type=F32).astype(out_ref.dtype)

    half0, half1 = slabs[:len(slabs) // 2], slabs[len(slabs) // 2:]
    xr = mix(0)
    xv = mix(3)
    project(xr, wr_ref, r_ref, half0)
    hid_g = _sigmoid(_mm(mix(5), g1_ref[...])).astype(BF16)
    project(xr, wr_ref, r_ref, half1)
    hid_w = jnp.tanh(_mm(mix(1), w1_ref[...])).astype(BF16)
    project(xv, wv_ref, v_ref, half0)
    hid_a = _mm(mix(4), a1_ref[...]).astype(BF16)
    project(xv, wv_ref, v_ref, half1)
    xk = mix(2)
    project(hid_g, g2_ref, gate_ref, slabs)
    for cols in slabs:
        k = jnp.dot(xk, wk_ref[:, cols], preferred_element_type=F32)
        a = _sigmoid(a0_ref[:, cols] + jnp.dot(hid_a, a2_ref[:, cols], preferred_element_type=F32))
        wl = w0_ref[:, cols] + jnp.dot(hid_w, w2_ref[:, cols], preferred_element_type=F32)
        k_ref[0, :, cols] = (k * (1.0 + (a - 1.0) * ka_ref[:, cols])).astype(k_ref.dtype)
        kkv_ref[0, :, cols] = k * kk_ref[:, cols]
        a_ref[0, :, cols] = a
        lw_ref[0, :, cols] = -math.exp(-0.5) * _sigmoid(wl)


def _rwkv_proj(h, mu, w_r, w_k, w_v, w1, w2, a1, a2, g1, g2, w0, a0, k_k, k_a):
    b, s, _ = h.shape
    tm = ROW_TILE
    const2 = lambda bi, si: (0, 0)
    tile3 = lambda bi, si: (bi, si, 0)
    full = lambda arr: pl.BlockSpec(arr.shape, const2)
    outs = pl.pallas_call(
        _rwkvproj_kernel,
        grid=(b, s // tm),
        in_specs=[pl.BlockSpec((1, tm, D), tile3),
                  full(mu), full(w_r), full(w_k), full(w_v),
                  full(w1), full(w2), full(a1), full(a2), full(g1), full(g2),
                  full(w0), full(a0), full(k_k), full(k_a)],
        out_specs=[pl.BlockSpec((1, tm, D), tile3)] * 7,
        out_shape=[jax.ShapeDtypeStruct((b, s, D), dt) for dt in (BF16, BF16, BF16, F32, F32, F32, BF16)],
        scratch_shapes=[pltpu.VMEM((1, D), F32)],
        compiler_params=pltpu.CompilerParams(
            dimension_semantics=("arbitrary", "arbitrary"), vmem_limit_bytes=VMEM_LIMIT),
        name="rwkv_proj",
    )(h, mu, w_r, w_k, w_v, w1, w2, a1, a2, g1, g2, w0, a0, k_k, k_a)
    return outs


def _scan_kernel(r_ref, k_ref, v_ref, kkv_ref, a_ref, lw_ref, gate_ref,
                 rk_ref, lnw_ref, lnb_ref, o_ref, state_ref):
    tc = r_ref.shape[1]
    group_rows = SCAN_GROUP * CHUNK
    n_groups = tc // group_rows

    @pl.when(pl.program_id(2) == 0)
    def _():
        state_ref[...] = jnp.zeros_like(state_ref)

    li = lax.broadcasted_iota(jnp.int32, (PAIR, PAIR), 0)
    lj = lax.broadcasted_iota(jnp.int32, (PAIR, PAIR), 1)
    ones_bd = jnp.where((li >> LOG2_HEAD) == (lj >> LOG2_HEAD), 1.0, 0.0).astype(BF16)
    strict = (li & (CHUNK - 1)) > (lj & (CHUNK - 1))
    incl = (li & (CHUNK - 1)) >= (lj & (CHUNK - 1))
    eye = jnp.where(li == lj, 1.0, 0.0).astype(F32)
    tri_rows = 4 * CHUNK
    ti = lax.broadcasted_iota(jnp.int32, (tri_rows, tri_rows), 0)
    tj = lax.broadcasted_iota(jnp.int32, (tri_rows, tri_rows), 1)
    same_chunk = (ti >> LOG2_HEAD) == (tj >> LOG2_HEAD)
    tri = jnp.where((ti >= tj) & same_chunk, 1.0, 0.0).astype(BF16)
    is_h0 = lax.broadcasted_iota(jnp.int32, (1, PAIR), 1) < HEAD
    chunks = range(SCAN_GROUP)

    def stack(z):
        return jnp.concatenate([jnp.where(is_h0, z, 0.0), jnp.where(is_h0, 0.0, z)], axis=0)

    def state_free_part(g, out):
        rows = slice(g * group_rows, (g + 1) * group_rows)
        r = r_ref[0, rows, :].astype(F32)
        k = k_ref[0, rows, :].astype(F32)
        v = v_ref[0, rows, :].astype(F32)
        kkv = kkv_ref[0, rows, :]
        lw = lw_ref[0, rows, :]
        lw1 = lw.astype(BF16)
        lw2 = (lw - lw1.astype(F32)).astype(BF16)
        cl = jnp.concatenate(
            [jnp.dot(tri, lw1[i:i + tri_rows], preferred_element_type=F32)
             + jnp.dot(tri, lw2[i:i + tri_rows], preferred_element_type=F32)
             for i in range(0, group_rows, tri_rows)], axis=0)
        n2 = _mm(kkv * kkv, ones_bd)
        kk = kkv / jnp.maximum(jnp.sqrt(n2), 1e-12)
        bvec = kk * a_ref[0, rows, :]
        egi = jnp.exp(-cl)
        a_t = -kk * jnp.exp(cl - lw)
        b_t = bvec * egi
        k_t = k * egi
        r_t = r * jnp.exp(cl)
        out["bonus"] = _mm(r * k * rk_ref[...], ones_bd) * v
        sls = [slice(c * CHUNK, (c + 1) * CHUNK) for c in chunks]
        cl_end = [cl[(c + 1) * CHUNK - 1:(c + 1) * CHUNK, :] for c in chunks]
        out["decay"] = [jnp.exp(ce) for ce in cl_end]
        tails = [jnp.exp(cl_end[c] - cl[sls[c]]) for c in chunks]
        a_s = [stack(a_t[sl]) for sl in sls]
        r_s = [stack(r_t[sl]) for sl in sls]
        v_s = [stack(v[sl]) for sl in sls]
        bh_s = [stack(bvec[sls[c]] * tails[c]) for c in chunks]
        kh_s = [stack(k[sls[c]] * tails[c]) for c in chunks]
        grams = [_mm_nt(jnp.concatenate([a_s[c], r_s[c]], axis=0),
                        jnp.concatenate([stack(b_t[sls[c]]), stack(k_t[sls[c]])], axis=0))
                 for c in chunks]
        yield
        m_ak = [jnp.where(strict, gm[:PAIR, PAIR:], 0.0) for gm in grams]
        a_r = [jnp.concatenate([jnp.where(incl, gm[PAIR:, :PAIR], 0.0),
                                jnp.where(incl, gm[PAIR:, PAIR:], 0.0)], axis=1) for gm in grams]
        rhs = [jnp.concatenate([a_s[c], _mm(m_ak[c], v_s[c])], axis=1) for c in chunks]
        mps = [jnp.where(strict, gm[:PAIR, :PAIR], 0.0) for gm in grams]
        ts = [eye + mp for mp in mps]
        mps = [_mm(mp, mp) for mp in mps]
        yield
        for level in range(1, 6):
            if level < 5:
                res = [_mm(mps[c], jnp.concatenate([mps[c], ts[c]], axis=1)) for c in chunks]
                mps = [rs[:, :PAIR] for rs in res]
                ts = [ts[c] + res[c][:, PAIR:] for c in chunks]
            else:
                ts = [ts[c] + _mm(mps[c], ts[c]) for c in chunks]
            yield
        xs = [_mm(ts[c], rhs[c]) for c in chunks]
        gys = [_mm(a_r[c], jnp.concatenate(
            [xs[c], jnp.concatenate([jnp.zeros_like(v_s[c]), v_s[c]], axis=1)], axis=0))
               for c in chunks]
        out["g"] = [r_s[c] + gys[c][:, :PAIR] for c in chunks]
        out["y0"] = [gy[:, PAIR:] for gy in gys]
        yield
        out["pb"] = [_mm_tn(xs[c][:, :PAIR], bh_s[c]) for c in chunks]
        out["q"] = [_mm_tn(jnp.concatenate([xs[c][:, PAIR:], v_s[c]], axis=0),
                           jnp.concatenate([bh_s[c], kh_s[c]], axis=0)) for c in chunks]
        yield

    carried_state = [state_ref[...]]

    def state_part(g, res):
        ys = []
        for c in chunks:
            state = carried_state[0]
            y_s = _mm_nt(res["g"][c], state) + res["y0"][c]
            ys.append(y_s[:CHUNK] + y_s[CHUNK:])
            carried_state[0] = state * res["decay"][c] + _mm(state, res["pb"][c]) + res["q"][c]
            yield
        rows = slice(g * group_rows, (g + 1) * group_rows)
        y = jnp.concatenate(ys, axis=0)
        inv_n = 1.0 / HEAD
        mean = _mm(y, ones_bd) * inv_n
        yc = y - mean
        var = _mm(yc * yc, ones_bd) * inv_n
        yn = yc * lax.rsqrt(var + GN_EPS) * lnw_ref[...] + lnb_ref[...]
        o_ref[0, rows, :] = ((yn + res["bonus"]) * gate_ref[0, rows, :].astype(F32)
                             ).astype(o_ref.dtype)
        yield

    results = [dict() for _ in range(n_groups)]
    for _ in state_free_part(0, results[0]):
        pass
    for g in range(n_groups):
        carried = state_part(g, results[g])
        ahead = state_free_part(g + 1, results[g + 1]) if g + 1 < n_groups else iter(())
        n_carried, n_ahead = SCAN_GROUP + 1, 9
        for i, _ in enumerate(ahead):
            if (i * n_carried) // n_ahead != ((i + 1) * n_carried) // n_ahead:
                next(carried, None)
        for _ in carried:
            pass
    state_ref[...] = carried_state[0]


def _rwkv_scan(r, k, v, kkv, a, lw, gate, r_k, ln_w, ln_b):
    b, s, _ = r.shape
    tc = SCAN_BLOCK
    tile = pl.BlockSpec((1, tc, PAIR), lambda bi, p, ti: (bi, ti, p))
    vec = pl.BlockSpec((1, PAIR), lambda bi, p, ti: (0, p))
    return pl.pallas_call(
        _scan_kernel,
        grid=(b, D // PAIR, s // tc),
        in_specs=[tile] * 7 + [vec] * 3,
        out_specs=tile,
        out_shape=jax.ShapeDtypeStruct((b, s, D), BF16),
        scratch_shapes=[pltpu.VMEM((PAIR, PAIR), F32)],
        compiler_params=pltpu.CompilerParams(
            dimension_semantics=("arbitrary", "arbitrary", "arbitrary"),
            vmem_limit_bytes=VMEM_LIMIT),
        name="rwkv_scan",
    )(r, k, v, kkv, a, lw, gate, r_k, ln_w, ln_b)


def kernel(x, c, attn_w_in, attn_b_f, attn_w_out, rwkv_mu, rwkv_w_r, rwkv_w_k, rwkv_w_v, rwkv_w_o, rwkv_w0, rwkv_w1, rwkv_w2, rwkv_a0, rwkv_a1, rwkv_a2, rwkv_g1, rwkv_g2, rwkv_k_k, rwkv_k_a, rwkv_r_k, rwkv_ln_w, rwkv_ln_b, mod_w, mod_b, norm_mix, norm_ffn, ffn_w_in, ffn_w_out, norm_final):
    b, s, _ = x.shape
    mod = _modulation(c, mod_w, mod_b)
    mods = [[mod[l, :, i * D:(i + 1) * D].reshape(b, 1, D) for i in range(6)] for l in range(2)]
    row = lambda t: t.reshape(1, D)
    g_final = row(norm_final)

    sh_m, sc_m, gt_m, sh_f, sc_f, gt_f = mods[0]
    w_in = attn_w_in[0]
    qscale = HEAD ** -0.5 * LOG2E
    col_scale = jnp.ones((3 * D,), F32)
    col_scale = col_scale.at[0:FOX_W].set(qscale).at[3 * FOX_W:4 * FOX_W].set(qscale)
    w_main = (w_in[:, :3 * D] * col_scale[None, :]).astype(BF16)
    w_f = jnp.zeros((D, AUG), F32).at[:, :8].set(w_in[:, 3 * D:]).astype(BF16)
    b_f = jnp.zeros((1, AUG), F32).at[0, :8].set(attn_b_f[0])
    qa, ka, vt = _inproj(x, row(norm_mix[0]), sh_m, sc_m, w_main, w_f, b_f)
    o = _attention(qa, ka, vt)

    w_ffn_in = ffn_w_in.astype(BF16)
    w_ffn_out = ffn_w_out.astype(BF16)
    x, h1 = _post(x, o, attn_w_out[0].astype(BF16), gt_m, row(norm_ffn[0]), sh_f, sc_f, gt_f,
                  w_ffn_in, w_ffn_out, 0, row(norm_mix[1]), mods[1][0], mods[1][1],
                  final_norm=False)

    sh_m, sc_m, gt_m, sh_f, sc_f, gt_f = mods[1]
    pad_in = lambda w, n: jnp.zeros((D, n), BF16).at[:, :w.shape[1]].set(w.astype(BF16))
    pad_out = lambda w, n: jnp.zeros((n, D), BF16).at[:w.shape[0], :].set(w.astype(BF16))
    r, k, v, kkv, a, lw, gate = _rwkv_proj(
        h1, rwkv_mu[0],
        rwkv_w_r[0].astype(BF16), rwkv_w_k[0].astype(BF16), rwkv_w_v[0].astype(BF16),
        pad_in(rwkv_w1[0], 128), pad_out(rwkv_w2[0], 128),
        pad_in(rwkv_a1[0], 128), pad_out(rwkv_a2[0], 128),
        pad_in(rwkv_g1[0], 256), pad_out(rwkv_g2[0], 256),
        row(rwkv_w0[0]), row(rwkv_a0[0]), row(rwkv_k_k[0]), row(rwkv_k_a[0]))
    y = _rwkv_scan(r, k, v, kkv, a, lw, gate, row(rwkv_r_k[0]), row(rwkv_ln_w[0]),
                   row(rwkv_ln_b[0]))
    (x,) = _post(x, y, rwkv_w_o[0].astype(BF16), gt_m, row(norm_ffn[1]), sh_f, sc_f, gt_f,
                 w_ffn_in, w_ffn_out, 1, g_final, sh_f, sc_f, final_norm=True)
    return x
```

```python
import functools
import math

import jax
import jax.numpy as jnp
import numpy as np
from jax import lax
from jax.experimental import pallas as pl
from jax.experimental.pallas import tpu as pltpu

F32 = jnp.float32
BF16 = jnp.bfloat16

D = 1024
HEAD = 64
PAIR = 2 * HEAD
N_PAIR_ATTN = 8
FOX_W = 512
D_FF = 2816
MXU_WIDTH = 256
FF_SLAB_EDGES = (0, 5 * MXU_WIDTH, D_FF)
MOBA_BLOCK = 256
N_MOBA_BLOCKS = 16
MOBA_TOPK = 3
AUG = 128
AUG_PER_HEAD = 16
NORM_EPS = 1e-6
GN_EPS = HEAD * 1e-5
LOG2E = 1.4426950408889634
NEG = -1e30
PEN = -30000.0
CHUNK = 64
SCAN_GROUP = 8
SCAN_BLOCK = 8 * SCAN_GROUP * CHUNK
BF16_SUBLANES = 16
SUM_ROWS = BF16_SUBLANES
ATTN_SUBSTEPS = 4
ATTN_TQ = 256
ATTN_FLUSH_TILES = 8
RWKV_PROJ_COLS = MXU_WIDTH
ROW_TILE = 512
MOD_COLS = 1536
LOG2_HEAD = 6
LOG2_MOBA_BLOCKS = 4
assert (1 << LOG2_HEAD, 1 << LOG2_MOBA_BLOCKS, CHUNK) == (HEAD, N_MOBA_BLOCKS, HEAD)
VMEM_LIMIT = 56 * 1024 * 1024


def _mm(a, b):
    return jnp.dot(a.astype(BF16), b.astype(BF16), preferred_element_type=F32)


def _mm_nt(a, b):
    return lax.dot_general(a.astype(BF16), b.astype(BF16), (((1,), (1,)), ((), ())),
                           preferred_element_type=F32)


def _mm_tn(a, b):
    return lax.dot_general(a.astype(BF16), b.astype(BF16), (((0,), (0,)), ((), ())),
                           preferred_element_type=F32)


def _sigmoid(x):
    return 1.0 / (1.0 + jnp.exp(-x))


def _softplus(x):
    return jnp.maximum(x, 0.0) + jnp.log(1.0 + jnp.exp(-jnp.abs(x)))


def _norm_mod(x, g, sh, sc):
    y = x * lax.rsqrt(jnp.mean(x * x, axis=-1, keepdims=True) + NORM_EPS)
    return (y * g) * (1.0 + sc) + sh


def _mod_kernel(c_ref, w_ref, b_ref, o_ref):
    c = c_ref[...]
    c1, c2, c3 = _split3(c * _sigmoid(c))
    w = w_ref[0]
    w1 = w.astype(BF16)
    w2 = (w - w1.astype(F32)).astype(BF16)
    n = c.shape[0]
    stacked = jnp.concatenate([c1, c2, c3], axis=0)
    p1 = jnp.dot(stacked, w1, preferred_element_type=F32)
    p2 = jnp.dot(stacked[:2 * n], w2, preferred_element_type=F32)
    o_ref[0] = (p1[:n] + (p1[n:2 * n] + p2[:n]) + (p1[2 * n:] + p2[n:])) + b_ref[0]


def _modulation(c, mod_w, mod_b):
    depth, _, n = mod_w.shape
    b = c.shape[0]
    rows = BF16_SUBLANES
    c_pad = jnp.zeros((rows, D), F32).at[:b].set(c)
    tn = MOD_COLS
    out = pl.pallas_call(
        _mod_kernel,
        grid=(depth, n // tn),
        in_specs=[pl.BlockSpec((rows, D), lambda l, j: (0, 0)),
                  pl.BlockSpec((1, D, tn), lambda l, j: (l, 0, j)),
                  pl.BlockSpec((1, 1, tn), lambda l, j: (l, 0, j))],
        out_specs=pl.BlockSpec((1, rows, tn), lambda l, j: (l, 0, j)),
        out_shape=jax.ShapeDtypeStruct((depth, rows, n), F32),
        compiler_params=pltpu.CompilerParams(
            dimension_semantics=("arbitrary", "arbitrary"), vmem_limit_bytes=VMEM_LIMIT),
        name="mod",
    )(c_pad, mod_w, mod_b.reshape(depth, 1, n))
    return out[:, :b]


def _split3(z):
    z1 = z.astype(BF16)
    r1 = z - z1.astype(F32)
    z2 = r1.astype(BF16)
    return z1, z2, (r1 - z2.astype(F32)).astype(BF16)


def _mm_split3_lhs_exact(a_exact, b):
    aa = a_exact.astype(BF16)
    b1, b2, b3 = _split3(b)
    return (jnp.dot(aa, b1, preferred_element_type=F32)
            + jnp.dot(aa, b2, preferred_element_type=F32)
            + jnp.dot(aa, b3, preferred_element_type=F32))


def _moba_penalty_steps(gate, own, out):
    lane = lax.broadcasted_iota(jnp.int32, gate.shape, 1)
    n = lane & (N_MOBA_BLOCKS - 1)
    past = n < own
    g = jnp.where(past, gate, -jnp.inf)
    rank = jnp.zeros(gate.shape, jnp.int32)
    for k in range(1, N_MOBA_BLOCKS):
        gk = jnp.where(n >= k, pltpu.roll(g, k, 1), pltpu.roll(g, AUG - N_MOBA_BLOCKS + k, 1))
        beats = (gk > g) | ((gk == g) & (n >= k))
        rank = rank + beats.astype(jnp.int32)
        yield
    keep = (past & (rank < MOBA_TOPK)) | (n == own)
    out[0] = jnp.where(keep, 0.0, PEN)


def _inproj_kernel(x_ref, g_ref, sh_ref, sc_ref, w_ref, wf_ref, bf_ref, sel_c_ref, const_c_ref,
                   sel_p_ref, qa_ref, ka_ref, vt_ref, carry_ref, km_ref):
    tm = x_ref.shape[1]
    si = pl.program_id(1)

    @pl.when(si == 0)
    def _():
        carry_ref[...] = jnp.zeros_like(carry_ref)
        km_ref[...] = jnp.zeros_like(km_ref)

    h = _norm_mod(x_ref[0], g_ref[...], sh_ref[0], sc_ref[0]).astype(BF16)
    width = PAIR + AUG

    def proj(lo, hi):
        return jnp.dot(h, w_ref[:, lo:hi], preferred_element_type=F32)

    def put(ref, pair, value):
        if ref is qa_ref:
            ref[0, pair, 0:PAIR, :] = value.T.astype(BF16)
        else:
            ref[0, :, pair * width:pair * width + PAIR] = value.astype(BF16)

    def put_bias(ref, pair, value):
        if ref is qa_ref:
            ref[0, pair, PAIR:width, :] = value.T.astype(BF16)
        else:
            ref[0, :, pair * width + PAIR:(pair + 1) * width] = value.astype(BF16)

    q_mb = proj(3 * FOX_W, 4 * FOX_W)
    qb = q_mb.astype(BF16)
    km1, km2, km3 = _split3(km_ref[...])
    gate = (jnp.dot(qb, km1, preferred_element_type=F32)
            + jnp.dot(qb, km2, preferred_element_type=F32)
            + jnp.dot(qb, km3, preferred_element_type=F32))
    pen = [None]
    ranking = _moba_penalty_steps(gate, si, pen)

    def fox_bias():
        f = jnp.dot(h, wf_ref[...], preferred_element_type=F32) + bf_ref[...]
        lf = -_softplus(-f)
        row = lax.broadcasted_iota(jnp.int32, (tm, tm), 0)
        col = lax.broadcasted_iota(jnp.int32, (tm, tm), 1)
        tri = jnp.where(row >= col, 1.0, 0.0).astype(F32)
        cum = _mm_split3_lhs_exact(tri, lf) + carry_ref[...]
        carry_ref[...] = cum[tm - 1:tm, :]
        c3 = jnp.concatenate(_split3(cum * LOG2E), axis=1)
        aug = jnp.dot(c3, sel_c_ref[...], preferred_element_type=F32) + const_c_ref[...]
        for p in range(4):
            put_bias(qa_ref, p, aug[:, p * PAIR:(p + 1) * PAIR])
            put_bias(ka_ref, p, aug[:, FOX_W + p * PAIR:FOX_W + (p + 1) * PAIR])

    def fox_qk(half):
        q = proj(half * 2 * PAIR, (half + 1) * 2 * PAIR)
        k = proj(FOX_W + half * 2 * PAIR, FOX_W + (half + 1) * 2 * PAIR)
        for i in range(2):
            put(qa_ref, 2 * half + i, q[:, i * PAIR:(i + 1) * PAIR])
            put(ka_ref, 2 * half + i, k[:, i * PAIR:(i + 1) * PAIR])

    def values(group, half):
        lo = (2 + 3 * group) * FOX_W + half * 2 * PAIR
        v = proj(lo, lo + 2 * PAIR)
        first = 4 * group + 2 * half
        vt_ref[0, first:first + 2, 0] = v.T.reshape(2, PAIR, tm).astype(BF16)

    def moba_k():
        k_mb = proj(4 * FOX_W, 5 * FOX_W)
        lane_p = lax.broadcasted_iota(jnp.int32, (tm, AUG), 1)
        onehot = jnp.where((lane_p < 2 * AUG_PER_HEAD) & ((lane_p & (N_MOBA_BLOCKS - 1)) == si),
                           1.0, 0.0)
        for p in range(4):
            put(ka_ref, 4 + p, k_mb[:, p * PAIR:(p + 1) * PAIR])
            put_bias(ka_ref, 4 + p, onehot)
        kb_t = k_mb.astype(BF16).astype(F32).T
        place = jnp.where((lane_p & (N_MOBA_BLOCKS - 1)) == si, 1.0 / tm, 0.0)
        upd = jnp.dot(kb_t.astype(BF16), place.astype(BF16), preferred_element_type=F32)
        d_head = lax.broadcasted_iota(jnp.int32, (FOX_W, AUG), 0) >> LOG2_HEAD
        l_head = lax.broadcasted_iota(jnp.int32, (FOX_W, AUG), 1) >> LOG2_MOBA_BLOCKS
        km_ref[...] += jnp.where(d_head == l_head, upd, 0.0)

    others = [fox_bias, lambda: fox_qk(0), lambda: fox_qk(1), moba_k,
              lambda: values(0, 0), lambda: values(0, 1), lambda: values(1, 0),
              lambda: values(1, 1)]
    for i, _ in enumerate(ranking):
        if i % 2 == 0 and others:
            others.pop(0)()
    for task in others:
        task()
    aug_mb_q = jnp.dot(pen[0].astype(BF16), sel_p_ref[...], preferred_element_type=F32)
    for p in range(4):
        put(qa_ref, 4 + p, q_mb[:, p * PAIR:(p + 1) * PAIR])
        put_bias(qa_ref, 4 + p, aug_mb_q[:, p * PAIR:(p + 1) * PAIR])


def _bias_lane_selectors():
    sel_c = np.zeros((3 * AUG, 2 * FOX_W), np.float32)
    const_c = np.zeros((1, 2 * FOX_W), np.float32)
    sel_p = np.zeros((AUG, FOX_W), np.float32)
    for head in range(8):
        base = (head // 2) * AUG + (head % 2) * AUG_PER_HEAD
        for i in range(3):
            sel_c[i * AUG + head, base + i] = 1.0
            sel_c[i * AUG + head, FOX_W + base + 3 + i] = -1.0
            const_c[0, base + 3 + i] = 1.0
            const_c[0, FOX_W + base + i] = 1.0
        for n in range(N_MOBA_BLOCKS):
            sel_p[head * N_MOBA_BLOCKS + n, base + n] = 1.0
    return jnp.asarray(sel_c, BF16), jnp.asarray(const_c, F32), jnp.asarray(sel_p, BF16)


def _inproj(x, g, sh, sc, w_main, w_f, b_f):
    b, s, _ = x.shape
    tm = MOBA_BLOCK
    n = w_main.shape[1]
    width = N_PAIR_ATTN * (PAIR + AUG)
    sel_c, const_c, sel_p = _bias_lane_selectors()
    row3 = lambda bi, si: (bi, 0, 0)
    const2 = lambda bi, si: (0, 0)
    tile3 = lambda bi, si: (bi, si, 0)
    full = lambda arr: pl.BlockSpec(arr.shape, const2)
    return pl.pallas_call(
        _inproj_kernel,
        grid=(b, s // tm),
        in_specs=[pl.BlockSpec((1, tm, D), tile3),
                  pl.BlockSpec((1, D), const2),
                  pl.BlockSpec((1, 1, D), row3),
                  pl.BlockSpec((1, 1, D), row3),
                  pl.BlockSpec((D, n), const2),
                  pl.BlockSpec((D, AUG), const2),
                  pl.BlockSpec((1, AUG), const2),
                  full(sel_c), full(const_c), full(sel_p)],
        out_specs=[pl.BlockSpec((1, N_PAIR_ATTN, PAIR + AUG, tm), lambda bi, si: (bi, 0, 0, si)),
                   pl.BlockSpec((1, tm, width), tile3),
                   pl.BlockSpec((1, N_PAIR_ATTN, 1, PAIR, tm), lambda bi, si: (bi, 0, si, 0, 0))],
        out_shape=[jax.ShapeDtypeStruct((b, N_PAIR_ATTN, PAIR + AUG, s), BF16),
                   jax.ShapeDtypeStruct((b, s, width), BF16),
                   jax.ShapeDtypeStruct((b, N_PAIR_ATTN, s // tm, PAIR, tm), BF16)],
        scratch_shapes=[pltpu.VMEM((1, AUG), F32), pltpu.VMEM((FOX_W, AUG), F32)],
        compiler_params=pltpu.CompilerParams(
            dimension_semantics=("arbitrary", "arbitrary"), vmem_limit_bytes=VMEM_LIMIT),
        name="inproj",
    )(x, g, sh, sc, w_main, w_f, b_f, sel_c, const_c, sel_p)


def _attn_kernel(*refs):
    qa_refs = refs[:ATTN_SUBSTEPS]
    ka_ref, vt_ref, o_ref, s_even, s_odd, macc_ref, mprev_ref, acc_ref = refs[ATTN_SUBSTEPS:]
    tq = qa_refs[0].shape[3]
    tk = vt_ref.shape[4]
    unit = tq // tk
    nq = s_even.shape[1] // tq
    lane = lax.broadcasted_iota(jnp.int32, (1, PAIR + AUG), 1)
    head0 = (lane < HEAD) | ((lane >= PAIR) & (lane < PAIR + AUG_PER_HEAD))
    head1 = ((lane >= HEAD) & (lane < PAIR)) | (
        (lane >= PAIR + AUG_PER_HEAD) & (lane < PAIR + 2 * AUG_PER_HEAD))
    ones_rows = jnp.ones((SUM_ROWS, tk), BF16)
    for sub in range(ATTN_SUBSTEPS):
        j = ATTN_SUBSTEPS * (pl.program_id(2) - 1) + 1 + sub
        s_cur, s_prev = (s_odd, s_even) if sub % 2 == 0 else (s_even, s_odd)
        _attn_substep(j, qa_refs[sub][0, 0], s_cur, s_prev, o_ref.at[0, sub * tq:(sub + 1) * tq, :],
                      ka_ref, vt_ref, macc_ref, mprev_ref, acc_ref, head0, head1, ones_rows,
                      tq, tk, unit, nq)


def _attn_substep(j, qa, s_cur, s_prev, o_ref, ka_ref, vt_ref, macc_ref, mprev_ref, acc_ref,
                  head0, head1, ones_rows, tq, tk, unit, nq):
    m_prev = mprev_ref[...]
    macc_ref[...] = jnp.full(macc_ref.shape, NEG, F32)
    acc_ref[...] = jnp.zeros_like(acc_ref)

    def score_tiles(s_ref, first_tile, n, diagonal):
        rows = n * tk
        start = pl.multiple_of(first_tile * tk, tk)
        k = ka_ref[0, pl.ds(start, rows), :]
        kz = jnp.zeros_like(k)
        k2 = jnp.concatenate([jnp.where(head0, k, kz), jnp.where(head1, k, kz)], axis=0)
        st = jnp.dot(k2, qa, preferred_element_type=F32)
        full = rows - tq if diagonal else rows
        for h in range(2):
            macc = macc_ref[h]
            if full:
                part = st[h * rows:h * rows + full]
                s_ref[h, pl.ds(start, full), :] = part
                macc = jnp.maximum(macc, jnp.max(part.reshape(full // 8, 8, tq), axis=0))
            if diagonal:
                key_pos = lax.broadcasted_iota(jnp.int32, (tq, tq), 0)
                qry_pos = lax.broadcasted_iota(jnp.int32, (tq, tq), 1)
                part = jnp.where(key_pos <= qry_pos, st[h * rows + full:(h + 1) * rows], NEG)
                s_ref[h, pl.ds(start + full, tq), :] = part
                macc = jnp.maximum(macc, jnp.max(part.reshape(tq // 8, 8, tq), axis=0))
            macc_ref[h] = macc

    def pv_tile(s_ref, t):
        start = pl.multiple_of(t * tk, tk)
        vt = vt_ref[0, 0, t]
        accs = []
        for h in range(2):
            p = jnp.exp2(s_ref[h, pl.ds(start, tk), :] - m_prev[h])
            lhs = jnp.concatenate([vt[h * HEAD:(h + 1) * HEAD, :], ones_rows], axis=0)
            accs.append(jnp.dot(lhs, p.astype(BF16), preferred_element_type=F32))
        return jnp.stack(accs)

    def pv_tiles(s_ref, first_tile, n):
        parts = [pv_tile(s_ref, first_tile + t) for t in range(n)]
        while len(parts) > 1:
            parts = [a + b for a, b in zip(parts[0::2], parts[1::2])]
        acc_ref[...] += parts[0]

    bit = nq // 2
    while bit >= 1:
        in_group = (j > 0) & (j < nq) & ((j & bit) != 0)
        lowest = (j & (bit - 1)) == 0
        first = (j & (-2 * bit)) * unit
        n = bit * unit

        if bit > 1:
            @pl.when(in_group & jnp.logical_not(lowest))
            def _(n=n, first=first):
                score_tiles(s_cur, first, n, False)
                pv_tiles(s_prev, first, n)

        @pl.when(in_group & lowest)
        def _(n=n, first=first):
            score_tiles(s_cur, first, n + unit, True)
            pv_tiles(s_prev, first, n)
        bit //= 2

    @pl.when(j == 0)
    def _():
        score_tiles(s_cur, 0, unit, True)

    @pl.when(j == nq)
    def _():
        for first in range(0, nq * unit, ATTN_FLUSH_TILES):
            pv_tiles(s_prev, first, ATTN_FLUSH_TILES)

    mprev_ref[...] = jnp.max(macc_ref[...], axis=1, keepdims=True)

    @pl.when(j > 0)
    def _():
        acc = acc_ref[...]
        ot = jnp.concatenate([acc[h, :HEAD] / acc[h, HEAD:HEAD + 1] for h in range(2)], axis=0)
        o_ref[...] = ot.T.astype(o_ref.dtype)


def _attention(qa, ka, vt):
    b, _, _, s = qa.shape
    tq = ATTN_TQ
    nk, tk = vt.shape[2], vt.shape[4]
    nq = s // tq
    assert nq & (nq - 1) == 0 and (nq * tq // tk) % ATTN_FLUSH_TILES == 0, "query tiles: a power of two"
    n = ATTN_SUBSTEPS
    assert n % 2 == 0 and nq % n == 0

    def qa_spec(sub):
        tile = lambda t: jnp.clip(n * (t - 1) + 1 + sub, 0, nq - 1)
        return pl.BlockSpec((1, 1, PAIR + AUG, tq), lambda bi, p, t: (bi, p, 0, tile(t)))

    return pl.pallas_call(
        _attn_kernel,
        grid=(b, N_PAIR_ATTN, nq // n + 1),
        in_specs=[qa_spec(sub) for sub in range(n)] + [
            pl.BlockSpec((1, s, PAIR + AUG), lambda bi, p, t: (bi, 0, p)),
            pl.BlockSpec((1, 1, nk, PAIR, tk), lambda bi, p, t: (bi, p, 0, 0, 0))],
        out_specs=pl.BlockSpec((1, n * tq, PAIR), lambda bi, p, t: (bi, jnp.maximum(t - 1, 0), p)),
        out_shape=jax.ShapeDtypeStruct((b, s, N_PAIR_ATTN * PAIR), BF16),
        scratch_shapes=[pltpu.VMEM((2, s, tq), F32),
                        pltpu.VMEM((2, s, tq), F32),
                        pltpu.VMEM((2, 8, tq), F32),
                        pltpu.VMEM((2, 1, tq), F32),
                        pltpu.VMEM((2, HEAD + SUM_ROWS, tq), F32)],
        compiler_params=pltpu.CompilerParams(
            dimension_semantics=("arbitrary", "arbitrary", "arbitrary"),
            vmem_limit_bytes=VMEM_LIMIT),
        name="attn",
    )(*([qa] * n), ka, vt)


def _post_kernel(x_ref, a_ref, wo_ref, gtm_ref, g_ref, sh_ref, sc_ref, gtf_ref,
                 win_ref, wd_ref, gfin_ref, o_ref, *, final_norm):
    x1 = x_ref[0] + gtm_ref[0] * jnp.dot(a_ref[0], wo_ref[...], preferred_element_type=F32)
    h = _norm_mod(x1, g_ref[...], sh_ref[0], sc_ref[0]).astype(BF16)
    ff = jnp.zeros_like(x1)
    for lo, hi in zip(FF_SLAB_EDGES[:-1], FF_SLAB_EDGES[1:]):
        gate = jnp.dot(h, win_ref[0, :, lo:hi], preferred_element_type=F32)
        up = jnp.dot(h, win_ref[0, :, D_FF + lo:D_FF + hi], preferred_element_type=F32)
        act = (gate * _sigmoid(gate) * up).astype(BF16)
        ff = ff + jnp.dot(act, wd_ref[0, lo:hi, :], preferred_element_type=F32)
    x2 = x1 + gtf_ref[0] * ff
    if final_norm:
        x2 = x2 * lax.rsqrt(jnp.mean(x2 * x2, axis=-1, keepdims=True) + NORM_EPS) * gfin_ref[...]
    o_ref[0] = x2


def _post(x, a, w_o, gt_m, g_ffn, sh_f, sc_f, gt_f, w_in, w_down, layer, g_final, final_norm):
    b, s, _ = x.shape
    tm = ROW_TILE
    row3 = lambda bi, si: (bi, 0, 0)
    const2 = lambda bi, si: (0, 0)
    tile3 = lambda bi, si: (bi, si, 0)
    of_layer = lambda bi, si: (layer, 0, 0)
    once = pl.Buffered(1)
    kern = functools.partial(_post_kernel, final_norm=final_norm)
    return pl.pallas_call(
        kern,
        grid=(b, s // tm),
        in_specs=[pl.BlockSpec((1, tm, D), tile3),
                  pl.BlockSpec((1, tm, D), tile3),
                  pl.BlockSpec((D, D), const2),
                  pl.BlockSpec((1, 1, D), row3),
                  pl.BlockSpec((1, D), const2),
                  pl.BlockSpec((1, 1, D), row3),
                  pl.BlockSpec((1, 1, D), row3),
                  pl.BlockSpec((1, 1, D), row3),
                  pl.BlockSpec((1, D, 2 * D_FF), of_layer, pipeline_mode=once),
                  pl.BlockSpec((1, D_FF, D), of_layer, pipeline_mode=once),
                  pl.BlockSpec((1, D), const2)],
        out_specs=pl.BlockSpec((1, tm, D), tile3),
        out_shape=jax.ShapeDtypeStruct((b, s, D), F32),
        compiler_params=pltpu.CompilerParams(
            dimension_semantics=("arbitrary", "arbitrary"), vmem_limit_bytes=VMEM_LIMIT),
        name="post_final" if final_norm else "post",
    )(x, a, w_o, gt_m, g_ffn, sh_f, sc_f, gt_f, w_in, w_down, g_final)


def _rwkvproj_kernel(x_ref, g_ref, sh_ref, sc_ref, mu_ref, wr_ref, wk_ref, wv_ref,
                     w1_ref, w2_ref, a1_ref, a2_ref, g1_ref, g2_ref,
                     w0_ref, a0_ref, kk_ref, ka_ref,
                     r_ref, k_ref, v_ref, kkv_ref, a_ref, lw_ref, gate_ref, carry_ref):
    tm = x_ref.shape[1]

    @pl.when(pl.program_id(1) == 0)
    def _():
        carry_ref[...] = jnp.zeros_like(carry_ref)

    h = _norm_mod(x_ref[0], g_ref[...], sh_ref[0], sc_ref[0])
    rowi = lax.broadcasted_iota(jnp.int32, (tm, 1), 0)
    h_prev = jnp.where(rowi == 0, carry_ref[...], pltpu.roll(h, 1, 0))
    carry_ref[...] = h[tm - 1:tm, :]
    xx = h_prev - h
    mix = lambda i: (h + xx * mu_ref[i:i + 1, :]).astype(BF16)
    cw = RWKV_PROJ_COLS
    slabs = [slice(c * cw, (c + 1) * cw) for c in range(D // cw)]

    def project(lhs, w_ref, out_ref, which):
        for cols in which:
            out_ref[0, :, cols] = jnp.dot(lhs, w_ref[:, cols],
                                          preferred_element_type=F32).astype(out_ref.dtype)

    half0, half1 = slabs[:len(slabs) // 2], slabs[len(slabs) // 2:]
    xr = mix(0)
    xv = mix(3)
    project(xr, wr_ref, r_ref, half0)
    hid_g = _sigmoid(_mm(mix(5), g1_ref[...])).astype(BF16)
    project(xr, wr_ref, r_ref, half1)
    hid_w = jnp.tanh(_mm(mix(1), w1_ref[...])).astype(BF16)
    project(xv, wv_ref, v_ref, half0)
    hid_a = _mm(mix(4), a1_ref[...]).astype(BF16)
    project(xv, wv_ref, v_ref, half1)
    xk = mix(2)
    project(hid_g, g2_ref, gate_ref, slabs)
    for cols in slabs:
        k = jnp.dot(xk, wk_ref[:, cols], preferred_element_type=F32)
        a = _sigmoid(a0_ref[:, cols] + jnp.dot(hid_a, a2_ref[:, cols], preferred_element_type=F32))
        wl = w0_ref[:, cols] + jnp.dot(hid_w, w2_ref[:, cols], preferred_element_type=F32)
        k_ref[0, :, cols] = (k * (1.0 + (a - 1.0) * ka_ref[:, cols])).astype(k_ref.dtype)
        kkv_ref[0, :, cols] = k * kk_ref[:, cols]
        a_ref[0, :, cols] = a
        lw_ref[0, :, cols] = -math.exp(-0.5) * _sigmoid(wl)


def _rwkv_proj(x, g, sh, sc, mu, w_r, w_k, w_v, w1, w2, a1, a2, g1, g2, w0, a0, k_k, k_a):
    b, s, _ = x.shape
    tm = ROW_TILE
    row3 = lambda bi, si: (bi, 0, 0)
    const2 = lambda bi, si: (0, 0)
    tile3 = lambda bi, si: (bi, si, 0)
    full = lambda arr: pl.BlockSpec(arr.shape, const2)
    outs = pl.pallas_call(
        _rwkvproj_kernel,
        grid=(b, s // tm),
        in_specs=[pl.BlockSpec((1, tm, D), tile3), full(g),
                  pl.BlockSpec((1, 1, D), row3), pl.BlockSpec((1, 1, D), row3),
                  full(mu), full(w_r), full(w_k), full(w_v),
                  full(w1), full(w2), full(a1), full(a2), full(g1), full(g2),
                  full(w0), full(a0), full(k_k), full(k_a)],
        out_specs=[pl.BlockSpec((1, tm, D), tile3)] * 7,
        out_shape=[jax.ShapeDtypeStruct((b, s, D), dt) for dt in (BF16, BF16, BF16, F32, F32, F32, BF16)],
        scratch_shapes=[pltpu.VMEM((1, D), F32)],
        compiler_params=pltpu.CompilerParams(
            dimension_semantics=("arbitrary", "arbitrary"), vmem_limit_bytes=VMEM_LIMIT),
        name="rwkv_proj",
    )(x, g, sh, sc, mu, w_r, w_k, w_v, w1, w2, a1, a2, g1, g2, w0, a0, k_k, k_a)
    return outs


def _scan_kernel(r_ref, k_ref, v_ref, kkv_ref, a_ref, lw_ref, gate_ref,
                 rk_ref, lnw_ref, lnb_ref, o_ref, state_ref):
    tc = r_ref.shape[1]
    group_rows = SCAN_GROUP * CHUNK
    n_groups = tc // group_rows

    @pl.when(pl.program_id(2) == 0)
    def _():
        state_ref[...] = jnp.zeros_like(state_ref)

    li = lax.broadcasted_iota(jnp.int32, (PAIR, PAIR), 0)
    lj = lax.broadcasted_iota(jnp.int32, (PAIR, PAIR), 1)
    ones_bd = jnp.where((li >> LOG2_HEAD) == (lj >> LOG2_HEAD), 1.0, 0.0).astype(BF16)
    strict = (li & (CHUNK - 1)) > (lj & (CHUNK - 1))
    incl = (li & (CHUNK - 1)) >= (lj & (CHUNK - 1))
    eye = jnp.where(li == lj, 1.0, 0.0).astype(F32)
    tri_rows = 4 * CHUNK
    ti = lax.broadcasted_iota(jnp.int32, (tri_rows, tri_rows), 0)
    tj = lax.broadcasted_iota(jnp.int32, (tri_rows, tri_rows), 1)
    same_chunk = (ti >> LOG2_HEAD) == (tj >> LOG2_HEAD)
    tri = jnp.where((ti >= tj) & same_chunk, 1.0, 0.0).astype(BF16)
    is_h0 = lax.broadcasted_iota(jnp.int32, (1, PAIR), 1) < HEAD
    chunks = range(SCAN_GROUP)

    def stack(z):
        return jnp.concatenate([jnp.where(is_h0, z, 0.0), jnp.where(is_h0, 0.0, z)], axis=0)

    def state_free_part(g, out):
        rows = slice(g * group_rows, (g + 1) * group_rows)
        r = r_ref[0, rows, :].astype(F32)
        k = k_ref[0, rows, :].astype(F32)
        v = v_ref[0, rows, :].astype(F32)
        kkv = kkv_ref[0, rows, :]
        lw = lw_ref[0, rows, :]
        lw1 = lw.astype(BF16)
        lw2 = (lw - lw1.astype(F32)).astype(BF16)
        cl = jnp.concatenate(
            [jnp.dot(tri, lw1[i:i + tri_rows], preferred_element_type=F32)
             + jnp.dot(tri, lw2[i:i + tri_rows], preferred_element_type=F32)
             for i in range(0, group_rows, tri_rows)], axis=0)
        n2 = _mm(kkv * kkv, ones_bd)
        kk = kkv / jnp.maximum(jnp.sqrt(n2), 1e-12)
        bvec = kk * a_ref[0, rows, :]
        egi = jnp.exp(-cl)
        a_t = -kk * jnp.exp(cl - lw)
        b_t = bvec * egi
        k_t = k * egi
        r_t = r * jnp.exp(cl)
        out["bonus"] = _mm(r * k * rk_ref[...], ones_bd) * v
        sls = [slice(c * CHUNK, (c + 1) * CHUNK) for c in chunks]
        cl_end = [cl[(c + 1) * CHUNK - 1:(c + 1) * CHUNK, :] for c in chunks]
        out["decay"] = [jnp.exp(ce) for ce in cl_end]
        tails = [jnp.exp(cl_end[c] - cl[sls[c]]) for c in chunks]
        a_s = [stack(a_t[sl]) for sl in sls]
        r_s = [stack(r_t[sl]) for sl in sls]
        v_s = [stack(v[sl]) for sl in sls]
        bh_s = [stack(bvec[sls[c]] * tails[c]) for c in chunks]
        kh_s = [stack(k[sls[c]] * tails[c]) for c in chunks]
        grams = [_mm_nt(jnp.concatenate([a_s[c], r_s[c]], axis=0),
                        jnp.concatenate([stack(b_t[sls[c]]), stack(k_t[sls[c]])], axis=0))
                 for c in chunks]
        yield
        m_ak = [jnp.where(strict, gm[:PAIR, PAIR:], 0.0) for gm in grams]
        a_r = [jnp.concatenate([jnp.where(incl, gm[PAIR:, :PAIR], 0.0),
                                jnp.where(incl, gm[PAIR:, PAIR:], 0.0)], axis=1) for gm in grams]
        rhs = [jnp.concatenate([a_s[c], _mm(m_ak[c], v_s[c])], axis=1) for c in chunks]
        mps = [jnp.where(strict, gm[:PAIR, :PAIR], 0.0) for gm in grams]
        ts = [eye + mp for mp in mps]
        mps = [_mm(mp, mp) for mp in mps]
        yield
        for level in range(1, 6):
            if level < 5:
                res = [_mm(mps[c], jnp.concatenate([mps[c], ts[c]], axis=1)) for c in chunks]
                mps = [rs[:, :PAIR] for rs in res]
                ts = [ts[c] + res[c][:, PAIR:] for c in chunks]
            else:
                ts = [ts[c] + _mm(mps[c], ts[c]) for c in chunks]
            yield
        xs = [_mm(ts[c], rhs[c]) for c in chunks]
        gys = [_mm(a_r[c], jnp.concatenate(
            [xs[c], jnp.concatenate([jnp.zeros_like(v_s[c]), v_s[c]], axis=1)], axis=0))
               for c in chunks]
        out["g"] = [r_s[c] + gys[c][:, :PAIR] for c in chunks]
        out["y0"] = [gy[:, PAIR:] for gy in gys]
        yield
        out["pb"] = [_mm_tn(xs[c][:, :PAIR], bh_s[c]) for c in chunks]
        out["q"] = [_mm_tn(jnp.concatenate([xs[c][:, PAIR:], v_s[c]], axis=0),
                           jnp.concatenate([bh_s[c], kh_s[c]], axis=0)) for c in chunks]
        yield

    carried_state = [state_ref[...]]

    def state_part(g, res):
        ys = []
        for c in chunks:
            state = carried_state[0]
            y_s = _mm_nt(res["g"][c], state) + res["y0"][c]
            ys.append(y_s[:CHUNK] + y_s[CHUNK:])
            carried_state[0] = state * res["decay"][c] + _mm(state, res["pb"][c]) + res["q"][c]
            yield
        rows = slice(g * group_rows, (g + 1) * group_rows)
        y = jnp.concatenate(ys, axis=0)
        inv_n = 1.0 / HEAD
        mean = _mm(y, ones_bd) * inv_n
        yc = y - mean
        var = _mm(yc * yc, ones_bd) * inv_n
        yn = yc * lax.rsqrt(var + GN_EPS) * lnw_ref[...] + lnb_ref[...]
        o_ref[0, rows, :] = ((yn + res["bonus"]) * gate_ref[0, rows, :].astype(F32)
                             ).astype(o_ref.dtype)
        yield

    results = [dict() for _ in range(n_groups)]
    for _ in state_free_part(0, results[0]):
        pass
    for g in range(n_groups):
        carried = state_part(g, results[g])
        ahead = state_free_part(g + 1, results[g + 1]) if g + 1 < n_groups else iter(())
        n_carried, n_ahead = SCAN_GROUP + 1, 9
        for i, _ in enumerate(ahead):
            if (i * n_carried) // n_ahead != ((i + 1) * n_carried) // n_ahead:
                next(carried, None)
        for _ in carried:
            pass
    state_ref[...] = carried_state[0]


def _rwkv_scan(r, k, v, kkv, a, lw, gate, r_k, ln_w, ln_b):
    b, s, _ = r.shape
    tc = SCAN_BLOCK
    tile = pl.BlockSpec((1, tc, PAIR), lambda bi, p, ti: (bi, ti, p))
    vec = pl.BlockSpec((1, PAIR), lambda bi, p, ti: (0, p))
    return pl.pallas_call(
        _scan_kernel,
        grid=(b, D // PAIR, s // tc),
        in_specs=[tile] * 7 + [vec] * 3,
        out_specs=tile,
        out_shape=jax.ShapeDtypeStruct((b, s, D), BF16),
        scratch_shapes=[pltpu.VMEM((PAIR, PAIR), F32)],
        compiler_params=pltpu.CompilerParams(
            dimension_semantics=("arbitrary", "arbitrary", "arbitrary"),
            vmem_limit_bytes=VMEM_LIMIT),
        name="rwkv_scan",
    )(r, k, v, kkv, a, lw, gate, r_k, ln_w, ln_b)


def kernel(x, c, attn_w_in, attn_b_f, attn_w_out, rwkv_mu, rwkv_w_r, rwkv_w_k, rwkv_w_v, rwkv_w_o, rwkv_w0, rwkv_w1, rwkv_w2, rwkv_a0, rwkv_a1, rwkv_a2, rwkv_g1, rwkv_g2, rwkv_k_k, rwkv_k_a, rwkv_r_k, rwkv_ln_w, rwkv_ln_b, mod_w, mod_b, norm_mix, norm_ffn, ffn_w_in, ffn_w_out, norm_final):
    b, s, _ = x.shape
    mod = _modulation(c, mod_w, mod_b)
    mods = [[mod[l, :, i * D:(i + 1) * D].reshape(b, 1, D) for i in range(6)] for l in range(2)]
    row = lambda t: t.reshape(1, D)
    g_final = row(norm_final)

    sh_m, sc_m, gt_m, sh_f, sc_f, gt_f = mods[0]
    w_in = attn_w_in[0]
    qscale = HEAD ** -0.5 * LOG2E
    col_scale = jnp.ones((3 * D,), F32)
    col_scale = col_scale.at[0:FOX_W].set(qscale).at[3 * FOX_W:4 * FOX_W].set(qscale)
    w_main = (w_in[:, :3 * D] * col_scale[None, :]).astype(BF16)
    w_f = jnp.zeros((D, AUG), F32).at[:, :8].set(w_in[:, 3 * D:]).astype(BF16)
    b_f = jnp.zeros((1, AUG), F32).at[0, :8].set(attn_b_f[0])
    qa, ka, vt = _inproj(x, row(norm_mix[0]), sh_m, sc_m, w_main, w_f, b_f)
    o = _attention(qa, ka, vt)

    w_ffn_in = ffn_w_in.astype(BF16)
    w_ffn_out = ffn_w_out.astype(BF16)
    x = _post(x, o, attn_w_out[0].astype(BF16), gt_m, row(norm_ffn[0]), sh_f, sc_f, gt_f,
              w_ffn_in, w_ffn_out, 0, g_final, final_norm=False)

    sh_m, sc_m, gt_m, sh_f, sc_f, gt_f = mods[1]
    pad_in = lambda w, n: jnp.zeros((D, n), BF16).at[:, :w.shape[1]].set(w.astype(BF16))
    pad_out = lambda w, n: jnp.zeros((n, D), BF16).at[:w.shape[0], :].set(w.astype(BF16))
    r, k, v, kkv, a, lw, gate = _rwkv_proj(
        x, row(norm_mix[1]), sh_m, sc_m, rwkv_mu[0],
        rwkv_w_r[0].astype(BF16), rwkv_w_k[0].astype(BF16), rwkv_w_v[0].astype(BF16),
        pad_in(rwkv_w1[0], 128), pad_out(rwkv_w2[0], 128),
        pad_in(rwkv_a1[0], 128), pad_out(rwkv_a2[0], 128),
        pad_in(rwkv_g1[0], 256), pad_out(rwkv_g2[0], 256),
        row(rwkv_w0[0]), row(rwkv_a0[0]), row(rwkv_k_k[0]), row(rwkv_k_a[0]))
    y = _rwkv_scan(r, k, v, kkv, a, lw, gate, row(rwkv_r_k[0]), row(rwkv_ln_w[0]),
                   row(rwkv_ln_b[0]))
    x = _post(x, y, rwkv_w_o[0].astype(BF16), gt_m, row(norm_ffn[1]), sh_f, sc_f, gt_f,
              w_ffn_in, w_ffn_out, 1, g_final, final_norm=True)
    return x
```
